```python
import math
import jax
import jax.numpy as jnp
from jax import lax
import numpy as np

D_MODEL = 1024
BATCH = 32
SEQ = 2048
DEPTH = 1
DEC_BATCH = 2
DEC_SEQ = 8192
PAST_LEN = 128

DA_HEADS = 8
DA_DIM = 64
DA_QK = DA_HEADS * 2 * DA_DIM
DA_V = DA_HEADS * 2 * DA_DIM
Q_BLOCK = 128
ROPE_THETA = 10000.0
ML_HEADS = 4
ML_DIM = 256
ML_W = ML_HEADS * ML_DIM
ML_CHUNK = 64
ML_N_GATES = 4 * ML_HEADS
MEM_TOKENS = 256
XA_HEADS = 4
XA_DIM = D_MODEL // XA_HEADS
N_EXPERTS = 16
CAPACITY_FACTOR = 2
D_EXPERT = 2048
EPS = 1e-6
SPLIT_SIZES = (DA_QK, DA_QK, DA_V, ML_W, ML_W, ML_W, ML_W, ML_N_GATES, 2 * D_MODEL)
D_IN = sum(SPLIT_SIZES)

kernel_name = "hybrid_diffattn_mlstm_ec_encoder"


def rmsnorm(x, g):
    xf = x.astype(jnp.float32)
    y = xf * lax.rsqrt(jnp.mean(xf * xf, axis=-1, keepdims=True) + EPS)
    return (y * g.astype(jnp.float32)).astype(x.dtype)


def rope(x, pos):
    half = x.shape[-1] // 2
    inv = ROPE_THETA ** (-jnp.arange(half, dtype=jnp.float32) / half)
    ang = pos[:, None] * inv[None, :]
    cos = jnp.cos(ang).astype(x.dtype)
    sin = jnp.sin(ang).astype(x.dtype)
    x1, x2 = x[..., :half], x[..., half:]
    return jnp.concatenate([x1 * cos - x2 * sin, x1 * sin + x2 * cos], axis=-1)


def diff_attention(q, k, v, lam):
    B, H, _, S, d = q.shape
    nb = S // Q_BLOCK
    qb = q.reshape(B, H, 2, nb, Q_BLOCK, d).transpose(3, 0, 1, 2, 4, 5)
    scale = d ** -0.5

    def one_block(qblk):
        s = jnp.einsum('bhmqd,bhmkd->bhmqk', qblk, k, preferred_element_type=jnp.float32) * scale
        p = jax.nn.softmax(s, axis=-1)
        a = p[:, :, 0] - lam * p[:, :, 1]
        return jnp.einsum('bhqk,bhkv->bhqv', a.astype(v.dtype), v)

    o = lax.map(one_block, qb)
    return o.transpose(1, 2, 0, 3, 4).reshape(B, H, S, 2 * d)


def mlstm_chunkwise(q, k, v, i_pre, f_pre):
    B, H, S, d = q.shape
    L = ML_CHUNK
    nc = S // L

    def to_chunks(t):
        return jnp.moveaxis(t.reshape(B, H, nc, L, *t.shape[3:]), 2, 0)

    qc, kc, vc = to_chunks(q), to_chunks(k), to_chunks(v)
    ic = to_chunks(i_pre)
    gc = jnp.cumsum(to_chunks(jax.nn.log_sigmoid(f_pre)), axis=-1)
    causal = jnp.tril(jnp.ones((L, L), dtype=bool))

    def step(carry, xs):
        C, n, m = carry
        qj, kj, vj, ij, gj = xs
        qf, kf, vf = qj.astype(jnp.float32), kj.astype(jnp.float32), vj.astype(jnp.float32)
        D = gj[..., :, None] - gj[..., None, :] + ij[..., None, :]
        D = jnp.where(causal, D, -jnp.inf)
        a = gj + m[..., None]
        mj = jnp.maximum(a, jnp.max(D, axis=-1))
        w_inter = jnp.exp(a - mj)
        s = jnp.einsum('bhld,bhsd->bhls', qf, kf) * jnp.exp(D - mj[..., None])
        num = (w_inter[..., None] * jnp.einsum('bhvk,bhlk->bhlv', C, qf)
               + jnp.einsum('bhls,bhsv->bhlv', s, vf))
        den = w_inter * jnp.einsum('bhk,bhlk->bhl', n, qf) + jnp.sum(s, axis=-1)
        h = num / jnp.maximum(jnp.abs(den), jnp.exp(-mj))[..., None]
        gL = gj[..., -1]
        dec = gL[..., None] - gj + ij
        m_new = jnp.maximum(gL + m, jnp.max(dec, axis=-1))
        wk = jnp.exp(dec - m_new[..., None])
        carry_scale = jnp.exp(gL + m - m_new)
        C_new = carry_scale[..., None, None] * C + jnp.einsum('bhl,bhlv,bhlk->bhvk', wk, vf, kf)
        n_new = carry_scale[..., None] * n + jnp.einsum('bhl,bhlk->bhk', wk, kf)
        return (C_new, n_new, m_new), h

    init = (jnp.zeros((B, H, d, d), jnp.float32),
            jnp.zeros((B, H, d), jnp.float32),
            jnp.zeros((B, H), jnp.float32))
    _, h = lax.scan(step, init, (qc, kc, vc, ic, gc))
    return jnp.moveaxis(h, 0, 2).reshape(B, H, S, d).astype(q.dtype)


def parallel_mixer(h, pos, p, l):
    B, S, _ = h.shape
    z = h @ p['w_in'][l]
    idx = [int(c) for c in np.cumsum(SPLIT_SIZES)[:-1]]
    da_q, da_k, da_v, ml_q, ml_k, ml_v, ml_o, ml_g, br_g = jnp.split(z, idx, axis=-1)

    q = rope(da_q.reshape(B, S, DA_HEADS, 2, DA_DIM).transpose(0, 2, 3, 1, 4), pos)
    k = rope(da_k.reshape(B, S, DA_HEADS, 2, DA_DIM).transpose(0, 2, 3, 1, 4), pos)
    v = da_v.reshape(B, S, DA_HEADS, 2 * DA_DIM).transpose(0, 2, 1, 3)
    lam_init = 0.8 - 0.6 * math.exp(-0.3 * l)
    f32 = jnp.float32
    lam = (jnp.exp(jnp.sum(p['lam_q1'][l].astype(f32) * p['lam_k1'][l].astype(f32)))
           - jnp.exp(jnp.sum(p['lam_q2'][l].astype(f32) * p['lam_k2'][l].astype(f32)))
           + lam_init)
    oa = diff_attention(q, k, v, lam)
    oa = rmsnorm(oa, p['g_da_out'][l]) * (1.0 - lam_init)
    y_a = oa.transpose(0, 2, 1, 3).reshape(B, S, DA_V) @ p['w_proj_a'][l]

    mq = ml_q.reshape(B, S, ML_HEADS, ML_DIM).transpose(0, 2, 1, 3)
    mk = ml_k.reshape(B, S, ML_HEADS, ML_DIM).transpose(0, 2, 1, 3) * (ML_DIM ** -0.5)
    mv = ml_v.reshape(B, S, ML_HEADS, ML_DIM).transpose(0, 2, 1, 3)
    gates = (ml_g.astype(f32) + p['b_ml_gates'][l].astype(f32)).reshape(B, S, 4, ML_HEADS).transpose(2, 0, 3, 1)
    i_fw, i_bw, f_fw, f_bw = gates[0], gates[1], gates[2], gates[3]
    h_fw = mlstm_chunkwise(mq, mk, mv, i_fw, f_fw)
    rev = lambda t: jnp.flip(t, axis=2)
    h_bw = rev(mlstm_chunkwise(rev(mq), rev(mk), rev(mv), rev(i_bw), rev(f_bw)))
    hb = rmsnorm(h_fw + h_bw, p['g_ml_out'][l].reshape(ML_HEADS, 1, ML_DIM))
    hb = hb.transpose(0, 2, 1, 3).reshape(B, S, ML_W) * jax.nn.sigmoid(ml_o)
    y_b = hb @ p['w_proj_b'][l]

    g = jax.nn.sigmoid(br_g)
    merged = g[..., :D_MODEL] * y_a + g[..., D_MODEL:] * y_b
    return merged @ p['w_out'][l]


def memory_cross_attention(h, mem, g_mem, w_xq, w_xkv, w_xo):
    B, S, _ = h.shape
    M = mem.shape[1]
    mn = rmsnorm(mem, g_mem)
    q = (h @ w_xq).reshape(B, S, XA_HEADS, XA_DIM)
    kv = (mn @ w_xkv).reshape(B, M, 2, XA_HEADS, XA_DIM)
    k, v = kv[:, :, 0], kv[:, :, 1]
    s = jnp.einsum('bqhd,bkhd->bhqk', q, k, preferred_element_type=jnp.float32) * (XA_DIM ** -0.5)
    pr = jax.nn.softmax(s, axis=-1).astype(v.dtype)
    o = jnp.einsum('bhqk,bkhd->bqhd', pr, v).reshape(B, S, D_MODEL)
    return o @ w_xo


def expert_choice_ffn(h, w_router, w_gate, w_up, w_down):
    B, S, D = h.shape
    N = B * S
    t = h.reshape(N, D)
    aff = jax.nn.softmax(jnp.matmul(t, w_router, preferred_element_type=jnp.float32), axis=-1)
    cap = CAPACITY_FACTOR * N // N_EXPERTS
    gval, tok = lax.top_k(aff.T, cap)
    xe = t[tok]
    hid = jax.nn.silu(jnp.einsum('ecd,edf->ecf', xe, w_gate)) * jnp.einsum('ecd,edf->ecf', xe, w_up)
    ye = jnp.einsum('ecf,efd->ecd', hid, w_down) * gval[..., None].astype(t.dtype)
    out = jnp.zeros((N, D), t.dtype).at[tok.reshape(-1)].add(ye.reshape(-1, D))
    return out.reshape(B, S, D)


def encoder(x, mem, p):
    pos = jnp.arange(x.shape[1], dtype=jnp.float32)
    for l in range(DEPTH):
        x = x + parallel_mixer(rmsnorm(x, p['g_mix'][l]), pos, p, l)
        x = x + memory_cross_attention(rmsnorm(x, p['g_xa'][l]), mem, p['g_mem'][l],
                                       p['w_xq'][l], p['w_xkv'][l], p['w_xo'][l])
        x = x + expert_choice_ffn(rmsnorm(x, p['g_ffn'][l]), p['w_router'][l],
                                  p['w_e_gate'][l], p['w_e_up'][l], p['w_e_down'][l])
    return rmsnorm(x, p['g_final'])


def setup_inputs(seed: int = 0) -> dict:
    key = jax.random.key(seed)
    ks = jax.random.split(key, 32)
    nrm = lambda k, shape, scale: scale * jax.random.normal(k, shape, jnp.float32)
    gain = lambda k, shape: 1.0 + 0.02 * jax.random.normal(k, shape, jnp.float32)
    f_bias = jnp.tile(jnp.linspace(3.0, 6.0, ML_HEADS, dtype=jnp.float32), 2)
    b_ml_gates = jnp.concatenate([
        nrm(ks[10], (DEPTH, 2 * ML_HEADS), 0.1),
        f_bias[None, :] + nrm(ks[11], (DEPTH, 2 * ML_HEADS), 0.1)], axis=-1)
    return {
        'x_prompt': nrm(ks[0], (BATCH, SEQ, D_MODEL), 1.0),
        'x_sample': nrm(ks[1], (DEC_BATCH, DEC_SEQ, D_MODEL), 1.0),
        'mem_prompt': nrm(ks[2], (BATCH, MEM_TOKENS, D_MODEL), 1.0),
        'mem_sample': nrm(ks[3], (DEC_BATCH, MEM_TOKENS, D_MODEL), 1.0),
        'g_mix': gain(ks[4], (DEPTH, D_MODEL)),
        'w_in': nrm(ks[5], (DEPTH, D_MODEL, D_IN), D_MODEL ** -0.5),
        'lam_q1': nrm(ks[6], (DEPTH, DA_DIM), 0.1),
        'lam_k1': nrm(ks[7], (DEPTH, DA_DIM), 0.1),
        'lam_q2': nrm(ks[8], (DEPTH, DA_DIM), 0.1),
        'lam_k2': nrm(ks[9], (DEPTH, DA_DIM), 0.1),
        'g_da_out': gain(ks[12], (DEPTH, 2 * DA_DIM)),
        'b_ml_gates': b_ml_gates,
        'g_ml_out': gain(ks[13], (DEPTH, ML_W)),
        'w_proj_a': nrm(ks[14], (DEPTH, DA_V, D_MODEL), DA_V ** -0.5),
        'w_proj_b': nrm(ks[15], (DEPTH, ML_W, D_MODEL), ML_W ** -0.5),
        'w_out': nrm(ks[16], (DEPTH, D_MODEL, D_MODEL), D_MODEL ** -0.5),
        'g_xa': gain(ks[17], (DEPTH, D_MODEL)),
        'g_mem': gain(ks[18], (DEPTH, D_MODEL)),
        'w_xq': nrm(ks[19], (DEPTH, D_MODEL, D_MODEL), D_MODEL ** -0.5),
        'w_xkv': nrm(ks[20], (DEPTH, D_MODEL, 2 * D_MODEL), D_MODEL ** -0.5),
        'w_xo': nrm(ks[21], (DEPTH, D_MODEL, D_MODEL), D_MODEL ** -0.5),
        'g_ffn': gain(ks[22], (DEPTH, D_MODEL)),
        'w_router': nrm(ks[23], (DEPTH, D_MODEL, N_EXPERTS), D_MODEL ** -0.5),
        'w_e_gate': nrm(ks[24], (DEPTH, N_EXPERTS, D_MODEL, D_EXPERT), D_MODEL ** -0.5),
        'w_e_up': nrm(ks[25], (DEPTH, N_EXPERTS, D_MODEL, D_EXPERT), D_MODEL ** -0.5),
        'w_e_down': nrm(ks[26], (DEPTH, N_EXPERTS, D_EXPERT, D_MODEL), D_EXPERT ** -0.5),
        'g_final': gain(ks[27], (D_MODEL,)),
    }


def reference(x_prompt, x_sample, mem_prompt, mem_sample, g_mix, w_in, lam_q1, lam_k1, lam_q2, lam_k2,
              g_da_out, b_ml_gates, g_ml_out, w_proj_a, w_proj_b, w_out, g_xa, g_mem, w_xq, w_xkv, w_xo,
              g_ffn, w_router, w_e_gate, w_e_up, w_e_down, g_final):
    params = dict(g_mix=g_mix, w_in=w_in, lam_q1=lam_q1, lam_k1=lam_k1, lam_q2=lam_q2, lam_k2=lam_k2,
                  g_da_out=g_da_out, b_ml_gates=b_ml_gates, g_ml_out=g_ml_out, w_proj_a=w_proj_a,
                  w_proj_b=w_proj_b, w_out=w_out, g_xa=g_xa, g_mem=g_mem, w_xq=w_xq, w_xkv=w_xkv,
                  w_xo=w_xo, g_ffn=g_ffn, w_router=w_router, w_e_gate=w_e_gate, w_e_up=w_e_up,
                  w_e_down=w_e_down, g_final=g_final)
    y_prompt = encoder(x_prompt, mem_prompt, params)
    y_sample = encoder(x_sample, mem_sample, params)
    return (y_prompt, y_sample)
```

```python
import functools
import math

import jax
import jax.numpy as jnp
from jax import lax
from jax.experimental import pallas as pl
from jax.experimental.pallas import tpu as pltpu

F32 = jnp.float32
BF16 = jnp.bfloat16
I32 = jnp.int32

D_MODEL = 1024
DA_HEADS = 8
DA_DIM = 64
DA_HEAD_W = 2 * DA_DIM
ML_HEADS = 4
ML_DIM = 256
XA_HEADS = 4
XA_DIM = D_MODEL // XA_HEADS
N_EXPERTS = 16
CAPACITY_FACTOR = 2
D_EXPERT = 2048
EPS = 1e-6
ROPE_THETA = 10000.0
N_GATES = 4 * ML_HEADS

Z_DA_Q = 0
Z_DA_K = 1024
Z_DA_V = 2048
Z_ML_Q = 3072
Z_ML_K = 4096
Z_ML_V = 5120
Z_ML_O = 6144
Z_BR_G = 7168
Z_WIDTH = 9216

LANES = 128
VMEM_LIMIT = 56 * 1024 * 1024

TM_PROJ = 1024
TN_PROJ = 512
TQ_ATT = 256
TK_ATT = 512
L_MLSTM = 256
TM_MIX = 256
T_DISP = 1024
B_SLOT = 256
TM_FFN = 512
F_CHUNK = 512
T_COMB = 256


def _cparams(sem):
    return pltpu.CompilerParams(dimension_semantics=sem, vmem_limit_bytes=VMEM_LIMIT)


def _rms(xf, g):
    return xf * lax.rsqrt(jnp.mean(xf * xf, axis=-1, keepdims=True) + EPS) * g


def _inproj_kernel(x_ref, g_ref, w_ref, cs_ref, cos_ref, sin_ref, z_ref, hn_ref, hs_ref, *, n_rope):
    j = pl.program_id(1)

    @pl.when(j == 0)
    def _():
        hb = _rms(x_ref[...], g_ref[...]).astype(BF16)
        hs_ref[...] = hb
        hn_ref[...] = hb

    acc = jnp.dot(hs_ref[...], w_ref[...], preferred_element_type=F32) * cs_ref[...]

    @pl.when(j >= n_rope)
    def _():
        z_ref[...] = acc.astype(BF16)

    @pl.when(j < n_rope)
    def _():
        cos = cos_ref[...]
        sin = sin_ref[...]
        lane = lax.broadcasted_iota(I32, cos.shape, 1)
        first_half = (lane % DA_DIM) < (DA_DIM // 2)
        for hh in range(acc.shape[1] // LANES):
            a = acc[:, hh * LANES:(hh + 1) * LANES]
            rot = jnp.where(first_half, pltpu.roll(a, LANES - DA_DIM // 2, 1),
                            pltpu.roll(a, DA_DIM // 2, 1))
            z_ref[:, hh * LANES:(hh + 1) * LANES] = (a * cos + rot * sin).astype(BF16)


def _inproj(x2d, g_mix, w_main, colscale, cos_t, sin_t, seq):
    n = x2d.shape[0]
    tm, tn = TM_PROJ, TN_PROJ
    n_seq_tiles = seq // tm
    n_rope = (2 * D_MODEL) // tn
    return pl.pallas_call(
        functools.partial(_inproj_kernel, n_rope=n_rope),
        grid=(n // tm, Z_WIDTH // tn),
        in_specs=[
            pl.BlockSpec((tm, D_MODEL), lambda i, j: (i, 0)),
            pl.BlockSpec((1, D_MODEL), lambda i, j: (0, 0)),
            pl.BlockSpec((D_MODEL, tn), lambda i, j: (0, j)),
            pl.BlockSpec((1, tn), lambda i, j: (0, j)),
            pl.BlockSpec((tm, LANES), lambda i, j: (i % n_seq_tiles, 0)),
            pl.BlockSpec((tm, LANES), lambda i, j: (i % n_seq_tiles, 0)),
        ],
        out_specs=[
            pl.BlockSpec((tm, tn), lambda i, j: (i, j)),
            pl.BlockSpec((tm, D_MODEL), lambda i, j: (i, 0)),
        ],
        out_shape=[jax.ShapeDtypeStruct((n, Z_WIDTH), BF16),
                   jax.ShapeDtypeStruct((n, D_MODEL), BF16)],
        scratch_shapes=[pltpu.VMEM((tm, D_MODEL), BF16)],
        compiler_params=_cparams(("parallel", "arbitrary")),
        name="inproj",
    )(x2d, g_mix, w_main, colscale, cos_t, sin_t)


def _ktrans_kernel(w_ref, h_ref, o_ref):
    kt = lax.dot_general(w_ref[...], h_ref[...], (((1,), (1,)), ((), ())),
                         preferred_element_type=F32)
    o_ref[...] = (kt * (ML_DIM ** -0.5)).astype(BF16)


def _ktrans(hn, w_kt):
    n = hn.shape[0]
    tm = TM_PROJ
    return pl.pallas_call(
        _ktrans_kernel,
        grid=(n // tm, ML_HEADS),
        in_specs=[pl.BlockSpec((ML_DIM, D_MODEL), lambda i, h: (h, 0)),
                  pl.BlockSpec((tm, D_MODEL), lambda i, h: (i, 0))],
        out_specs=pl.BlockSpec((ML_DIM, tm), lambda i, h: (h, i)),
        out_shape=jax.ShapeDtypeStruct((ML_HEADS * ML_DIM, n), BF16),
        compiler_params=_cparams(("parallel", "arbitrary")),
        name="ml_k_transposed",
    )(w_kt, hn)


def _gates_kernel(h_ref, w_ref, wt_ref, b_ref, bt_ref, g_ref, gt_ref):
    h = h_ref[...]
    g_ref[...] = jnp.dot(h, w_ref[...], preferred_element_type=F32) + b_ref[...]
    gt_ref[...] = lax.dot_general(wt_ref[...], h, (((1,), (1,)), ((), ())),
                                  preferred_element_type=F32) + bt_ref[...]


def _gates(hn, w_g, w_gt, b_g, b_gt):
    n = hn.shape[0]
    tm = TM_PROJ
    return pl.pallas_call(
        _gates_kernel,
        grid=(n // tm,),
        in_specs=[pl.BlockSpec((tm, D_MODEL), lambda i: (i, 0)),
                  pl.BlockSpec((D_MODEL, LANES), lambda i: (0, 0)),
                  pl.BlockSpec((N_GATES, D_MODEL), lambda i: (0, 0)),
                  pl.BlockSpec((1, LANES), lambda i: (0, 0)),
                  pl.BlockSpec((N_GATES, 1), lambda i: (0, 0))],
        out_specs=[pl.BlockSpec((tm, LANES), lambda i: (i, 0)),
                   pl.BlockSpec((N_GATES, tm), lambda i: (0, i))],
        out_shape=[jax.ShapeDtypeStruct((n, LANES), F32),
                   jax.ShapeDtypeStruct((N_GATES, n), F32)],
        compiler_params=_cparams(("parallel",)),
        name="ml_gates",
    )(hn, w_g, w_gt, b_g, b_gt)


def _diffattn_kernel(q_ref, k_ref, v_ref, lamv_ref, g_ref, o_ref, *, seq, lam_init):
    tq = q_ref.shape[0]
    tk = min(TK_ATT, seq)
    q = q_ref[...]
    lane = lax.broadcasted_iota(I32, q.shape, 1)
    zero = jnp.zeros_like(q)
    q0 = jnp.where(lane < DA_DIM, q, zero)
    q1 = jnp.where(lane >= DA_DIM, q, zero)

    def body(c, carry):
        m0, l0, a0, m1, l1, a1 = carry
        start = pl.multiple_of(c * tk, tk)
        kc = k_ref[pl.ds(start, tk), :]
        vc = v_ref[pl.ds(start, tk), :]

        def one(qm, m, l, a):
            s = lax.dot_general(qm, kc, (((1,), (1,)), ((), ())), preferred_element_type=F32)
            mn = jnp.maximum(m, jnp.max(s, axis=-1, keepdims=True))
            alpha = jnp.exp(m - mn)
            p = jnp.exp(s - mn)
            l = alpha * l + jnp.sum(p, axis=-1, keepdims=True)
            a = alpha * a + jnp.dot(p.astype(BF16), vc, preferred_element_type=F32)
            return mn, l, a

        m0, l0, a0 = one(q0, m0, l0, a0)
        m1, l1, a1 = one(q1, m1, l1, a1)
        return m0, l0, a0, m1, l1, a1

    neg = jnp.full((tq, 1), -jnp.inf, F32)
    zl = jnp.zeros((tq, 1), F32)
    za = jnp.zeros((tq, DA_HEAD_W), F32)
    m0, l0, a0, m1, l1, a1 = lax.fori_loop(0, seq // tk, body, (neg, zl, za, neg, zl, za))

    lv = lamv_ref[...]
    lam = (jnp.exp(jnp.sum(lv[0:1, :] * lv[1:2, :], axis=-1, keepdims=True))
           - jnp.exp(jnp.sum(lv[2:3, :] * lv[3:4, :], axis=-1, keepdims=True)) + lam_init)
    o = a0 / l0 - lam * (a1 / l1)
    o_ref[...] = (_rms(o, g_ref[...]) * (1.0 - lam_init)).astype(BF16)


def _diffattn(z, lamv, g_da, batch, seq, lam_init):
    n = z.shape[0]
    tq = TQ_ATT
    nq = seq // tq
    hq = Z_DA_Q // DA_HEAD_W
    hk = Z_DA_K // DA_HEAD_W
    hv = Z_DA_V // DA_HEAD_W
    return pl.pallas_call(
        functools.partial(_diffattn_kernel, seq=seq, lam_init=lam_init),
        grid=(batch, DA_HEADS, nq),
        in_specs=[
            pl.BlockSpec((tq, DA_HEAD_W), lambda b, h, i: (b * nq + i, hq + h)),
            pl.BlockSpec((seq, DA_HEAD_W), lambda b, h, i: (b, hk + h)),
            pl.BlockSpec((seq, DA_HEAD_W), lambda b, h, i: (b, hv + h)),
            pl.BlockSpec((4, DA_DIM), lambda b, h, i: (0, 0)),
            pl.BlockSpec((1, DA_HEAD_W), lambda b, h, i: (0, 0)),
        ],
        out_specs=pl.BlockSpec((tq, DA_HEAD_W), lambda b, h, i: (b * nq + i, h)),
        out_shape=jax.ShapeDtypeStruct((n, DA_HEADS * DA_HEAD_W), BF16),
        compiler_params=_cparams(("parallel", "parallel", "arbitrary")),
        name="diff_attention",
    )(z, z, z, lamv, g_da)


def _log_sigmoid(x):
    return jnp.minimum(x, 0.0) - jnp.log1p(jnp.exp(-jnp.abs(x)))


def _mlstm_kernel(q_ref, k_ref, kt_ref, v_ref, g_ref, gt_ref, h_ref, ct_ref, n_ref, m_ref, *, reverse):
    hh = pl.program_id(1)
    t = pl.program_id(2)
    L = q_ref.shape[0]
    hi = lax.Precision.HIGHEST

    @pl.when(t == 0)
    def _():
        ct_ref[...] = jnp.zeros_like(ct_ref)
        n_ref[...] = jnp.zeros_like(n_ref)
        m_ref[...] = jnp.zeros_like(m_ref)

    col_i = hh + (ML_HEADS if reverse else 0)
    col_f = col_i + 2 * ML_HEADS
    g = g_ref[...]
    gt = gt_ref[...]
    lane = lax.broadcasted_iota(I32, g.shape, 1)
    sub = lax.broadcasted_iota(I32, gt.shape, 0)
    i_col = jnp.sum(jnp.where(lane == col_i, g, 0.0), axis=1, keepdims=True)
    f_col = jnp.sum(jnp.where(lane == col_f, g, 0.0), axis=1, keepdims=True)
    i_row = jnp.sum(jnp.where(sub == col_i, gt, 0.0), axis=0, keepdims=True)
    f_row = jnp.sum(jnp.where(sub == col_f, gt, 0.0), axis=0, keepdims=True)
    lf_col = _log_sigmoid(f_col)
    lf_row = _log_sigmoid(f_row)

    ri = lax.broadcasted_iota(I32, (L, L), 0)
    ci = lax.broadcasted_iota(I32, (L, L), 1)
    tri = (ci >= ri) if reverse else (ci <= ri)
    tri_f = tri.astype(F32)
    g_col = jnp.dot(tri_f, jnp.broadcast_to(lf_col, (L, LANES)), precision=hi,
                    preferred_element_type=F32)[:, 0:1]
    g_row = lax.dot_general(jnp.broadcast_to(lf_row, (8, L)), tri_f, (((1,), (1,)), ((), ())),
                            precision=hi, preferred_element_type=F32)[0:1, :]
    last = 0 if reverse else L - 1
    lane_l = lax.broadcasted_iota(I32, (1, L), 1)
    g_last = jnp.sum(jnp.where(lane_l == last, g_row, 0.0), axis=1, keepdims=True)

    m_prev = m_ref[...]
    dmat = jnp.where(tri, g_col - g_row + i_row, -jnp.inf)
    a_col = g_col + m_prev
    mj = jnp.maximum(a_col, jnp.max(dmat, axis=1, keepdims=True))
    w_inter = jnp.exp(a_col - mj)
    q = q_ref[...]
    k = k_ref[...]
    kt = kt_ref[...]
    v = v_ref[...]
    s = jnp.dot(q, kt, preferred_element_type=F32) * jnp.exp(dmat - mj)
    ct = ct_ref[...]
    n_row = n_ref[...]
    num = (w_inter * jnp.dot(q, ct.astype(BF16), preferred_element_type=F32)
           + jnp.dot(s.astype(BF16), v, preferred_element_type=F32))
    den = (w_inter * jnp.sum(q.astype(F32) * n_row, axis=1, keepdims=True)
           + jnp.sum(s, axis=1, keepdims=True))
    h_ref[...] = (num / jnp.maximum(jnp.abs(den), jnp.exp(-mj))).astype(BF16)

    dec_row = g_last - g_row + i_row
    dec_col = g_last - g_col + i_col
    m_new = jnp.maximum(g_last + m_prev, jnp.max(dec_row, axis=1, keepdims=True))
    wk_col = jnp.exp(dec_col - m_new)
    carry_scale = jnp.exp(g_last + m_prev - m_new)
    wv = (wk_col * v.astype(F32)).astype(BF16)
    ct_ref[...] = carry_scale * ct + jnp.dot(kt, wv, preferred_element_type=F32)
    n_ref[...] = carry_scale * n_row + jnp.sum(wk_col * k.astype(F32), axis=0, keepdims=True)
    m_ref[...] = m_new


def _mlstm(z, kt, gates, gates_t, batch, seq, reverse):
    n = z.shape[0]
    L = L_MLSTM
    nc = seq // L
    cq = Z_ML_Q // ML_DIM
    ck = Z_ML_K // ML_DIM
    cv = Z_ML_V // ML_DIM

    def tix(t):
        return (nc - 1 - t) if reverse else t

    return pl.pallas_call(
        functools.partial(_mlstm_kernel, reverse=reverse),
        grid=(batch, ML_HEADS, nc),
        in_specs=[
            pl.BlockSpec((L, ML_DIM), lambda b, h, t: (b * nc + tix(t), cq + h)),
            pl.BlockSpec((L, ML_DIM), lambda b, h, t: (b * nc + tix(t), ck + h)),
            pl.BlockSpec((ML_DIM, L), lambda b, h, t: (h, b * nc + tix(t))),
            pl.BlockSpec((L, ML_DIM), lambda b, h, t: (b * nc + tix(t), cv + h)),
            pl.BlockSpec((L, LANES), lambda b, h, t: (b * nc + tix(t), 0)),
            pl.BlockSpec((N_GATES, L), lambda b, h, t: (0, b * nc + tix(t))),
        ],
        out_specs=pl.BlockSpec((L, ML_DIM), lambda b, h, t: (b * nc + tix(t), h)),
        out_shape=jax.ShapeDtypeStruct((n, ML_HEADS * ML_DIM), BF16),
        scratch_shapes=[pltpu.VMEM((ML_DIM, ML_DIM), F32),
                        pltpu.VMEM((1, ML_DIM), F32),
                        pltpu.VMEM((1, 1), F32)],
        compiler_params=_cparams(("parallel", "parallel", "arbitrary")),
        name="mlstm_bwd" if reverse else "mlstm_fwd",
    )(z, z, kt, z, gates, gates_t)


def _memkv_kernel(m_ref, g_ref, w_ref, o_ref):
    mn = _rms(m_ref[...], g_ref[...]).astype(BF16)
    o_ref[...] = jnp.dot(mn, w_ref[...], preferred_element_type=F32).astype(BF16)


def _memkv(mem2d, g_mem, w_xkv, mem_tokens):
    n = mem2d.shape[0]
    return pl.pallas_call(
        _memkv_kernel,
        grid=(n // mem_tokens,),
        in_specs=[pl.BlockSpec((mem_tokens, D_MODEL), lambda b: (b, 0)),
                  pl.BlockSpec((1, D_MODEL), lambda b: (0, 0)),
                  pl.BlockSpec((D_MODEL, 2 * D_MODEL), lambda b: (0, 0))],
        out_specs=pl.BlockSpec((mem_tokens, 2 * D_MODEL), lambda b: (b, 0)),
        out_shape=jax.ShapeDtypeStruct((n, 2 * D_MODEL), BF16),
        compiler_params=_cparams(("parallel",)),
        name="mem_kv",
    )(mem2d, g_mem, w_xkv)


def _sigmoid(x):
    return 1.0 / (1.0 + jnp.exp(-x))


def _mix_kernel(x_ref, oa_ref, hf_ref, hb_ref, mo_ref, bga_ref, bgb_ref, kv_ref,
                gml_ref, gxa_ref, gffn_ref, wpa_ref, wpb_ref, wo_ref, wxq_ref, wxo_ref, wrt_ref,
                x2_ref, hn_ref, afft_ref):
    hsum = hf_ref[...].astype(F32) + hb_ref[...].astype(F32)
    gml = gml_ref[...]
    parts = []
    for hh in range(ML_HEADS):
        sl = slice(hh * ML_DIM, (hh + 1) * ML_DIM)
        parts.append(_rms(hsum[:, sl], gml[:, sl]))
    hb = jnp.concatenate(parts, axis=1)
    hb = hb * _sigmoid(mo_ref[...].astype(F32))
    y_a = jnp.dot(oa_ref[...], wpa_ref[...], preferred_element_type=F32)
    y_b = jnp.dot(hb.astype(BF16), wpb_ref[...], preferred_element_type=F32)
    merged = (_sigmoid(bga_ref[...].astype(F32)) * y_a
              + _sigmoid(bgb_ref[...].astype(F32)) * y_b)
    x1 = x_ref[...] + jnp.dot(merged.astype(BF16), wo_ref[...], preferred_element_type=F32)

    hq = _rms(x1, gxa_ref[...]).astype(BF16)
    qx = (jnp.dot(hq, wxq_ref[...], preferred_element_type=F32) * (XA_DIM ** -0.5)).astype(BF16)
    kv = kv_ref[...]
    outs = []
    for hh in range(XA_HEADS):
        sl = slice(hh * XA_DIM, (hh + 1) * XA_DIM)
        kh = kv[:, sl]
        vh = kv[:, D_MODEL + hh * XA_DIM:D_MODEL + (hh + 1) * XA_DIM]
        s = lax.dot_general(qx[:, sl], kh, (((1,), (1,)), ((), ())), preferred_element_type=F32)
        p = jnp.exp(s - jnp.max(s, axis=-1, keepdims=True))
        p = p / jnp.sum(p, axis=-1, keepdims=True)
        outs.append(jnp.dot(p.astype(BF16), vh, preferred_element_type=F32).astype(BF16))
    o = jnp.concatenate(outs, axis=1)
    x2 = x1 + jnp.dot(o, wxo_ref[...], preferred_element_type=F32)
    x2_ref[...] = x2

    hn = _rms(x2, gffn_ref[...])
    hn_ref[...] = hn.astype(BF16)
    logits_t = lax.dot_general(wrt_ref[...], hn, (((1,), (1,)), ((), ())),
                               precision=lax.Precision.HIGHEST, preferred_element_type=F32)
    e = jnp.exp(logits_t - jnp.max(logits_t, axis=0, keepdims=True))
    afft_ref[...] = e / jnp.sum(e, axis=0, keepdims=True)


def _mix(x2d, oa, h_fw, h_bw, z, kv, p, batch, seq, mem_tokens):
    n = x2d.shape[0]
    tm = TM_MIX
    nt = seq // tm
    cmo = Z_ML_O // D_MODEL
    cbg = Z_BR_G // D_MODEL
    full = lambda r, c: pl.BlockSpec((r, c), lambda i: (0, 0))
    return pl.pallas_call(
        _mix_kernel,
        grid=(n // tm,),
        in_specs=[
            pl.BlockSpec((tm, D_MODEL), lambda i: (i, 0)),
            pl.BlockSpec((tm, D_MODEL), lambda i: (i, 0)),
            pl.BlockSpec((tm, D_MODEL), lambda i: (i, 0)),
            pl.BlockSpec((tm, D_MODEL), lambda i: (i, 0)),
            pl.BlockSpec((tm, D_MODEL), lambda i: (i, cmo)),
            pl.BlockSpec((tm, D_MODEL), lambda i: (i, cbg)),
            pl.BlockSpec((tm, D_MODEL), lambda i: (i, cbg + 1)),
            pl.BlockSpec((mem_tokens, 2 * D_MODEL), lambda i: (i // nt, 0)),
            full(1, D_MODEL), full(1, D_MODEL), full(1, D_MODEL),
            full(D_MODEL, D_MODEL), full(D_MODEL, D_MODEL), full(D_MODEL, D_MODEL),
            full(D_MODEL, D_MODEL), full(D_MODEL, D_MODEL),
            full(N_EXPERTS, D_MODEL),
        ],
        out_specs=[pl.BlockSpec((tm, D_MODEL), lambda i: (i, 0)),
                   pl.BlockSpec((tm, D_MODEL), lambda i: (i, 0)),
                   pl.BlockSpec((N_EXPERTS, tm), lambda i: (0, i))],
        out_shape=[jax.ShapeDtypeStruct((n, D_MODEL), F32),
                   jax.ShapeDtypeStruct((n, D_MODEL), BF16),
                   jax.ShapeDtypeStruct((N_EXPERTS, n), F32)],
        compiler_params=_cparams(("parallel",)),
        name="mix_xattn_router",
    )(x2d, oa, h_fw, h_bw, z, z, z, kv,
      p["g_ml_out"], p["g_xa"], p["g_ffn"], p["w_proj_a"], p["w_proj_b"], p["w_out"],
      p["w_xq"], p["w_xo"], p["w_router_t"])


def _prefix_incl(mask_f, upper, ones, lower):
    mb = mask_f.astype(BF16)
    lane_incl = jnp.dot(mb, upper, preferred_element_type=F32)
    row_tot = jnp.dot(mb, ones, preferred_element_type=F32)
    row_off = jnp.dot(lower, row_tot.astype(BF16), preferred_element_type=F32)
    return lane_incl + row_off, row_off


def _select_kernel(a_ref, pos_ref, off_ref, *, cap):
    a = a_ref[...]
    rows = a.shape[0]
    bits = pltpu.bitcast(a, I32)

    def body(i, thr):
        cand = thr | jnp.left_shift(jnp.int32(1), 30 - i)
        cnt = jnp.sum(jnp.where(bits >= cand, 1.0, 0.0), keepdims=True)
        return jnp.where(cnt >= cap, cand, thr)

    thr = lax.fori_loop(0, 31, body, jnp.zeros((1, 1), I32))
    gt = bits > thr
    eq = bits == thr
    n_gt = jnp.sum(jnp.where(gt, 1.0, 0.0), keepdims=True)
    need = cap - n_gt

    li = lax.broadcasted_iota(I32, (LANES, LANES), 0)
    lj = lax.broadcasted_iota(I32, (LANES, LANES), 1)
    upper = (li <= lj).astype(BF16)
    ones = jnp.ones((LANES, LANES), BF16)
    ri = lax.broadcasted_iota(I32, (rows, rows), 0)
    rj = lax.broadcasted_iota(I32, (rows, rows), 1)
    lower = (rj < ri).astype(BF16)

    eq_rank, _ = _prefix_incl(jnp.where(eq, 1.0, 0.0), upper, ones, lower)
    sel = gt | (eq & (eq_rank <= need))
    incl, row_off = _prefix_incl(jnp.where(sel, 1.0, 0.0), upper, ones, lower)
    pos_ref[...] = jnp.where(sel, incl - 1.0, -1.0).astype(I32)
    off_ref[...] = row_off.astype(I32)


def _select(aff_t, n_tokens, cap):
    rows = n_tokens // LANES
    a3 = aff_t.reshape(N_EXPERTS, rows, LANES)
    spec = pl.BlockSpec((None, rows, LANES), lambda e: (e, 0, 0))
    return pl.pallas_call(
        functools.partial(_select_kernel, cap=float(cap)),
        grid=(N_EXPERTS,),
        in_specs=[spec],
        out_specs=[spec, spec],
        out_shape=[jax.ShapeDtypeStruct((N_EXPERTS, rows, LANES), I32),
                   jax.ShapeDtypeStruct((N_EXPERTS, rows, LANES), I32)],
        compiler_params=_cparams(("parallel",)),
        name="expert_choice_select",
    )(a3)


def _dispatch_kernel(tile_ref, blk_ref, flag_ref, x_ref, pos_ref, o_ref, *, n_items):
    e = pl.program_id(0)
    w = pl.program_id(1)
    idx = e * n_items + w
    flags = flag_ref[idx]
    blk = blk_ref[idx]

    @pl.when((flags & 2) != 0)
    def _():
        o_ref[...] = jnp.zeros_like(o_ref)

    @pl.when((flags & 1) != 0)
    def _():
        rel = pos_ref[...] - blk * B_SLOT
        slot = lax.broadcasted_iota(I32, (B_SLOT, rel.shape[1]), 0)
        onehot = jnp.where(slot == rel, 1.0, 0.0).astype(BF16)
        rows = jnp.dot(onehot, x_ref[...], preferred_element_type=F32)
        o_ref[...] = (o_ref[...].astype(F32) + rows).astype(BF16)


def _dispatch(hn, pos_t3, tile_of, blk_of, flags, cap, n_items):
    n = hn.shape[0]
    t = min(T_DISP, n)
    grid_spec = pltpu.PrefetchScalarGridSpec(
        num_scalar_prefetch=3,
        grid=(N_EXPERTS, n_items),
        in_specs=[
            pl.BlockSpec((t, D_MODEL), lambda e, w, tl, bl, fl: (tl[e * n_items + w], 0)),
            pl.BlockSpec((None, 1, t), lambda e, w, tl, bl, fl: (e, 0, tl[e * n_items + w])),
        ],
        out_specs=pl.BlockSpec((None, B_SLOT, D_MODEL),
                               lambda e, w, tl, bl, fl: (e, bl[e * n_items + w], 0)),
    )
    return pl.pallas_call(
        functools.partial(_dispatch_kernel, n_items=n_items),
        grid_spec=grid_spec,
        out_shape=jax.ShapeDtypeStruct((N_EXPERTS, cap, D_MODEL), BF16),
        compiler_params=_cparams(("arbitrary", "arbitrary")),
        name="dispatch",
    )(tile_of, blk_of, flags, hn, pos_t3)


def _dispatch_tables(row_off, n_tokens, cap):
    t = min(T_DISP, n_tokens)
    n_tiles = n_tokens // t
    n_blocks = cap // B_SLOT
    n_items = n_tiles + n_blocks
    start = row_off[:, ::t // LANES, 0]
    end = jnp.concatenate([start[:, 1:], jnp.full((N_EXPERTS, 1), cap, I32)], axis=1)
    nonempty = end > start
    b_lo = jnp.minimum(start // B_SLOT, n_blocks - 1)
    b_hi = jnp.maximum(end - 1, 0) // B_SLOT
    npairs = jnp.where(nonempty, b_hi - b_lo + 1, 0)
    cum = jnp.cumsum(npairs, axis=1)
    base = cum - npairs
    total = cum[:, -1:]
    w = jnp.arange(n_items, dtype=I32)[None, :]
    tile = jnp.sum((cum[:, None, :] <= w[:, :, None]).astype(I32), axis=-1)
    tile = jnp.minimum(tile, n_tiles - 1)
    blk = jnp.take_along_axis(b_lo, tile, axis=1) + (w - jnp.take_along_axis(base, tile, axis=1))
    valid = w < total
    last_tile = jnp.take_along_axis(tile, jnp.maximum(total - 1, 0), axis=1)
    tile = jnp.where(valid, tile, last_tile)
    blk = jnp.where(valid, blk, n_blocks - 1)
    prev = jnp.concatenate([jnp.full((N_EXPERTS, 1), -1, I32), blk[:, :-1]], axis=1)
    flags = valid.astype(I32) + 2 * (blk != prev).astype(I32)
    return (tile.reshape(-1).astype(I32), blk.reshape(-1).astype(I32),
            flags.reshape(-1).astype(I32), n_items)


def _ffn_kernel(x_ref, wg_ref, wu_ref, wd_ref, o_ref):
    x = x_ref[...]
    acc = jnp.zeros((x.shape[0], D_MODEL), F32)
    for c in range(D_EXPERT // F_CHUNK):
        sl = slice(c * F_CHUNK, (c + 1) * F_CHUNK)
        gte = jnp.dot(x, wg_ref[:, sl], preferred_element_type=F32)
        up = jnp.dot(x, wu_ref[:, sl], preferred_element_type=F32)
        hid = gte * (1.0 / (1.0 + jnp.exp(-gte))) * up
        acc = acc + jnp.dot(hid.astype(BF16), wd_ref[sl, :], preferred_element_type=F32)
    o_ref[...] = acc.astype(BF16)


def _ffn(xe, w_gate, w_up, w_down, cap):
    tm = min(TM_FFN, cap)
    return pl.pallas_call(
        _ffn_kernel,
        grid=(N_EXPERTS, cap // tm),
        in_specs=[pl.BlockSpec((None, tm, D_MODEL), lambda e, i: (e, i, 0)),
                  pl.BlockSpec((None, D_MODEL, D_EXPERT), lambda e, i: (e, 0, 0)),
                  pl.BlockSpec((None, D_MODEL, D_EXPERT), lambda e, i: (e, 0, 0)),
                  pl.BlockSpec((None, D_EXPERT, D_MODEL), lambda e, i: (e, 0, 0))],
        out_specs=pl.BlockSpec((None, tm, D_MODEL), lambda e, i: (e, i, 0)),
        out_shape=jax.ShapeDtypeStruct((N_EXPERTS, cap, D_MODEL), BF16),
        compiler_params=_cparams(("parallel", "arbitrary")),
        name="expert_ffn",
    )(xe, w_gate, w_up, w_down)


def _combine_kernel(b0_ref, x_ref, pos_ref, aff_ref, g_ref, *rest, n_blocks):
    ye_refs = rest[:2 * N_EXPERTS]
    o_ref = rest[2 * N_EXPERTS]
    t = pl.program_id(0)
    pos = pos_ref[...]
    aff = aff_ref[...]
    tt = pos.shape[0]
    acc = x_ref[...]
    slot = lax.broadcasted_iota(I32, (tt, B_SLOT), 1)
    for e in range(N_EXPERTS):
        b0 = b0_ref[t * N_EXPERTS + e]
        pe = pos[:, e:e + 1]
        ae = aff[:, e:e + 1]
        for kk in range(2):
            blk = jnp.minimum(b0 + kk, n_blocks - 1)
            onehot = jnp.where(slot == pe - blk * B_SLOT, 1.0, 0.0).astype(BF16)
            contrib = jnp.dot(onehot, ye_refs[2 * e + kk][...], preferred_element_type=F32)
            if kk == 0:
                acc = acc + contrib * ae
            else:
                acc = acc + jnp.where(b0 + 1 <= n_blocks - 1, contrib * ae, 0.0)
    o_ref[...] = _rms(acc, g_ref[...])


def _combine(x2, pos_c, aff_c, ye, b0, g_final, cap):
    n = x2.shape[0]
    tt = T_COMB
    n_blocks = cap // B_SLOT

    def ye_spec(e, kk):
        return pl.BlockSpec(
            (None, B_SLOT, D_MODEL),
            lambda t, b0r: (e, jnp.minimum(b0r[t * N_EXPERTS + e] + kk, n_blocks - 1), 0))

    ye_specs = [ye_spec(e, kk) for e in range(N_EXPERTS) for kk in range(2)]
    grid_spec = pltpu.PrefetchScalarGridSpec(
        num_scalar_prefetch=1,
        grid=(n // tt,),
        in_specs=[pl.BlockSpec((tt, D_MODEL), lambda t, b0r: (t, 0)),
                  pl.BlockSpec((tt, N_EXPERTS), lambda t, b0r: (t, 0)),
                  pl.BlockSpec((tt, N_EXPERTS), lambda t, b0r: (t, 0)),
                  pl.BlockSpec((1, D_MODEL), lambda t, b0r: (0, 0))] + ye_specs,
        out_specs=pl.BlockSpec((tt, D_MODEL), lambda t, b0r: (t, 0)),
    )
    return pl.pallas_call(
        functools.partial(_combine_kernel, n_blocks=n_blocks),
        grid_spec=grid_spec,
        out_shape=jax.ShapeDtypeStruct((n, D_MODEL), F32),
        compiler_params=_cparams(("arbitrary",)),
        name="combine_final_norm",
    )(b0, x2, pos_c, aff_c, g_final, *([ye] * (2 * N_EXPERTS)))


def _encoder(x, mem, p):
    batch, seq, _ = x.shape
    mem_tokens = mem.shape[1]
    n = batch * seq
    cap = CAPACITY_FACTOR * n // N_EXPERTS
    x2d = x.reshape(n, D_MODEL)

    half = DA_DIM // 2
    pos = jnp.arange(seq, dtype=F32)
    inv = ROPE_THETA ** (-jnp.arange(half, dtype=F32) / half)
    ang = pos[:, None] * inv[None, :]
    cos = jnp.cos(ang)
    sin = jnp.sin(ang)
    cos_t = jnp.tile(cos, (1, LANES // half))
    sin_t = jnp.tile(jnp.concatenate([-sin, sin], axis=1), (1, LANES // DA_DIM))

    z, hn = _inproj(x2d, p["g_mix"], p["w_main"], p["colscale"], cos_t, sin_t, seq)
    kt = _ktrans(hn, p["w_kt"])
    gates, gates_t = _gates(hn, p["w_g"], p["w_gt"], p["b_g"], p["b_gt"])

    oa = _diffattn(z, p["lamv"], p["g_da_out"], batch, seq, p["lam_init"])
    h_fw = _mlstm(z, kt, gates, gates_t, batch, seq, reverse=False)
    h_bw = _mlstm(z, kt, gates, gates_t, batch, seq, reverse=True)

    kv = _memkv(mem.reshape(batch * mem_tokens, D_MODEL), p["g_mem"], p["w_xkv"], mem_tokens)
    x2, hffn, aff_t = _mix(x2d, oa, h_fw, h_bw, z, kv, p, batch, seq, mem_tokens)

    pos3, off3 = _select(aff_t, n, cap)
    pos_t = pos3.reshape(N_EXPERTS, n)
    tile_of, blk_of, flags, n_items = _dispatch_tables(off3, n, cap)
    xe = _dispatch(hffn, pos_t.reshape(N_EXPERTS, 1, n), tile_of, blk_of, flags, cap, n_items)
    ye = _ffn(xe, p["w_e_gate"], p["w_e_up"], p["w_e_down"], cap)

    n_blocks = cap // B_SLOT
    b0 = jnp.minimum(off3[:, ::T_COMB // LANES, 0] // B_SLOT, n_blocks - 1)
    b0 = b0.T.reshape(-1).astype(I32)
    y = _combine(x2, pos_t.T, aff_t.T, ye, b0, p["g_final"], cap)
    return y.reshape(batch, seq, D_MODEL)


def kernel(x_prompt, x_sample, mem_prompt, mem_sample, g_mix, w_in, lam_q1, lam_k1, lam_q2, lam_k2,
           g_da_out, b_ml_gates, g_ml_out, w_proj_a, w_proj_b, w_out, g_xa, g_mem, w_xq, w_xkv, w_xo,
           g_ffn, w_router, w_e_gate, w_e_up, w_e_down, g_final):
    l = 0
    w = w_in[l]
    gate_lo = Z_BR_G
    gate_hi = gate_lo + N_GATES
    colscale = jnp.ones((1, Z_WIDTH), F32)
    colscale = colscale.at[:, Z_DA_Q:Z_DA_K].set(DA_DIM ** -0.5)
    colscale = colscale.at[:, Z_ML_K:Z_ML_V].set(ML_DIM ** -0.5)
    w_g = w[:, gate_lo:gate_hi]
    b = b_ml_gates[l].astype(F32)
    p = {
        "g_mix": g_mix[l][None, :],
        "w_main": jnp.concatenate([w[:, :gate_lo], w[:, gate_hi:]], axis=1).astype(BF16),
        "colscale": colscale,
        "w_kt": w[:, Z_ML_K:Z_ML_V].T.astype(BF16),
        "w_g": jnp.pad(w_g, ((0, 0), (0, LANES - N_GATES))).astype(BF16),
        "w_gt": w_g.T.astype(BF16),
        "b_g": jnp.pad(b, (0, LANES - N_GATES))[None, :],
        "b_gt": b[:, None],
        "lamv": jnp.stack([lam_q1[l], lam_k1[l], lam_q2[l], lam_k2[l]]).astype(F32),
        "lam_init": 0.8 - 0.6 * math.exp(-0.3 * l),
        "g_da_out": g_da_out[l][None, :],
        "g_ml_out": g_ml_out[l][None, :],
        "w_proj_a": w_proj_a[l].astype(BF16),
        "w_proj_b": w_proj_b[l].astype(BF16),
        "w_out": w_out[l].astype(BF16),
        "g_xa": g_xa[l][None, :],
        "g_mem": g_mem[l][None, :],
        "w_xq": w_xq[l].astype(BF16),
        "w_xkv": w_xkv[l].astype(BF16),
        "w_xo": w_xo[l].astype(BF16),
        "g_ffn": g_ffn[l][None, :],
        "w_router_t": w_router[l].T.astype(F32),
        "w_e_gate": w_e_gate[l].astype(BF16),
        "w_e_up": w_e_up[l].astype(BF16),
        "w_e_down": w_e_down[l].astype(BF16),
        "g_final": g_final[None, :],
    }
    y_prompt = _encoder(x_prompt, mem_prompt, p)
    y_sample = _encoder(x_sample, mem_sample, p)
    return (y_prompt, y_sample)
```

```python
import functools
import math

import jax
import jax.numpy as jnp
from jax import lax
from jax.experimental import pallas as pl
from jax.experimental.pallas import tpu as pltpu

F32 = jnp.float32
BF16 = jnp.bfloat16
I32 = jnp.int32

D_MODEL = 1024
DA_HEADS = 8
DA_DIM = 64
DA_HEAD_W = 2 * DA_DIM
ML_HEADS = 4
ML_DIM = 256
XA_HEADS = 4
XA_DIM = D_MODEL // XA_HEADS
N_EXPERTS = 16
CAPACITY_FACTOR = 2
D_EXPERT = 2048
EPS = 1e-6
ROPE_THETA = 10000.0
N_GATES = 4 * ML_HEADS

Z_DA_Q = 0
Z_DA_K = 1024
Z_DA_V = 2048
Z_ML_Q = 3072
Z_ML_V = 4096
Z_ML_O = 5120
Z_BR_G = 6144
Z_WIDTH = 8192
W_ML_K = 4096
W_ML_V = 5120
W_GATES = 7168

LANES = 128
VMEM_LIMIT = 56 * 1024 * 1024

TM_PROJ = 1024
TN_PROJ = 512
TQ_ATT = 256
TK_ATT = 512
ATT_UNROLL = 4
L_MLSTM = 256
TM_MIX = 512
T_DISP = 1024
B_SLOT = 256
TM_FFN = 512
F_CHUNK = 512
T_COMB = 256


def _cparams(sem):
    return pltpu.CompilerParams(dimension_semantics=sem, vmem_limit_bytes=VMEM_LIMIT)


def _rms(xf, g):
    return xf * lax.rsqrt(jnp.mean(xf * xf, axis=-1, keepdims=True) + EPS) * g


def _inproj_kernel(x_ref, g_ref, w_ref, cs_ref, cos_ref, sin_ref, z_ref, hn_ref, hs_ref, *, n_rope):
    j = pl.program_id(1)

    @pl.when(j == 0)
    def _():
        hb = _rms(x_ref[...], g_ref[...]).astype(BF16)
        hs_ref[...] = hb
        hn_ref[...] = hb

    acc = jnp.dot(hs_ref[...], w_ref[...], preferred_element_type=F32) * cs_ref[...]

    @pl.when(j >= n_rope)
    def _():
        z_ref[...] = acc.astype(BF16)

    @pl.when(j < n_rope)
    def _():
        cos = cos_ref[...]
        sin = sin_ref[...]
        lane = lax.broadcasted_iota(I32, cos.shape, 1)
        first_half = (lane % DA_DIM) < (DA_DIM // 2)
        for hh in range(acc.shape[1] // LANES):
            a = acc[:, hh * LANES:(hh + 1) * LANES]
            rot = jnp.where(first_half, pltpu.roll(a, LANES - DA_DIM // 2, 1),
                            pltpu.roll(a, DA_DIM // 2, 1))
            z_ref[:, hh * LANES:(hh + 1) * LANES] = (a * cos + rot * sin).astype(BF16)


def _inproj(x2d, g_mix, w_main, colscale, cos_t, sin_t, seq):
    n = x2d.shape[0]
    tm, tn = TM_PROJ, TN_PROJ
    n_seq_tiles = seq // tm
    n_rope = (2 * D_MODEL) // tn
    return pl.pallas_call(
        functools.partial(_inproj_kernel, n_rope=n_rope),
        grid=(n // tm, Z_WIDTH // tn),
        in_specs=[
            pl.BlockSpec((tm, D_MODEL), lambda i, j: (i, 0)),
            pl.BlockSpec((1, D_MODEL), lambda i, j: (0, 0)),
            pl.BlockSpec((D_MODEL, tn), lambda i, j: (0, j)),
            pl.BlockSpec((1, tn), lambda i, j: (0, j)),
            pl.BlockSpec((tm, LANES), lambda i, j: (i % n_seq_tiles, 0)),
            pl.BlockSpec((tm, LANES), lambda i, j: (i % n_seq_tiles, 0)),
        ],
        out_specs=[
            pl.BlockSpec((tm, tn), lambda i, j: (i, j)),
            pl.BlockSpec((tm, D_MODEL), lambda i, j: (i, 0)),
        ],
        out_shape=[jax.ShapeDtypeStruct((n, Z_WIDTH), BF16),
                   jax.ShapeDtypeStruct((n, D_MODEL), BF16)],
        scratch_shapes=[pltpu.VMEM((tm, D_MODEL), BF16)],
        compiler_params=_cparams(("parallel", "arbitrary")),
        name="inproj",
    )(x2d, g_mix, w_main, colscale, cos_t, sin_t)


def _ktrans_kernel(w_ref, h_ref, o_ref):
    kt = lax.dot_general(w_ref[...], h_ref[...], (((1,), (1,)), ((), ())),
                         preferred_element_type=F32)
    o_ref[...] = (kt * (ML_DIM ** -0.5)).astype(BF16)


def _ktrans(hn, w_kt):
    n = hn.shape[0]
    tm = TM_PROJ
    return pl.pallas_call(
        _ktrans_kernel,
        grid=(n // tm, ML_HEADS),
        in_specs=[pl.BlockSpec((ML_DIM, D_MODEL), lambda i, h: (h, 0)),
                  pl.BlockSpec((tm, D_MODEL), lambda i, h: (i, 0))],
        out_specs=pl.BlockSpec((ML_DIM, tm), lambda i, h: (h, i)),
        out_shape=jax.ShapeDtypeStruct((ML_HEADS * ML_DIM, n), BF16),
        compiler_params=_cparams(("parallel", "arbitrary")),
        name="ml_k_transposed",
    )(w_kt, hn)


def _gates_kernel(h_ref, w_ref, wt_ref, b_ref, bt_ref, g_ref, gt_ref):
    h = h_ref[...]
    g_ref[...] = jnp.dot(h, w_ref[...], preferred_element_type=F32) + b_ref[...]
    gt_ref[...] = lax.dot_general(wt_ref[...], h, (((1,), (1,)), ((), ())),
                                  preferred_element_type=F32) + bt_ref[...]


def _gates(hn, w_g, w_gt, b_g, b_gt):
    n = hn.shape[0]
    tm = TM_PROJ
    return pl.pallas_call(
        _gates_kernel,
        grid=(n // tm,),
        in_specs=[pl.BlockSpec((tm, D_MODEL), lambda i: (i, 0)),
                  pl.BlockSpec((D_MODEL, LANES), lambda i: (0, 0)),
                  pl.BlockSpec((N_GATES, D_MODEL), lambda i: (0, 0)),
                  pl.BlockSpec((1, LANES), lambda i: (0, 0)),
                  pl.BlockSpec((N_GATES, 1), lambda i: (0, 0))],
        out_specs=[pl.BlockSpec((tm, LANES), lambda i: (i, 0)),
                   pl.BlockSpec((N_GATES, tm), lambda i: (0, i))],
        out_shape=[jax.ShapeDtypeStruct((n, LANES), F32),
                   jax.ShapeDtypeStruct((N_GATES, n), F32)],
        compiler_params=_cparams(("parallel",)),
        name="ml_gates",
    )(hn, w_g, w_gt, b_g, b_gt)


def _diffattn_kernel(q_ref, k_ref, v_ref, lamv_ref, g_ref, o_ref, va_ref, *, seq, lam_init):
    tq = q_ref.shape[0]
    tk = min(TK_ATT, seq)
    n_chunks = seq // tk
    unroll = min(n_chunks, ATT_UNROLL)

    @pl.when(pl.program_id(2) == 0)
    def _():
        va_ref[:, :DA_HEAD_W] = v_ref[...]
        va_ref[:, DA_HEAD_W:] = jnp.ones((seq, DA_HEAD_W), BF16)

    q = q_ref[...]
    lane = lax.broadcasted_iota(I32, q.shape, 1)
    zero = jnp.zeros_like(q)
    qs = (jnp.where(lane < DA_DIM, q, zero), jnp.where(lane >= DA_DIM, q, zero))

    def chunk(start, carry):
        kc = k_ref[pl.ds(start, tk), :]
        vc = va_ref[pl.ds(start, tk), :]
        out = []
        for qm, (m, l, a) in zip(qs, carry):
            s = lax.dot_general(qm, kc, (((1,), (1,)), ((), ())), preferred_element_type=F32)
            mn = jnp.maximum(m, jnp.max(s, axis=-1, keepdims=True))
            alpha = jnp.exp2(m - mn)
            p = jnp.exp2((s - mn).astype(BF16))
            pv = jnp.dot(p, vc, preferred_element_type=F32)
            out.append((mn, alpha * l + pv[:, DA_HEAD_W:DA_HEAD_W + 1],
                        alpha * a + pv[:, :DA_HEAD_W]))
        return tuple(out)

    def body(c, carry):
        for u in range(unroll):
            carry = chunk(pl.multiple_of((c * unroll + u) * tk, tk), carry)
        return carry

    init = (jnp.full((tq, 1), -jnp.inf, F32), jnp.zeros((tq, 1), F32), jnp.zeros((tq, DA_HEAD_W), F32))
    (m0, l0, a0), (m1, l1, a1) = lax.fori_loop(0, n_chunks // unroll, body, (init, init))

    lv = lamv_ref[...]
    lam = (jnp.exp(jnp.sum(lv[0:1, :] * lv[1:2, :], axis=-1, keepdims=True))
           - jnp.exp(jnp.sum(lv[2:3, :] * lv[3:4, :], axis=-1, keepdims=True)) + lam_init)
    o = a0 / l0 - lam * (a1 / l1)
    o_ref[...] = (_rms(o, g_ref[...]) * (1.0 - lam_init)).astype(BF16)


def _diffattn(z, lamv, g_da, batch, seq, lam_init):
    n = z.shape[0]
    tq = TQ_ATT
    nq = seq // tq
    hq = Z_DA_Q // DA_HEAD_W
    hk = Z_DA_K // DA_HEAD_W
    hv = Z_DA_V // DA_HEAD_W
    return pl.pallas_call(
        functools.partial(_diffattn_kernel, seq=seq, lam_init=lam_init),
        grid=(batch, DA_HEADS, nq),
        in_specs=[
            pl.BlockSpec((tq, DA_HEAD_W), lambda b, h, i: (b * nq + i, hq + h)),
            pl.BlockSpec((seq, DA_HEAD_W), lambda b, h, i: (b, hk + h)),
            pl.BlockSpec((seq, DA_HEAD_W), lambda b, h, i: (b, hv + h)),
            pl.BlockSpec((4, DA_DIM), lambda b, h, i: (0, 0)),
            pl.BlockSpec((1, DA_HEAD_W), lambda b, h, i: (0, 0)),
        ],
        out_specs=pl.BlockSpec((tq, DA_HEAD_W), lambda b, h, i: (b * nq + i, h)),
        out_shape=jax.ShapeDtypeStruct((n, DA_HEADS * DA_HEAD_W), BF16),
        scratch_shapes=[pltpu.VMEM((seq, 2 * DA_HEAD_W), BF16)],
        compiler_params=_cparams(("parallel", "parallel", "arbitrary")),
        name="diff_attention",
    )(z, z, z, lamv, g_da)


def _log_sigmoid(x):
    return jnp.minimum(x, 0.0) - jnp.log1p(jnp.exp(-jnp.abs(x)))


def _split3(x):
    hi = x.astype(BF16)
    r = x - hi.astype(F32)
    mid = r.astype(BF16)
    lo = (r - mid.astype(F32)).astype(BF16)
    return hi, mid, lo


def _mlstm_chunk(q, kt, v, g, gt, cta_ref, m_ref, h_ref, head, reverse):
    L = q.shape[0]
    col_i = head + (ML_HEADS if reverse else 0)
    col_f = col_i + 2 * ML_HEADS
    lane = lax.broadcasted_iota(I32, g.shape, 1)
    sub = lax.broadcasted_iota(I32, gt.shape, 0)
    i_col = jnp.sum(jnp.where(lane == col_i, g, 0.0), axis=1, keepdims=True)
    f_col = jnp.sum(jnp.where(lane == col_f, g, 0.0), axis=1, keepdims=True)
    i_row = jnp.sum(jnp.where(sub == col_i, gt, 0.0), axis=0, keepdims=True)
    f_row = jnp.sum(jnp.where(sub == col_f, gt, 0.0), axis=0, keepdims=True)

    ri = lax.broadcasted_iota(I32, (L, L), 0)
    ci = lax.broadcasted_iota(I32, (L, L), 1)
    tri = (ci >= ri) if reverse else (ci <= ri)
    tri_b = jnp.where(tri, 1.0, 0.0).astype(BF16)
    g_col = jnp.zeros((L, LANES), F32)
    for piece in _split3(_log_sigmoid(f_col)):
        g_col = g_col + jnp.dot(tri_b, jnp.broadcast_to(piece, (L, LANES)), preferred_element_type=F32)
    g_col = g_col[:, 0:1]
    g_row = jnp.zeros((16, L), F32)
    for piece in _split3(_log_sigmoid(f_row)):
        g_row = g_row + lax.dot_general(jnp.broadcast_to(piece, (16, L)), tri_b, (((1,), (1,)), ((), ())),
                                        preferred_element_type=F32)
    g_row = g_row[0:1, :]
    last = 0 if reverse else L - 1
    lane_l = lax.broadcasted_iota(I32, (1, L), 1)
    g_last = jnp.sum(jnp.where(lane_l == last, g_row, 0.0), axis=1, keepdims=True)

    m_prev = m_ref[...]
    dmat = jnp.where(tri, g_col + (i_row - g_row), -jnp.inf)
    a_col = g_col + m_prev
    mj = jnp.maximum(a_col, jnp.max(dmat, axis=1, keepdims=True))
    w_inter = jnp.exp(a_col - mj)
    s = jnp.dot(q, kt, preferred_element_type=F32) * jnp.exp(dmat - mj)
    ones = jnp.ones((L, LANES), BF16)
    cta = cta_ref[...]
    num = (w_inter * jnp.dot(q, cta.astype(BF16), preferred_element_type=F32)
           + jnp.dot(s.astype(BF16), jnp.concatenate([v, ones], axis=1), preferred_element_type=F32))
    den = num[:, ML_DIM:ML_DIM + 1]
    inv = 1.0 / jnp.maximum(jnp.abs(den), jnp.exp(-mj))
    h_ref[...] = (num[:, :ML_DIM] * inv).astype(BF16)

    dec_row = g_last - g_row + i_row
    dec_col = g_last - g_col + i_col
    m_new = jnp.maximum(g_last + m_prev, jnp.max(dec_row, axis=1, keepdims=True))
    wk_col = jnp.exp(dec_col - m_new)
    carry_scale = jnp.exp(g_last + m_prev - m_new)
    wv = jnp.concatenate([(wk_col * v.astype(F32)).astype(BF16),
                          jnp.broadcast_to(wk_col, (L, LANES)).astype(BF16)], axis=1)
    cta_ref[...] = carry_scale * cta + jnp.dot(kt, wv, preferred_element_type=F32)
    m_ref[...] = m_new


def _mlstm_kernel(qf_ref, ktf_ref, vf_ref, gf_ref, gtf_ref, qb_ref, ktb_ref, vb_ref, gb_ref, gtb_ref,
                  hf_ref, hb_ref, ctf_ref, mf_ref, ctb_ref, mb_ref):
    head = pl.program_id(1)

    @pl.when(pl.program_id(2) == 0)
    def _():
        ctf_ref[...] = jnp.zeros_like(ctf_ref)
        mf_ref[...] = jnp.zeros_like(mf_ref)
        ctb_ref[...] = jnp.zeros_like(ctb_ref)
        mb_ref[...] = jnp.zeros_like(mb_ref)

    _mlstm_chunk(qf_ref[...], ktf_ref[...], vf_ref[...], gf_ref[...], gtf_ref[...],
                 ctf_ref, mf_ref, hf_ref, head, False)
    _mlstm_chunk(qb_ref[...], ktb_ref[...], vb_ref[...], gb_ref[...], gtb_ref[...],
                 ctb_ref, mb_ref, hb_ref, head, True)


def _mlstm(z, kt, gates, gates_t, batch, seq):
    n = z.shape[0]
    L = L_MLSTM
    nc = seq // L
    cq = Z_ML_Q // ML_DIM
    cv = Z_ML_V // ML_DIM

    def specs(tix):
        return [
            pl.BlockSpec((L, ML_DIM), lambda b, h, t: (b * nc + tix(t), cq + h)),
            pl.BlockSpec((ML_DIM, L), lambda b, h, t: (h, b * nc + tix(t))),
            pl.BlockSpec((L, ML_DIM), lambda b, h, t: (b * nc + tix(t), cv + h)),
            pl.BlockSpec((L, LANES), lambda b, h, t: (b * nc + tix(t), 0)),
            pl.BlockSpec((N_GATES, L), lambda b, h, t: (0, b * nc + tix(t))),
        ]

    fwd = lambda t: t
    bwd = lambda t: nc - 1 - t
    out = jax.ShapeDtypeStruct((n, ML_HEADS * ML_DIM), BF16)
    return pl.pallas_call(
        _mlstm_kernel,
        grid=(batch, ML_HEADS, nc),
        in_specs=specs(fwd) + specs(bwd),
        out_specs=[pl.BlockSpec((L, ML_DIM), lambda b, h, t: (b * nc + fwd(t), h)),
                   pl.BlockSpec((L, ML_DIM), lambda b, h, t: (b * nc + bwd(t), h))],
        out_shape=[out, out],
        scratch_shapes=[pltpu.VMEM((ML_DIM, ML_DIM + LANES), F32), pltpu.VMEM((1, 1), F32),
                        pltpu.VMEM((ML_DIM, ML_DIM + LANES), F32), pltpu.VMEM((1, 1), F32)],
        compiler_params=_cparams(("parallel", "parallel", "arbitrary")),
        name="mlstm",
    )(z, kt, z, gates, gates_t, z, kt, z, gates, gates_t)


def _memkv_kernel(m_ref, g_ref, w_ref, o_ref):
    mn = _rms(m_ref[...], g_ref[...]).astype(BF16)
    o_ref[...] = jnp.dot(mn, w_ref[...], preferred_element_type=F32).astype(BF16)


def _memkv(mem2d, g_mem, w_xkv, mem_tokens):
    n = mem2d.shape[0]
    return pl.pallas_call(
        _memkv_kernel,
        grid=(n // mem_tokens,),
        in_specs=[pl.BlockSpec((mem_tokens, D_MODEL), lambda b: (b, 0)),
                  pl.BlockSpec((1, D_MODEL), lambda b: (0, 0)),
                  pl.BlockSpec((D_MODEL, 2 * D_MODEL), lambda b: (0, 0))],
        out_specs=pl.BlockSpec((mem_tokens, 2 * D_MODEL), lambda b: (b, 0)),
        out_shape=jax.ShapeDtypeStruct((n, 2 * D_MODEL), BF16),
        compiler_params=_cparams(("parallel",)),
        name="mem_kv",
    )(mem2d, g_mem, w_xkv)


def _sigmoid(x):
    return 1.0 / (1.0 + jnp.exp(-x))


def _mix_kernel(x_ref, oa_ref, hf_ref, hb_ref, mo_ref, bga_ref, bgb_ref, kv_ref,
                gml_ref, gxa_ref, gffn_ref, wpa_ref, wpb_ref, wo_ref, wxq_ref, wxo_ref, wrt_ref,
                x2_ref, hn_ref, afft_ref):
    hsum = hf_ref[...].astype(F32) + hb_ref[...].astype(F32)
    gml = gml_ref[...]
    parts = []
    for hh in range(ML_HEADS):
        sl = slice(hh * ML_DIM, (hh + 1) * ML_DIM)
        parts.append(_rms(hsum[:, sl], gml[:, sl]))
    hb = jnp.concatenate(parts, axis=1)
    hb = hb * _sigmoid(mo_ref[...].astype(F32))
    y_a = jnp.dot(oa_ref[...], wpa_ref[...], preferred_element_type=F32)
    y_b = jnp.dot(hb.astype(BF16), wpb_ref[...], preferred_element_type=F32)
    merged = (_sigmoid(bga_ref[...].astype(F32)) * y_a
              + _sigmoid(bgb_ref[...].astype(F32)) * y_b)
    x1 = x_ref[...] + jnp.dot(merged.astype(BF16), wo_ref[...], preferred_element_type=F32)

    hq = _rms(x1, gxa_ref[...]).astype(BF16)
    qx = (jnp.dot(hq, wxq_ref[...], preferred_element_type=F32) * (XA_DIM ** -0.5)).astype(BF16)
    kv = kv_ref[...]
    outs = []
    for hh in range(XA_HEADS):
        sl = slice(hh * XA_DIM, (hh + 1) * XA_DIM)
        kh = kv[:, sl]
        vh = kv[:, D_MODEL + hh * XA_DIM:D_MODEL + (hh + 1) * XA_DIM]
        s = lax.dot_general(qx[:, sl], kh, (((1,), (1,)), ((), ())), preferred_element_type=F32)
        p = jnp.exp(s - jnp.max(s, axis=-1, keepdims=True))
        p = p / jnp.sum(p, axis=-1, keepdims=True)
        outs.append(jnp.dot(p.astype(BF16), vh, preferred_element_type=F32).astype(BF16))
    o = jnp.concatenate(outs, axis=1)
    x2 = x1 + jnp.dot(o, wxo_ref[...], preferred_element_type=F32)
    x2_ref[...] = x2

    hn = _rms(x2, gffn_ref[...])
    hn_ref[...] = hn.astype(BF16)
    logits_t = lax.dot_general(wrt_ref[...], hn, (((1,), (1,)), ((), ())),
                               precision=lax.Precision.HIGHEST, preferred_element_type=F32)
    e = jnp.exp(logits_t - jnp.max(logits_t, axis=0, keepdims=True))
    afft_ref[...] = e / jnp.sum(e, axis=0, keepdims=True)


def _mix(x2d, oa, h_fw, h_bw, z, kv, p, batch, seq, mem_tokens):
    n = x2d.shape[0]
    tm = TM_MIX
    nt = seq // tm
    cmo = Z_ML_O // D_MODEL
    cbg = Z_BR_G // D_MODEL
    full = lambda r, c: pl.BlockSpec((r, c), lambda i: (0, 0), pipeline_mode=pl.Buffered(1))
    return pl.pallas_call(
        _mix_kernel,
        grid=(n // tm,),
        in_specs=[
            pl.BlockSpec((tm, D_MODEL), lambda i: (i, 0)),
            pl.BlockSpec((tm, D_MODEL), lambda i: (i, 0)),
            pl.BlockSpec((tm, D_MODEL), lambda i: (i, 0)),
            pl.BlockSpec((tm, D_MODEL), lambda i: (i, 0)),
            pl.BlockSpec((tm, D_MODEL), lambda i: (i, cmo)),
            pl.BlockSpec((tm, D_MODEL), lambda i: (i, cbg)),
            pl.BlockSpec((tm, D_MODEL), lambda i: (i, cbg + 1)),
            pl.BlockSpec((mem_tokens, 2 * D_MODEL), lambda i: (i // nt, 0)),
            full(1, D_MODEL), full(1, D_MODEL), full(1, D_MODEL),
            full(D_MODEL, D_MODEL), full(D_MODEL, D_MODEL), full(D_MODEL, D_MODEL),
            full(D_MODEL, D_MODEL), full(D_MODEL, D_MODEL),
            full(N_EXPERTS, D_MODEL),
        ],
        out_specs=[pl.BlockSpec((tm, D_MODEL), lambda i: (i, 0)),
                   pl.BlockSpec((tm, D_MODEL), lambda i: (i, 0)),
                   pl.BlockSpec((N_EXPERTS, tm), lambda i: (0, i))],
        out_shape=[jax.ShapeDtypeStruct((n, D_MODEL), F32),
                   jax.ShapeDtypeStruct((n, D_MODEL), BF16),
                   jax.ShapeDtypeStruct((N_EXPERTS, n), F32)],
        compiler_params=_cparams(("parallel",)),
        name="mix_xattn_router",
    )(x2d, oa, h_fw, h_bw, z, z, z, kv,
      p["g_ml_out"], p["g_xa"], p["g_ffn"], p["w_proj_a"], p["w_proj_b"], p["w_out"],
      p["w_xq"], p["w_xo"], p["w_router_t"])


def _prefix_incl(mask_f, upper, ones, lower):
    mb = mask_f.astype(BF16)
    lane_incl = jnp.dot(mb, upper, preferred_element_type=F32)
    row_tot = jnp.dot(mb, ones, preferred_element_type=F32)
    row_off = jnp.dot(lower, row_tot.astype(BF16), preferred_element_type=F32)
    return lane_incl + row_off, row_off


def _select_kernel(a_ref, pos_ref, off_ref, *, cap):
    a = a_ref[...]
    rows = a.shape[0]
    bits = pltpu.bitcast(a, I32)

    def body(i, thr):
        cand = thr | jnp.left_shift(jnp.int32(1), 30 - i)
        cnt = jnp.sum(jnp.where(bits >= cand, 1.0, 0.0), keepdims=True)
        return jnp.where(cnt >= cap, cand, thr)

    thr = lax.fori_loop(0, 31, body, jnp.zeros((1, 1), I32))
    gt = bits > thr
    eq = bits == thr
    n_gt = jnp.sum(jnp.where(gt, 1.0, 0.0), keepdims=True)
    need = cap - n_gt

    li = lax.broadcasted_iota(I32, (LANES, LANES), 0)
    lj = lax.broadcasted_iota(I32, (LANES, LANES), 1)
    upper = (li <= lj).astype(BF16)
    ones = jnp.ones((LANES, LANES), BF16)
    ri = lax.broadcasted_iota(I32, (rows, rows), 0)
    rj = lax.broadcasted_iota(I32, (rows, rows), 1)
    lower = (rj < ri).astype(BF16)

    eq_rank, _ = _prefix_incl(jnp.where(eq, 1.0, 0.0), upper, ones, lower)
    sel = gt | (eq & (eq_rank <= need))
    incl, row_off = _prefix_incl(jnp.where(sel, 1.0, 0.0), upper, ones, lower)
    pos_ref[...] = jnp.where(sel, incl - 1.0, -1.0).astype(I32)
    off_ref[...] = row_off.astype(I32)


def _select(aff_t, n_tokens, cap):
    rows = n_tokens // LANES
    a3 = aff_t.reshape(N_EXPERTS, rows, LANES)
    spec = pl.BlockSpec((None, rows, LANES), lambda e: (e, 0, 0))
    return pl.pallas_call(
        functools.partial(_select_kernel, cap=float(cap)),
        grid=(N_EXPERTS,),
        in_specs=[spec],
        out_specs=[spec, spec],
        out_shape=[jax.ShapeDtypeStruct((N_EXPERTS, rows, LANES), I32),
                   jax.ShapeDtypeStruct((N_EXPERTS, rows, LANES), I32)],
        compiler_params=_cparams(("parallel",)),
        name="expert_choice_select",
    )(a3)


def _dispatch_kernel(tile_ref, blk_ref, flag_ref, x_ref, pos_ref, o_ref, *, n_items):
    e = pl.program_id(0)
    w = pl.program_id(1)
    idx = e * n_items + w
    flags = flag_ref[idx]
    blk = blk_ref[idx]

    @pl.when((flags & 2) != 0)
    def _():
        o_ref[...] = jnp.zeros_like(o_ref)

    @pl.when((flags & 1) != 0)
    def _():
        rel = pos_ref[...] - blk * B_SLOT
        slot = lax.broadcasted_iota(I32, (B_SLOT, rel.shape[1]), 0)
        onehot = jnp.where(slot == rel, 1.0, 0.0).astype(BF16)
        rows = jnp.dot(onehot, x_ref[...], preferred_element_type=F32)
        o_ref[...] = (o_ref[...].astype(F32) + rows).astype(BF16)


def _dispatch(hn, pos_t3, tile_of, blk_of, flags, cap, n_items):
    n = hn.shape[0]
    t = min(T_DISP, n)
    grid_spec = pltpu.PrefetchScalarGridSpec(
        num_scalar_prefetch=3,
        grid=(N_EXPERTS, n_items),
        in_specs=[
            pl.BlockSpec((t, D_MODEL), lambda e, w, tl, bl, fl: (tl[e * n_items + w], 0)),
            pl.BlockSpec((None, 1, t), lambda e, w, tl, bl, fl: (e, 0, tl[e * n_items + w])),
        ],
        out_specs=pl.BlockSpec((None, B_SLOT, D_MODEL),
                               lambda e, w, tl, bl, fl: (e, bl[e * n_items + w], 0)),
    )
    return pl.pallas_call(
        functools.partial(_dispatch_kernel, n_items=n_items),
        grid_spec=grid_spec,
        out_shape=jax.ShapeDtypeStruct((N_EXPERTS, cap, D_MODEL), BF16),
        compiler_params=_cparams(("arbitrary", "arbitrary")),
        name="dispatch",
    )(tile_of, blk_of, flags, hn, pos_t3)


def _dispatch_tables(row_off, n_tokens, cap):
    t = min(T_DISP, n_tokens)
    n_tiles = n_tokens // t
    n_blocks = cap // B_SLOT
    n_items = n_tiles + n_blocks
    start = row_off[:, ::t // LANES, 0]
    end = jnp.concatenate([start[:, 1:], jnp.full((N_EXPERTS, 1), cap, I32)], axis=1)
    nonempty = end > start
    b_lo = jnp.minimum(start // B_SLOT, n_blocks - 1)
    b_hi = jnp.maximum(end - 1, 0) // B_SLOT
    npairs = jnp.where(nonempty, b_hi - b_lo + 1, 0)
    cum = jnp.cumsum(npairs, axis=1)
    base = cum - npairs
    total = cum[:, -1:]
    w = jnp.arange(n_items, dtype=I32)[None, :]
    tile = jnp.sum((cum[:, None, :] <= w[:, :, None]).astype(I32), axis=-1)
    tile = jnp.minimum(tile, n_tiles - 1)
    blk = jnp.take_along_axis(b_lo, tile, axis=1) + (w - jnp.take_along_axis(base, tile, axis=1))
    valid = w < total
    last_tile = jnp.take_along_axis(tile, jnp.maximum(total - 1, 0), axis=1)
    tile = jnp.where(valid, tile, last_tile)
    blk = jnp.where(valid, blk, n_blocks - 1)
    prev = jnp.concatenate([jnp.full((N_EXPERTS, 1), -1, I32), blk[:, :-1]], axis=1)
    flags = valid.astype(I32) + 2 * (blk != prev).astype(I32)
    return (tile.reshape(-1).astype(I32), blk.reshape(-1).astype(I32),
            flags.reshape(-1).astype(I32), n_items)


def _ffn_kernel(x_ref, wg_ref, wu_ref, wd_ref, o_ref):
    x = x_ref[...]
    acc = jnp.zeros((x.shape[0], D_MODEL), F32)
    for c in range(D_EXPERT // F_CHUNK):
        sl = slice(c * F_CHUNK, (c + 1) * F_CHUNK)
        gte = jnp.dot(x, wg_ref[:, sl], preferred_element_type=F32)
        up = jnp.dot(x, wu_ref[:, sl], preferred_element_type=F32)
        hid = gte * (1.0 / (1.0 + jnp.exp(-gte))) * up
        acc = acc + jnp.dot(hid.astype(BF16), wd_ref[sl, :], preferred_element_type=F32)
    o_ref[...] = acc.astype(BF16)


def _ffn(xe, w_gate, w_up, w_down, cap):
    tm = min(TM_FFN, cap)
    return pl.pallas_call(
        _ffn_kernel,
        grid=(N_EXPERTS, cap // tm),
        in_specs=[pl.BlockSpec((None, tm, D_MODEL), lambda e, i: (e, i, 0)),
                  pl.BlockSpec((None, D_MODEL, D_EXPERT), lambda e, i: (e, 0, 0)),
                  pl.BlockSpec((None, D_MODEL, D_EXPERT), lambda e, i: (e, 0, 0)),
                  pl.BlockSpec((None, D_EXPERT, D_MODEL), lambda e, i: (e, 0, 0))],
        out_specs=pl.BlockSpec((None, tm, D_MODEL), lambda e, i: (e, i, 0)),
        out_shape=jax.ShapeDtypeStruct((N_EXPERTS, cap, D_MODEL), BF16),
        compiler_params=_cparams(("parallel", "arbitrary")),
        name="expert_ffn",
    )(xe, w_gate, w_up, w_down)


def _combine_kernel(b0_ref, st_ref, x_ref, pos_ref, aff_ref, g_ref, *rest):
    ye_refs = rest[:2 * N_EXPERTS]
    o_ref = rest[2 * N_EXPERTS]
    t = pl.program_id(0)
    tt = pos_ref.shape[0]
    o_ref[...] = x_ref[...]
    slot = lax.broadcasted_iota(I32, (tt, B_SLOT), 1)
    for e in range(N_EXPERTS):
        b0 = b0_ref[t * N_EXPERTS + e]
        pe = pos_ref[:, e:e + 1]
        ae = aff_ref[:, e:e + 1]

        def add_block(kk, b0=b0, pe=pe, ae=ae, e=e):
            onehot = jnp.where(slot == pe - (b0 + kk) * B_SLOT, 1.0, 0.0).astype(BF16)
            contrib = jnp.dot(onehot, ye_refs[2 * e + kk][...], preferred_element_type=F32)
            o_ref[...] += contrib * ae

        add_block(0)
        pl.when(st_ref[t * N_EXPERTS + e] != 0)(functools.partial(add_block, 1))
    o_ref[...] = _rms(o_ref[...], g_ref[...])


def _combine(x2, pos_c, aff_c, ye, b0, straddle, g_final, cap):
    n = x2.shape[0]
    tt = T_COMB
    n_blocks = cap // B_SLOT

    def ye_spec(e, kk):
        return pl.BlockSpec(
            (None, B_SLOT, D_MODEL),
            lambda t, b0r, st: (e, jnp.minimum(b0r[t * N_EXPERTS + e] + kk, n_blocks - 1), 0))

    ye_specs = [ye_spec(e, kk) for e in range(N_EXPERTS) for kk in range(2)]
    grid_spec = pltpu.PrefetchScalarGridSpec(
        num_scalar_prefetch=2,
        grid=(n // tt,),
        in_specs=[pl.BlockSpec((tt, D_MODEL), lambda t, b0r, st: (t, 0)),
                  pl.BlockSpec((tt, N_EXPERTS), lambda t, b0r, st: (t, 0)),
                  pl.BlockSpec((tt, N_EXPERTS), lambda t, b0r, st: (t, 0)),
                  pl.BlockSpec((1, D_MODEL), lambda t, b0r, st: (0, 0))] + ye_specs,
        out_specs=pl.BlockSpec((tt, D_MODEL), lambda t, b0r, st: (t, 0)),
    )
    return pl.pallas_call(
        _combine_kernel,
        grid_spec=grid_spec,
        out_shape=jax.ShapeDtypeStruct((n, D_MODEL), F32),
        compiler_params=_cparams(("arbitrary",)),
        name="combine_final_norm",
    )(b0, straddle, x2, pos_c, aff_c, g_final, *([ye] * (2 * N_EXPERTS)))


def _encoder(x, mem, p):
    batch, seq, _ = x.shape
    mem_tokens = mem.shape[1]
    n = batch * seq
    cap = CAPACITY_FACTOR * n // N_EXPERTS
    x2d = x.reshape(n, D_MODEL)

    half = DA_DIM // 2
    pos = jnp.arange(seq, dtype=F32)
    inv = ROPE_THETA ** (-jnp.arange(half, dtype=F32) / half)
    ang = pos[:, None] * inv[None, :]
    cos = jnp.cos(ang)
    sin = jnp.sin(ang)
    cos_t = jnp.tile(cos, (1, LANES // half))
    sin_t = jnp.tile(jnp.concatenate([-sin, sin], axis=1), (1, LANES // DA_DIM))

    z, hn = _inproj(x2d, p["g_mix"], p["w_main"], p["colscale"], cos_t, sin_t, seq)
    kt = _ktrans(hn, p["w_kt"])
    gates, gates_t = _gates(hn, p["w_g"], p["w_gt"], p["b_g"], p["b_gt"])

    oa = _diffattn(z, p["lamv"], p["g_da_out"], batch, seq, p["lam_init"])
    h_fw, h_bw = _mlstm(z, kt, gates, gates_t, batch, seq)

    kv = _memkv(mem.reshape(batch * mem_tokens, D_MODEL), p["g_mem"], p["w_xkv"], mem_tokens)
    x2, hffn, aff_t = _mix(x2d, oa, h_fw, h_bw, z, kv, p, batch, seq, mem_tokens)

    pos3, off3 = _select(aff_t, n, cap)
    pos_t = pos3.reshape(N_EXPERTS, n)
    tile_of, blk_of, flags, n_items = _dispatch_tables(off3, n, cap)
    xe = _dispatch(hffn, pos_t.reshape(N_EXPERTS, 1, n), tile_of, blk_of, flags, cap, n_items)
    ye = _ffn(xe, p["w_e_gate"], p["w_e_up"], p["w_e_down"], cap)

    n_blocks = cap // B_SLOT
    start = off3[:, ::T_COMB // LANES, 0]
    end = jnp.concatenate([start[:, 1:], jnp.full((N_EXPERTS, 1), cap, I32)], axis=1)
    b0 = jnp.minimum(start // B_SLOT, n_blocks - 1)
    straddle = (end > start) & ((end - 1) // B_SLOT > b0)
    y = _combine(x2, pos_t.T, aff_t.T, ye, b0.T.reshape(-1).astype(I32),
                 straddle.T.reshape(-1).astype(I32), p["g_final"], cap)
    return y.reshape(batch, seq, D_MODEL)


def kernel(x_prompt, x_sample, mem_prompt, mem_sample, g_mix, w_in, lam_q1, lam_k1, lam_q2, lam_k2,
           g_da_out, b_ml_gates, g_ml_out, w_proj_a, w_proj_b, w_out, g_xa, g_mem, w_xq, w_xkv, w_xo,
           g_ffn, w_router, w_e_gate, w_e_up, w_e_down, g_final):
    l = 0
    w = w_in[l]
    assert w_in.shape[0] == 1, "single-layer encoder"
    gate_lo = W_GATES
    gate_hi = gate_lo + N_GATES
    colscale = jnp.ones((1, Z_WIDTH), F32)
    colscale = colscale.at[:, Z_DA_Q:Z_DA_K].set(DA_DIM ** -0.5 * math.log2(math.e))
    w_g = w[:, gate_lo:gate_hi]
    b = b_ml_gates[l].astype(F32)
    p = {
        "g_mix": g_mix[l][None, :],
        "w_main": jnp.concatenate([w[:, :W_ML_K], w[:, W_ML_V:gate_lo], w[:, gate_hi:]],
                                  axis=1).astype(BF16),
        "colscale": colscale,
        "w_kt": w[:, W_ML_K:W_ML_V].T.astype(BF16),
        "w_g": jnp.pad(w_g, ((0, 0), (0, LANES - N_GATES))).astype(BF16),
        "w_gt": w_g.T.astype(BF16),
        "b_g": jnp.pad(b, (0, LANES - N_GATES))[None, :],
        "b_gt": b[:, None],
        "lamv": jnp.stack([lam_q1[l], lam_k1[l], lam_q2[l], lam_k2[l]]).astype(F32),
        "lam_init": 0.8 - 0.6 * math.exp(-0.3 * l),
        "g_da_out": g_da_out[l][None, :],
        "g_ml_out": g_ml_out[l][None, :],
        "w_proj_a": w_proj_a[l].astype(BF16),
        "w_proj_b": w_proj_b[l].astype(BF16),
        "w_out": w_out[l].astype(BF16),
        "g_xa": g_xa[l][None, :],
        "g_mem": g_mem[l][None, :],
        "w_xq": w_xq[l].astype(BF16),
        "w_xkv": w_xkv[l].astype(BF16),
        "w_xo": w_xo[l].astype(BF16),
        "g_ffn": g_ffn[l][None, :],
        "w_router_t": w_router[l].T.astype(F32),
        "w_e_gate": w_e_gate[l].astype(BF16),
        "w_e_up": w_e_up[l].astype(BF16),
        "w_e_down": w_e_down[l].astype(BF16),
        "g_final": g_final[None, :],
    }
    y_prompt = _encoder(x_prompt, mem_prompt, p)
    y_sample = _encoder(x_sample, mem_sample, p)
    return (y_prompt, y_sample)
```

```python
import functools
import math

import jax
import jax.numpy as jnp
from jax import lax
from jax.experimental import pallas as pl
from jax.experimental.pallas import tpu as pltpu

F32 = jnp.float32
BF16 = jnp.bfloat16
I32 = jnp.int32

D_MODEL = 1024
DA_HEADS = 8
DA_DIM = 64
DA_HEAD_W = 2 * DA_DIM
ML_HEADS = 4
ML_DIM = 256
XA_HEADS = 4
XA_DIM = D_MODEL // XA_HEADS
N_EXPERTS = 16
CAPACITY_FACTOR = 2
D_EXPERT = 2048
EPS = 1e-6
ROPE_THETA = 10000.0
N_GATES = 4 * ML_HEADS

Z_DA_Q = 0
Z_DA_K = 1024
Z_DA_V = 2048
Z_ML_Q = 3072
Z_ML_V = 4096
Z_ML_O = 5120
Z_BR_G = 6144
Z_WIDTH = 8192
W_ML_K = 4096
W_ML_V = 5120
W_GATES = 7168

LANES = 128
VMEM_LIMIT = 56 * 1024 * 1024

TM_PROJ = 1024
TN_PROJ = 512
TQ_ATT = 1024
TK_ATT = 512
ATT_UNROLL = 4
L_MLSTM = 256
TM_MIX = 512
T_DISP = 512
B_SLOT = 256
TM_FFN = 512
F_CHUNK = 512
T_COMB = 256


def _cparams(sem):
    return pltpu.CompilerParams(dimension_semantics=sem, vmem_limit_bytes=VMEM_LIMIT)


def _rms(xf, g):
    return xf * lax.rsqrt(jnp.mean(xf * xf, axis=-1, keepdims=True) + EPS) * g


def _inproj_kernel(x_ref, g_ref, w_ref, cs_ref, cos_ref, sin_ref, z_ref, hn_ref, hs_ref, *, n_rope):
    j = pl.program_id(1)

    @pl.when(j == 0)
    def _():
        hb = _rms(x_ref[...], g_ref[...]).astype(BF16)
        hs_ref[...] = hb
        hn_ref[...] = hb

    acc = jnp.dot(hs_ref[...], w_ref[...], preferred_element_type=F32) * cs_ref[...]

    @pl.when(j >= n_rope)
    def _():
        z_ref[...] = acc.astype(BF16)

    @pl.when(j < n_rope)
    def _():
        cos = cos_ref[...]
        sin = sin_ref[...]
        lane = lax.broadcasted_iota(I32, cos.shape, 1)
        first_half = (lane % DA_DIM) < (DA_DIM // 2)
        for hh in range(acc.shape[1] // LANES):
            a = acc[:, hh * LANES:(hh + 1) * LANES]
            rot = jnp.where(first_half, pltpu.roll(a, LANES - DA_DIM // 2, 1),
                            pltpu.roll(a, DA_DIM // 2, 1))
            z_ref[:, hh * LANES:(hh + 1) * LANES] = (a * cos + rot * sin).astype(BF16)


def _inproj(x2d, g_mix, w_main, colscale, cos_t, sin_t, seq):
    n = x2d.shape[0]
    tm, tn = TM_PROJ, TN_PROJ
    n_seq_tiles = seq // tm
    n_rope = (2 * D_MODEL) // tn
    return pl.pallas_call(
        functools.partial(_inproj_kernel, n_rope=n_rope),
        grid=(n // tm, Z_WIDTH // tn),
        in_specs=[
            pl.BlockSpec((tm, D_MODEL), lambda i, j: (i, 0)),
            pl.BlockSpec((1, D_MODEL), lambda i, j: (0, 0)),
            pl.BlockSpec((D_MODEL, tn), lambda i, j: (0, j)),
            pl.BlockSpec((1, tn), lambda i, j: (0, j)),
            pl.BlockSpec((tm, LANES), lambda i, j: (i % n_seq_tiles, 0)),
            pl.BlockSpec((tm, LANES), lambda i, j: (i % n_seq_tiles, 0)),
        ],
        out_specs=[
            pl.BlockSpec((tm, tn), lambda i, j: (i, j)),
            pl.BlockSpec((tm, D_MODEL), lambda i, j: (i, 0)),
        ],
        out_shape=[jax.ShapeDtypeStruct((n, Z_WIDTH), BF16),
                   jax.ShapeDtypeStruct((n, D_MODEL), BF16)],
        scratch_shapes=[pltpu.VMEM((tm, D_MODEL), BF16)],
        compiler_params=_cparams(("parallel", "arbitrary")),
        name="inproj",
    )(x2d, g_mix, w_main, colscale, cos_t, sin_t)


def _ktrans_kernel(w_ref, h_ref, o_ref):
    kt = lax.dot_general(w_ref[...], h_ref[...], (((1,), (1,)), ((), ())),
                         preferred_element_type=F32)
    o_ref[...] = (kt * (ML_DIM ** -0.5)).astype(BF16)


def _ktrans(hn, w_kt):
    n = hn.shape[0]
    tm = TM_PROJ
    return pl.pallas_call(
        _ktrans_kernel,
        grid=(n // tm, ML_HEADS),
        in_specs=[pl.BlockSpec((ML_DIM, D_MODEL), lambda i, h: (h, 0)),
                  pl.BlockSpec((tm, D_MODEL), lambda i, h: (i, 0))],
        out_specs=pl.BlockSpec((ML_DIM, tm), lambda i, h: (h, i)),
        out_shape=jax.ShapeDtypeStruct((ML_HEADS * ML_DIM, n), BF16),
        compiler_params=_cparams(("parallel", "arbitrary")),
        name="ml_k_transposed",
    )(w_kt, hn)


def _log_sigmoid(x):
    return jnp.minimum(x, 0.0) - jnp.log1p(jnp.exp(-jnp.abs(x)))


def _split3(x):
    hi = x.astype(BF16)
    r = x - hi.astype(F32)
    mid = r.astype(BF16)
    lo = (r - mid.astype(F32)).astype(BF16)
    return hi, mid, lo


def _gates_kernel(h_ref, w_ref, wt_ref, b_ref, bt_ref, g_ref, gt_ref):
    h = h_ref[...]
    g = jnp.dot(h, w_ref[...], preferred_element_type=F32) + b_ref[...]
    gt = lax.dot_general(wt_ref[...], h, (((1,), (1,)), ((), ())),
                         preferred_element_type=F32) + bt_ref[...]
    L = L_MLSTM
    n_in = 2 * ML_HEADS
    n_fw = 3 * ML_HEADS
    lane = lax.broadcasted_iota(I32, (L, LANES), 1)
    sub = lax.broadcasted_iota(I32, (N_GATES, L), 0)
    ri = lax.broadcasted_iota(I32, (L, L), 0)
    ci = lax.broadcasted_iota(I32, (L, L), 1)
    lower = jnp.where(ci <= ri, 1.0, 0.0).astype(BF16)
    upper = jnp.where(ci >= ri, 1.0, 0.0).astype(BF16)
    nt = (((1,), (1,)), ((), ()))
    for c in range(g.shape[0] // L):
        gc = g[c * L:(c + 1) * L, :]
        pre = jnp.zeros((L, LANES), F32)
        suf = jnp.zeros((L, LANES), F32)
        for piece in _split3(jnp.where(lane >= n_in, _log_sigmoid(gc), 0.0)):
            pre = pre + jnp.dot(lower, piece, preferred_element_type=F32)
            suf = suf + jnp.dot(upper, piece, preferred_element_type=F32)
        g_ref[c * L:(c + 1) * L, :] = jnp.where(lane < n_in, gc, jnp.where(lane < n_fw, pre, suf))

        gtc = gt[:, c * L:(c + 1) * L]
        pre = jnp.zeros((N_GATES, L), F32)
        suf = jnp.zeros((N_GATES, L), F32)
        for piece in _split3(jnp.where(sub >= n_in, _log_sigmoid(gtc), 0.0)):
            pre = pre + lax.dot_general(piece, lower, nt, preferred_element_type=F32)
            suf = suf + lax.dot_general(piece, upper, nt, preferred_element_type=F32)
        gt_ref[:, c * L:(c + 1) * L] = jnp.where(sub < n_in, gtc, jnp.where(sub < n_fw, pre, suf))


def _gates(hn, w_g, w_gt, b_g, b_gt):
    n = hn.shape[0]
    tm = TM_PROJ
    return pl.pallas_call(
        _gates_kernel,
        grid=(n // tm,),
        in_specs=[pl.BlockSpec((tm, D_MODEL), lambda i: (i, 0)),
                  pl.BlockSpec((D_MODEL, LANES), lambda i: (0, 0)),
                  pl.BlockSpec((N_GATES, D_MODEL), lambda i: (0, 0)),
                  pl.BlockSpec((1, LANES), lambda i: (0, 0)),
                  pl.BlockSpec((N_GATES, 1), lambda i: (0, 0))],
        out_specs=[pl.BlockSpec((tm, LANES), lambda i: (i, 0)),
                   pl.BlockSpec((N_GATES, tm), lambda i: (0, i))],
        out_shape=[jax.ShapeDtypeStruct((n, LANES), F32),
                   jax.ShapeDtypeStruct((N_GATES, n), F32)],
        compiler_params=_cparams(("parallel",)),
        name="ml_gates",
    )(hn, w_g, w_gt, b_g, b_gt)


def _diffattn_kernel(q_ref, k_ref, v_ref, lamv_ref, g_ref, o_ref, va_ref, *, seq, lam_init):
    tq = q_ref.shape[0]
    tk = min(TK_ATT, seq)
    n_chunks = seq // tk
    unroll = min(n_chunks, ATT_UNROLL)

    @pl.when(pl.program_id(2) == 0)
    def _():
        va_ref[:, :DA_HEAD_W] = v_ref[...]
        va_ref[:, DA_HEAD_W:] = jnp.ones((seq, DA_HEAD_W), BF16)

    q = q_ref[...]
    lane = lax.broadcasted_iota(I32, q.shape, 1)
    zero = jnp.zeros_like(q)
    qs = (jnp.where(lane < DA_DIM, q, zero), jnp.where(lane >= DA_DIM, q, zero))

    def chunk(start, carry):
        kc = k_ref[pl.ds(start, tk), :]
        vc = va_ref[pl.ds(start, tk), :]
        out = []
        for qm, (m, l, a) in zip(qs, carry):
            s = lax.dot_general(qm, kc, (((1,), (1,)), ((), ())), preferred_element_type=F32)
            mn = jnp.maximum(m, jnp.max(s, axis=-1, keepdims=True))
            alpha = jnp.exp2(m - mn)
            p = jnp.exp2((s - mn).astype(BF16))
            pv = jnp.dot(p, vc, preferred_element_type=F32)
            out.append((mn, alpha * l + pv[:, DA_HEAD_W:],
                        alpha * a + pv[:, :DA_HEAD_W]))
        return tuple(out)

    def body(c, carry):
        for u in range(unroll):
            carry = chunk(pl.multiple_of((c * unroll + u) * tk, tk), carry)
        return carry

    init = (jnp.full((tq, 1), -jnp.inf, F32), jnp.zeros((tq, DA_HEAD_W), F32),
            jnp.zeros((tq, DA_HEAD_W), F32))
    (m0, l0, a0), (m1, l1, a1) = lax.fori_loop(0, n_chunks // unroll, body, (init, init))

    lv = lamv_ref[...]
    lam = (jnp.exp(jnp.sum(lv[0:1, :] * lv[1:2, :], axis=-1, keepdims=True))
           - jnp.exp(jnp.sum(lv[2:3, :] * lv[3:4, :], axis=-1, keepdims=True)) + lam_init)
    o = a0 * (1.0 / l0) - lam * (a1 * (1.0 / l1))
    o_ref[...] = (_rms(o, g_ref[...]) * (1.0 - lam_init)).astype(BF16)


def _diffattn(z, lamv, g_da, batch, seq, lam_init):
    n = z.shape[0]
    tq = TQ_ATT
    nq = seq // tq
    hq = Z_DA_Q // DA_HEAD_W
    hk = Z_DA_K // DA_HEAD_W
    hv = Z_DA_V // DA_HEAD_W
    return pl.pallas_call(
        functools.partial(_diffattn_kernel, seq=seq, lam_init=lam_init),
        grid=(batch, DA_HEADS, nq),
        in_specs=[
            pl.BlockSpec((tq, DA_HEAD_W), lambda b, h, i: (b * nq + i, hq + h)),
            pl.BlockSpec((seq, DA_HEAD_W), lambda b, h, i: (b, hk + h)),
            pl.BlockSpec((seq, DA_HEAD_W), lambda b, h, i: (b, hv + h)),
            pl.BlockSpec((4, DA_DIM), lambda b, h, i: (0, 0)),
            pl.BlockSpec((1, DA_HEAD_W), lambda b, h, i: (0, 0)),
        ],
        out_specs=pl.BlockSpec((tq, DA_HEAD_W), lambda b, h, i: (b * nq + i, h)),
        out_shape=jax.ShapeDtypeStruct((n, DA_HEADS * DA_HEAD_W), BF16),
        scratch_shapes=[pltpu.VMEM((seq, 2 * DA_HEAD_W), BF16)],
        compiler_params=_cparams(("parallel", "parallel", "arbitrary")),
        name="diff_attention",
    )(z, z, z, lamv, g_da)


def _mlstm_chunk(q, kt, v, g, gt, cta_ref, m_ref, h_ref, head, reverse):
    L = q.shape[0]
    col_i = head + (ML_HEADS if reverse else 0)
    col_f = col_i + 2 * ML_HEADS
    lane = lax.broadcasted_iota(I32, g.shape, 1)
    sub = lax.broadcasted_iota(I32, gt.shape, 0)
    i_col = jnp.sum(jnp.where(lane == col_i, g, 0.0), axis=1, keepdims=True)
    g_col = jnp.sum(jnp.where(lane == col_f, g, 0.0), axis=1, keepdims=True)
    i_row = jnp.sum(jnp.where(sub == col_i, gt, 0.0), axis=0, keepdims=True)
    g_row = jnp.sum(jnp.where(sub == col_f, gt, 0.0), axis=0, keepdims=True)

    ri = lax.broadcasted_iota(I32, (L, L), 0)
    ci = lax.broadcasted_iota(I32, (L, L), 1)
    tri = (ci >= ri) if reverse else (ci <= ri)
    last = 0 if reverse else L - 1
    lane_l = lax.broadcasted_iota(I32, (1, L), 1)
    g_last = jnp.sum(jnp.where(lane_l == last, g_row, 0.0), axis=1, keepdims=True)

    m_prev = m_ref[...]
    dmat = jnp.where(tri, g_col + (i_row - g_row), -jnp.inf)
    a_col = g_col + m_prev
    mj = jnp.maximum(a_col, jnp.max(dmat, axis=1, keepdims=True))
    w_inter = jnp.exp(a_col - mj)
    s = jnp.dot(q, kt, preferred_element_type=F32) * jnp.exp(dmat - mj)
    ones = jnp.ones((L, LANES), BF16)
    cta = cta_ref[...]
    num = (w_inter * jnp.dot(q, cta.astype(BF16), preferred_element_type=F32)
           + jnp.dot(s.astype(BF16), jnp.concatenate([v, ones], axis=1), preferred_element_type=F32))
    inv = 1.0 / jnp.maximum(jnp.abs(num[:, ML_DIM:]), jnp.exp(-mj))
    h_ref[...] = (num[:, :ML_DIM] * jnp.concatenate([inv] * (ML_DIM // LANES), axis=1)).astype(BF16)

    dec_row = g_last - g_row + i_row
    dec_col = g_last - g_col + i_col
    m_new = jnp.maximum(g_last + m_prev, jnp.max(dec_row, axis=1, keepdims=True))
    wk_col = jnp.exp(dec_col - m_new)
    carry_scale = jnp.exp(g_last + m_prev - m_new)
    wv = jnp.concatenate([(wk_col * v.astype(F32)).astype(BF16),
                          jnp.broadcast_to(wk_col, (L, LANES)).astype(BF16)], axis=1)
    cta_ref[...] = carry_scale * cta + jnp.dot(kt, wv, preferred_element_type=F32)
    m_ref[...] = m_new


def _mlstm_kernel(qf_ref, ktf_ref, vf_ref, gf_ref, gtf_ref, qb_ref, ktb_ref, vb_ref, gb_ref, gtb_ref,
                  hf_ref, hb_ref, ctf_ref, mf_ref, ctb_ref, mb_ref):
    @pl.when(pl.program_id(1) == 0)
    def _():
        ctf_ref[...] = jnp.zeros_like(ctf_ref)
        mf_ref[...] = jnp.zeros_like(mf_ref)
        ctb_ref[...] = jnp.zeros_like(ctb_ref)
        mb_ref[...] = jnp.zeros_like(mb_ref)

    for head in range(ML_HEADS):
        sl = slice(head * ML_DIM, (head + 1) * ML_DIM)
        _mlstm_chunk(qf_ref[:, sl], ktf_ref[sl, :], vf_ref[:, sl], gf_ref[...], gtf_ref[...],
                     ctf_ref.at[head], mf_ref.at[head], hf_ref.at[:, sl], head, False)
        _mlstm_chunk(qb_ref[:, sl], ktb_ref[sl, :], vb_ref[:, sl], gb_ref[...], gtb_ref[...],
                     ctb_ref.at[head], mb_ref.at[head], hb_ref.at[:, sl], head, True)


def _mlstm(z, kt, gates, gates_t, batch, seq):
    n = z.shape[0]
    L = L_MLSTM
    nc = seq // L
    width = ML_HEADS * ML_DIM
    cq = Z_ML_Q // width
    cv = Z_ML_V // width

    def specs(tix):
        return [
            pl.BlockSpec((L, width), lambda b, t: (b * nc + tix(t), cq)),
            pl.BlockSpec((width, L), lambda b, t: (0, b * nc + tix(t))),
            pl.BlockSpec((L, width), lambda b, t: (b * nc + tix(t), cv)),
            pl.BlockSpec((L, LANES), lambda b, t: (b * nc + tix(t), 0)),
            pl.BlockSpec((N_GATES, L), lambda b, t: (0, b * nc + tix(t))),
        ]

    fwd = lambda t: t
    bwd = lambda t: nc - 1 - t
    out = jax.ShapeDtypeStruct((n, width), BF16)
    state = [pltpu.VMEM((ML_HEADS, ML_DIM, ML_DIM + LANES), F32), pltpu.VMEM((ML_HEADS, 1, 1), F32)]
    return pl.pallas_call(
        _mlstm_kernel,
        grid=(batch, nc),
        in_specs=specs(fwd) + specs(bwd),
        out_specs=[pl.BlockSpec((L, width), lambda b, t: (b * nc + fwd(t), 0)),
                   pl.BlockSpec((L, width), lambda b, t: (b * nc + bwd(t), 0))],
        out_shape=[out, out],
        scratch_shapes=state + state,
        compiler_params=_cparams(("parallel", "arbitrary")),
        name="mlstm",
    )(z, kt, z, gates, gates_t, z, kt, z, gates, gates_t)


def _memkv_kernel(m_ref, g_ref, w_ref, o_ref):
    mn = _rms(m_ref[...], g_ref[...]).astype(BF16)
    o_ref[...] = jnp.dot(mn, w_ref[...], preferred_element_type=F32).astype(BF16)


def _memkv(mem2d, g_mem, w_xkv, mem_tokens):
    n = mem2d.shape[0]
    return pl.pallas_call(
        _memkv_kernel,
        grid=(n // mem_tokens,),
        in_specs=[pl.BlockSpec((mem_tokens, D_MODEL), lambda b: (b, 0)),
                  pl.BlockSpec((1, D_MODEL), lambda b: (0, 0)),
                  pl.BlockSpec((D_MODEL, 2 * D_MODEL), lambda b: (0, 0))],
        out_specs=pl.BlockSpec((mem_tokens, 2 * D_MODEL), lambda b: (b, 0)),
        out_shape=jax.ShapeDtypeStruct((n, 2 * D_MODEL), BF16),
        compiler_params=_cparams(("parallel",)),
        name="mem_kv",
    )(mem2d, g_mem, w_xkv)


def _sigmoid(x):
    return 1.0 / (1.0 + jnp.exp(-x))


def _mix_kernel(x_ref, oa_ref, hf_ref, hb_ref, mo_ref, bga_ref, bgb_ref, kv_ref,
                gml_ref, gxa_ref, gffn_ref, wpa_ref, wpb_ref, wo_ref, wxq_ref, wxo_ref, wrt_ref,
                x2_ref, hn_ref, afft_ref):
    hsum = hf_ref[...].astype(F32) + hb_ref[...].astype(F32)
    gml = gml_ref[...]
    parts = []
    for hh in range(ML_HEADS):
        sl = slice(hh * ML_DIM, (hh + 1) * ML_DIM)
        parts.append(_rms(hsum[:, sl], gml[:, sl]))
    hb = jnp.concatenate(parts, axis=1)
    hb = hb * _sigmoid(mo_ref[...].astype(F32))
    y_a = jnp.dot(oa_ref[...], wpa_ref[...], preferred_element_type=F32)
    y_b = jnp.dot(hb.astype(BF16), wpb_ref[...], preferred_element_type=F32)
    merged = (_sigmoid(bga_ref[...].astype(F32)) * y_a
              + _sigmoid(bgb_ref[...].astype(F32)) * y_b)
    x1 = x_ref[...] + jnp.dot(merged.astype(BF16), wo_ref[...], preferred_element_type=F32)

    hq = _rms(x1, gxa_ref[...]).astype(BF16)
    qx = (jnp.dot(hq, wxq_ref[...], preferred_element_type=F32) * (XA_DIM ** -0.5)).astype(BF16)
    kv = kv_ref[...]
    outs = []
    for hh in range(XA_HEADS):
        sl = slice(hh * XA_DIM, (hh + 1) * XA_DIM)
        kh = kv[:, sl]
        vh = kv[:, D_MODEL + hh * XA_DIM:D_MODEL + (hh + 1) * XA_DIM]
        s = lax.dot_general(qx[:, sl], kh, (((1,), (1,)), ((), ())), preferred_element_type=F32)
        p = jnp.exp(s - jnp.max(s, axis=-1, keepdims=True))
        p = p / jnp.sum(p, axis=-1, keepdims=True)
        outs.append(jnp.dot(p.astype(BF16), vh, preferred_element_type=F32).astype(BF16))
    o = jnp.concatenate(outs, axis=1)
    x2 = x1 + jnp.dot(o, wxo_ref[...], preferred_element_type=F32)
    x2_ref[...] = x2

    hn = _rms(x2, gffn_ref[...])
    hn_ref[...] = hn.astype(BF16)
    logits_t = lax.dot_general(wrt_ref[...], hn, (((1,), (1,)), ((), ())),
                               precision=lax.Precision.HIGHEST, preferred_element_type=F32)
    e = jnp.exp(logits_t - jnp.max(logits_t, axis=0, keepdims=True))
    afft_ref[...] = e / jnp.sum(e, axis=0, keepdims=True)


def _mix(x2d, oa, h_fw, h_bw, z, kv, p, batch, seq, mem_tokens):
    n = x2d.shape[0]
    tm = TM_MIX
    nt = seq // tm
    cmo = Z_ML_O // D_MODEL
    cbg = Z_BR_G // D_MODEL
    full = lambda r, c: pl.BlockSpec((r, c), lambda i: (0, 0), pipeline_mode=pl.Buffered(1))
    return pl.pallas_call(
        _mix_kernel,
        grid=(n // tm,),
        in_specs=[
            pl.BlockSpec((tm, D_MODEL), lambda i: (i, 0)),
            pl.BlockSpec((tm, D_MODEL), lambda i: (i, 0)),
            pl.BlockSpec((tm, D_MODEL), lambda i: (i, 0)),
            pl.BlockSpec((tm, D_MODEL), lambda i: (i, 0)),
            pl.BlockSpec((tm, D_MODEL), lambda i: (i, cmo)),
            pl.BlockSpec((tm, D_MODEL), lambda i: (i, cbg)),
            pl.BlockSpec((tm, D_MODEL), lambda i: (i, cbg + 1)),
            pl.BlockSpec((mem_tokens, 2 * D_MODEL), lambda i: (i // nt, 0)),
            full(1, D_MODEL), full(1, D_MODEL), full(1, D_MODEL),
            full(D_MODEL, D_MODEL), full(D_MODEL, D_MODEL), full(D_MODEL, D_MODEL),
            full(D_MODEL, D_MODEL), full(D_MODEL, D_MODEL),
            full(N_EXPERTS, D_MODEL),
        ],
        out_specs=[pl.BlockSpec((tm, D_MODEL), lambda i: (i, 0)),
                   pl.BlockSpec((tm, D_MODEL), lambda i: (i, 0)),
                   pl.BlockSpec((N_EXPERTS, tm), lambda i: (0, i))],
        out_shape=[jax.ShapeDtypeStruct((n, D_MODEL), F32),
                   jax.ShapeDtypeStruct((n, D_MODEL), BF16),
                   jax.ShapeDtypeStruct((N_EXPERTS, n), F32)],
        compiler_params=_cparams(("parallel",)),
        name="mix_xattn_router",
    )(x2d, oa, h_fw, h_bw, z, z, z, kv,
      p["g_ml_out"], p["g_xa"], p["g_ffn"], p["w_proj_a"], p["w_proj_b"], p["w_out"],
      p["w_xq"], p["w_xo"], p["w_router_t"])


def _prefix_incl(mask_f, upper, ones, lower):
    mb = mask_f.astype(BF16)
    lane_incl = jnp.dot(mb, upper, preferred_element_type=F32)
    row_tot = jnp.dot(mb, ones, preferred_element_type=F32)
    row_off = jnp.dot(lower, row_tot.astype(BF16), preferred_element_type=F32)
    return lane_incl + row_off, row_off


def _select_kernel(a_ref, pos_ref, off_ref, *, cap):
    a = a_ref[...]
    rows = a.shape[0]
    bits = pltpu.bitcast(a, I32)

    def body(i, thr):
        cand = thr | jnp.left_shift(jnp.int32(1), 30 - i)
        cnt = jnp.sum(jnp.where(bits >= cand, 1.0, 0.0), keepdims=True)
        return jnp.where(cnt >= cap, cand, thr)

    thr = lax.fori_loop(0, 31, body, jnp.zeros((1, 1), I32))
    gt = bits > thr
    eq = bits == thr
    n_gt = jnp.sum(jnp.where(gt, 1.0, 0.0), keepdims=True)
    need = cap - n_gt

    li = lax.broadcasted_iota(I32, (LANES, LANES), 0)
    lj = lax.broadcasted_iota(I32, (LANES, LANES), 1)
    upper = (li <= lj).astype(BF16)
    ones = jnp.ones((LANES, LANES), BF16)
    ri = lax.broadcasted_iota(I32, (rows, rows), 0)
    rj = lax.broadcasted_iota(I32, (rows, rows), 1)
    lower = (rj < ri).astype(BF16)

    eq_rank, _ = _prefix_incl(jnp.where(eq, 1.0, 0.0), upper, ones, lower)
    sel = gt | (eq & (eq_rank <= need))
    incl, row_off = _prefix_incl(jnp.where(sel, 1.0, 0.0), upper, ones, lower)
    pos_ref[...] = jnp.where(sel, incl - 1.0, -1.0).astype(I32)
    off_ref[...] = row_off.astype(I32)


def _select(aff_t, n_tokens, cap):
    rows = n_tokens // LANES
    a3 = aff_t.reshape(N_EXPERTS, rows, LANES)
    spec = pl.BlockSpec((None, rows, LANES), lambda e: (e, 0, 0))
    return pl.pallas_call(
        functools.partial(_select_kernel, cap=float(cap)),
        grid=(N_EXPERTS,),
        in_specs=[spec],
        out_specs=[spec, spec],
        out_shape=[jax.ShapeDtypeStruct((N_EXPERTS, rows, LANES), I32),
                   jax.ShapeDtypeStruct((N_EXPERTS, rows, LANES), I32)],
        compiler_params=_cparams(("parallel",)),
        name="expert_choice_select",
    )(a3)


def _dispatch_kernel(tile_ref, blk_ref, flag_ref, x_ref, pos_ref, o_ref, *, n_items):
    e = pl.program_id(0)
    w = pl.program_id(1)
    idx = e * n_items + w
    flags = flag_ref[idx]
    blk = blk_ref[idx]

    @pl.when((flags & 2) != 0)
    def _():
        o_ref[...] = jnp.zeros_like(o_ref)

    @pl.when((flags & 1) != 0)
    def _():
        rel = pos_ref[...] - blk * B_SLOT
        slot = lax.broadcasted_iota(I32, (B_SLOT, rel.shape[1]), 0)
        onehot = jnp.where(slot == rel, 1.0, 0.0).astype(BF16)
        rows = jnp.dot(onehot, x_ref[...], preferred_element_type=F32)
        o_ref[...] = (o_ref[...].astype(F32) + rows).astype(BF16)


def _dispatch(hn, pos_t3, tile_of, blk_of, flags, cap, n_items):
    n = hn.shape[0]
    t = min(T_DISP, n)
    grid_spec = pltpu.PrefetchScalarGridSpec(
        num_scalar_prefetch=3,
        grid=(N_EXPERTS, n_items),
        in_specs=[
            pl.BlockSpec((t, D_MODEL), lambda e, w, tl, bl, fl: (tl[e * n_items + w], 0)),
            pl.BlockSpec((None, 1, t), lambda e, w, tl, bl, fl: (e, 0, tl[e * n_items + w])),
        ],
        out_specs=pl.BlockSpec((None, B_SLOT, D_MODEL),
                               lambda e, w, tl, bl, fl: (e, bl[e * n_items + w], 0)),
    )
    return pl.pallas_call(
        functools.partial(_dispatch_kernel, n_items=n_items),
        grid_spec=grid_spec,
        out_shape=jax.ShapeDtypeStruct((N_EXPERTS, cap, D_MODEL), BF16),
        compiler_params=_cparams(("arbitrary", "arbitrary")),
        name="dispatch",
    )(tile_of, blk_of, flags, hn, pos_t3)


def _dispatch_tables(row_off, n_tokens, cap):
    t = min(T_DISP, n_tokens)
    n_tiles = n_tokens // t
    n_blocks = cap // B_SLOT
    n_items = n_tiles + n_blocks
    start = row_off[:, ::t // LANES, 0]
    end = jnp.concatenate([start[:, 1:], jnp.full((N_EXPERTS, 1), cap, I32)], axis=1)
    nonempty = end > start
    b_lo = jnp.minimum(start // B_SLOT, n_blocks - 1)
    b_hi = jnp.maximum(end - 1, 0) // B_SLOT
    npairs = jnp.where(nonempty, b_hi - b_lo + 1, 0)
    cum = jnp.cumsum(npairs, axis=1)
    base = cum - npairs
    total = cum[:, -1:]
    w = jnp.arange(n_items, dtype=I32)[None, :]
    tile = jnp.sum((cum[:, None, :] <= w[:, :, None]).astype(I32), axis=-1)
    tile = jnp.minimum(tile, n_tiles - 1)
    blk = jnp.take_along_axis(b_lo, tile, axis=1) + (w - jnp.take_along_axis(base, tile, axis=1))
    valid = w < total
    last_tile = jnp.take_along_axis(tile, jnp.maximum(total - 1, 0), axis=1)
    tile = jnp.where(valid, tile, last_tile)
    blk = jnp.where(valid, blk, n_blocks - 1)
    prev = jnp.concatenate([jnp.full((N_EXPERTS, 1), -1, I32), blk[:, :-1]], axis=1)
    flags = valid.astype(I32) + 2 * (blk != prev).astype(I32)
    return (tile.reshape(-1).astype(I32), blk.reshape(-1).astype(I32),
            flags.reshape(-1).astype(I32), n_items)


def _ffn_kernel(x_ref, wg_ref, wu_ref, wd_ref, o_ref):
    x = x_ref[...]
    acc = jnp.zeros((x.shape[0], D_MODEL), F32)
    for c in range(D_EXPERT // F_CHUNK):
        sl = slice(c * F_CHUNK, (c + 1) * F_CHUNK)
        gte = jnp.dot(x, wg_ref[:, sl], preferred_element_type=F32)
        up = jnp.dot(x, wu_ref[:, sl], preferred_element_type=F32)
        hid = gte * (1.0 / (1.0 + jnp.exp(-gte))) * up
        acc = acc + jnp.dot(hid.astype(BF16), wd_ref[sl, :], preferred_element_type=F32)
    o_ref[...] = acc.astype(BF16)


def _ffn(xe, w_gate, w_up, w_down, cap):
    tm = min(TM_FFN, cap)
    return pl.pallas_call(
        _ffn_kernel,
        grid=(N_EXPERTS, cap // tm),
        in_specs=[pl.BlockSpec((None, tm, D_MODEL), lambda e, i: (e, i, 0)),
                  pl.BlockSpec((None, D_MODEL, D_EXPERT), lambda e, i: (e, 0, 0)),
                  pl.BlockSpec((None, D_MODEL, D_EXPERT), lambda e, i: (e, 0, 0)),
                  pl.BlockSpec((None, D_EXPERT, D_MODEL), lambda e, i: (e, 0, 0))],
        out_specs=pl.BlockSpec((None, tm, D_MODEL), lambda e, i: (e, i, 0)),
        out_shape=jax.ShapeDtypeStruct((N_EXPERTS, cap, D_MODEL), BF16),
        compiler_params=_cparams(("parallel", "arbitrary")),
        name="expert_ffn",
    )(xe, w_gate, w_up, w_down)


def _combine_kernel(b0_ref, st_ref, x_ref, pos_ref, aff_ref, g_ref, *rest):
    ye_refs = rest[:2 * N_EXPERTS]
    o_ref = rest[2 * N_EXPERTS]
    t = pl.program_id(0)
    tt = pos_ref.shape[0]
    slot = lax.broadcasted_iota(I32, (tt, B_SLOT), 1)

    def gated_onehot(e, kk):
        rel = pos_ref[:, e:e + 1] - (b0_ref[t * N_EXPERTS + e] + kk) * B_SLOT
        return jnp.where(slot == rel, aff_ref[:, e:e + 1], 0.0).astype(BF16)

    acc = x_ref[...]
    for e in range(N_EXPERTS):
        acc = acc + jnp.dot(gated_onehot(e, 0), ye_refs[2 * e][...], preferred_element_type=F32)
    o_ref[...] = acc
    for e in range(N_EXPERTS):
        @pl.when(st_ref[t * N_EXPERTS + e] != 0)
        def _(e=e):
            o_ref[...] += jnp.dot(gated_onehot(e, 1), ye_refs[2 * e + 1][...],
                                  preferred_element_type=F32)
    o_ref[...] = _rms(o_ref[...], g_ref[...])


def _combine(x2, pos_c, aff_c, ye, b0, straddle, g_final, cap):
    n = x2.shape[0]
    tt = T_COMB
    n_blocks = cap // B_SLOT

    def ye_spec(e, kk):
        return pl.BlockSpec(
            (None, B_SLOT, D_MODEL),
            lambda t, b0r, st: (e, jnp.minimum(b0r[t * N_EXPERTS + e] + kk, n_blocks - 1), 0))

    ye_specs = [ye_spec(e, kk) for e in range(N_EXPERTS) for kk in range(2)]
    grid_spec = pltpu.PrefetchScalarGridSpec(
        num_scalar_prefetch=2,
        grid=(n // tt,),
        in_specs=[pl.BlockSpec((tt, D_MODEL), lambda t, b0r, st: (t, 0)),
                  pl.BlockSpec((tt, N_EXPERTS), lambda t, b0r, st: (t, 0)),
                  pl.BlockSpec((tt, N_EXPERTS), lambda t, b0r, st: (t, 0)),
                  pl.BlockSpec((1, D_MODEL), lambda t, b0r, st: (0, 0))] + ye_specs,
        out_specs=pl.BlockSpec((tt, D_MODEL), lambda t, b0r, st: (t, 0)),
    )
    return pl.pallas_call(
        _combine_kernel,
        grid_spec=grid_spec,
        out_shape=jax.ShapeDtypeStruct((n, D_MODEL), F32),
        compiler_params=_cparams(("arbitrary",)),
        name="combine_final_norm",
    )(b0, straddle, x2, pos_c, aff_c, g_final, *([ye] * (2 * N_EXPERTS)))


def _encoder(x, mem, p):
    batch, seq, _ = x.shape
    mem_tokens = mem.shape[1]
    n = batch * seq
    cap = CAPACITY_FACTOR * n // N_EXPERTS
    x2d = x.reshape(n, D_MODEL)

    half = DA_DIM // 2
    pos = jnp.arange(seq, dtype=F32)
    inv = ROPE_THETA ** (-jnp.arange(half, dtype=F32) / half)
    ang = pos[:, None] * inv[None, :]
    cos = jnp.cos(ang)
    sin = jnp.sin(ang)
    cos_t = jnp.tile(cos, (1, LANES // half))
    sin_t = jnp.tile(jnp.concatenate([-sin, sin], axis=1), (1, LANES // DA_DIM))

    z, hn = _inproj(x2d, p["g_mix"], p["w_main"], p["colscale"], cos_t, sin_t, seq)
    kt = _ktrans(hn, p["w_kt"])
    gates, gates_t = _gates(hn, p["w_g"], p["w_gt"], p["b_g"], p["b_gt"])

    oa = _diffattn(z, p["lamv"], p["g_da_out"], batch, seq, p["lam_init"])
    h_fw, h_bw = _mlstm(z, kt, gates, gates_t, batch, seq)

    kv = _memkv(mem.reshape(batch * mem_tokens, D_MODEL), p["g_mem"], p["w_xkv"], mem_tokens)
    x2, hffn, aff_t = _mix(x2d, oa, h_fw, h_bw, z, kv, p, batch, seq, mem_tokens)

    pos3, off3 = _select(aff_t, n, cap)
    pos_t = pos3.reshape(N_EXPERTS, n)
    tile_of, blk_of, flags, n_items = _dispatch_tables(off3, n, cap)
    xe = _dispatch(hffn, pos_t.reshape(N_EXPERTS, 1, n), tile_of, blk_of, flags, cap, n_items)
    ye = _ffn(xe, p["w_e_gate"], p["w_e_up"], p["w_e_down"], cap)

    n_blocks = cap // B_SLOT
    start = off3[:, ::T_COMB // LANES, 0]
    end = jnp.concatenate([start[:, 1:], jnp.full((N_EXPERTS, 1), cap, I32)], axis=1)
    b0 = jnp.minimum(start // B_SLOT, n_blocks - 1)
    straddle = (end > start) & ((end - 1) // B_SLOT > b0)
    y = _combine(x2, pos_t.T, aff_t.T, ye, b0.T.reshape(-1).astype(I32),
                 straddle.T.reshape(-1).astype(I32), p["g_final"], cap)
    return y.reshape(batch, seq, D_MODEL)


def kernel(x_prompt, x_sample, mem_prompt, mem_sample, g_mix, w_in, lam_q1, lam_k1, lam_q2, lam_k2,
           g_da_out, b_ml_gates, g_ml_out, w_proj_a, w_proj_b, w_out, g_xa, g_mem, w_xq, w_xkv, w_xo,
           g_ffn, w_router, w_e_gate, w_e_up, w_e_down, g_final):
    l = 0
    w = w_in[l]
    assert w_in.shape[0] == 1, "single-layer encoder"
    gate_lo = W_GATES
    gate_hi = gate_lo + N_GATES
    colscale = jnp.ones((1, Z_WIDTH), F32)
    colscale = colscale.at[:, Z_DA_Q:Z_DA_K].set(DA_DIM ** -0.5 * math.log2(math.e))
    w_g = w[:, gate_lo:gate_hi]
    b = b_ml_gates[l].astype(F32)
    p = {
        "g_mix": g_mix[l][None, :],
        "w_main": jnp.concatenate([w[:, :W_ML_K], w[:, W_ML_V:gate_lo], w[:, gate_hi:]],
                                  axis=1).astype(BF16),
        "colscale": colscale,
        "w_kt": w[:, W_ML_K:W_ML_V].T.astype(BF16),
        "w_g": jnp.pad(w_g, ((0, 0), (0, LANES - N_GATES))).astype(BF16),
        "w_gt": w_g.T.astype(BF16),
        "b_g": jnp.pad(b, (0, LANES - N_GATES))[None, :],
        "b_gt": b[:, None],
        "lamv": jnp.stack([lam_q1[l], lam_k1[l], lam_q2[l], lam_k2[l]]).astype(F32),
        "lam_init": 0.8 - 0.6 * math.exp(-0.3 * l),
        "g_da_out": g_da_out[l][None, :],
        "g_ml_out": g_ml_out[l][None, :],
        "w_proj_a": w_proj_a[l].astype(BF16),
        "w_proj_b": w_proj_b[l].astype(BF16),
        "w_out": w_out[l].astype(BF16),
        "g_xa": g_xa[l][None, :],
        "g_mem": g_mem[l][None, :],
        "w_xq": w_xq[l].astype(BF16),
        "w_xkv": w_xkv[l].astype(BF16),
        "w_xo": w_xo[l].astype(BF16),
        "g_ffn": g_ffn[l][None, :],
        "w_router_t": w_router[l].T.astype(F32),
        "w_e_gate": w_e_gate[l].astype(BF16),
        "w_e_up": w_e_up[l].astype(BF16),
        "w_e_down": w_e_down[l].astype(BF16),
        "g_final": g_final[None, :],
    }
    y_prompt = _encoder(x_prompt, mem_prompt, p)
    y_sample = _encoder(x_sample, mem_sample, p)
    return (y_prompt, y_sample)
```

```python
import functools
import math

import jax
import jax.numpy as jnp
from jax import lax
from jax.experimental import pallas as pl
from jax.experimental.pallas import tpu as pltpu

F32 = jnp.float32
BF16 = jnp.bfloat16
I32 = jnp.int32

D_MODEL = 1024
DA_HEADS = 8
DA_DIM = 64
DA_HEAD_W = 2 * DA_DIM
ML_HEADS = 4
ML_DIM = 256
XA_HEADS = 4
XA_DIM = D_MODEL // XA_HEADS
N_EXPERTS = 16
CAPACITY_FACTOR = 2
D_EXPERT = 2048
EPS = 1e-6
ROPE_THETA = 10000.0
N_GATES = 4 * ML_HEADS

Z_DA_Q = 0
Z_DA_K = 1024
Z_DA_V = 2048
Z_ML_Q = 3072
Z_ML_V = 4096
Z_ML_O = 5120
Z_BR_G = 6144
Z_WIDTH = 8192
W_ML_K = 4096
W_ML_V = 5120
W_GATES = 7168

LANES = 128
VMEM_LIMIT = 56 * 1024 * 1024

TM_PROJ = 1024
TN_PROJ = 512
TQ_ATT = 1024
TK_ATT = 512
ATT_UNROLL = 4
L_MLSTM = 256
TM_MIX = 512
T_DISP = 256
B_SLOT = 256
TM_FFN = 512
F_CHUNK = 512
T_COMB = 256


def _cparams(sem):
    return pltpu.CompilerParams(dimension_semantics=sem, vmem_limit_bytes=VMEM_LIMIT)


def _rms(xf, g):
    return xf * lax.rsqrt(jnp.mean(xf * xf, axis=-1, keepdims=True) + EPS) * g


def _inproj_kernel(x_ref, g_ref, w_ref, cs_ref, cos_ref, sin_ref, z_ref, hn_ref, hs_ref, *, n_rope):
    j = pl.program_id(1)

    @pl.when(j == 0)
    def _():
        hb = _rms(x_ref[...], g_ref[...]).astype(BF16)
        hs_ref[...] = hb
        hn_ref[...] = hb

    acc = jnp.dot(hs_ref[...], w_ref[...], preferred_element_type=F32) * cs_ref[...]

    @pl.when(j >= n_rope)
    def _():
        z_ref[...] = acc.astype(BF16)

    @pl.when(j < n_rope)
    def _():
        cos = cos_ref[...]
        sin = sin_ref[...]
        lane = lax.broadcasted_iota(I32, cos.shape, 1)
        first_half = (lane % DA_DIM) < (DA_DIM // 2)
        for hh in range(acc.shape[1] // LANES):
            a = acc[:, hh * LANES:(hh + 1) * LANES]
            rot = jnp.where(first_half, pltpu.roll(a, LANES - DA_DIM // 2, 1),
                            pltpu.roll(a, DA_DIM // 2, 1))
            z_ref[:, hh * LANES:(hh + 1) * LANES] = (a * cos + rot * sin).astype(BF16)


def _inproj(x2d, g_mix, w_main, colscale, cos_t, sin_t, seq):
    n = x2d.shape[0]
    tm, tn = TM_PROJ, TN_PROJ
    n_seq_tiles = seq // tm
    n_rope = (2 * D_MODEL) // tn
    return pl.pallas_call(
        functools.partial(_inproj_kernel, n_rope=n_rope),
        grid=(n // tm, Z_WIDTH // tn),
        in_specs=[
            pl.BlockSpec((tm, D_MODEL), lambda i, j: (i, 0)),
            pl.BlockSpec((1, D_MODEL), lambda i, j: (0, 0)),
            pl.BlockSpec((D_MODEL, tn), lambda i, j: (0, j)),
            pl.BlockSpec((1, tn), lambda i, j: (0, j)),
            pl.BlockSpec((tm, LANES), lambda i, j: (i % n_seq_tiles, 0)),
            pl.BlockSpec((tm, LANES), lambda i, j: (i % n_seq_tiles, 0)),
        ],
        out_specs=[
            pl.BlockSpec((tm, tn), lambda i, j: (i, j)),
            pl.BlockSpec((tm, D_MODEL), lambda i, j: (i, 0)),
        ],
        out_shape=[jax.ShapeDtypeStruct((n, Z_WIDTH), BF16),
                   jax.ShapeDtypeStruct((n, D_MODEL), BF16)],
        scratch_shapes=[pltpu.VMEM((tm, D_MODEL), BF16)],
        compiler_params=_cparams(("parallel", "arbitrary")),
        name="inproj",
    )(x2d, g_mix, w_main, colscale, cos_t, sin_t)


def _ktrans_kernel(w_ref, h_ref, o_ref):
    kt = lax.dot_general(w_ref[...], h_ref[...], (((1,), (1,)), ((), ())),
                         preferred_element_type=F32)
    o_ref[...] = (kt * (ML_DIM ** -0.5)).astype(BF16)


def _ktrans(hn, w_kt):
    n = hn.shape[0]
    tm = TM_PROJ
    return pl.pallas_call(
        _ktrans_kernel,
        grid=(n // tm, ML_HEADS),
        in_specs=[pl.BlockSpec((ML_DIM, D_MODEL), lambda i, h: (h, 0)),
                  pl.BlockSpec((tm, D_MODEL), lambda i, h: (i, 0))],
        out_specs=pl.BlockSpec((ML_DIM, tm), lambda i, h: (h, i)),
        out_shape=jax.ShapeDtypeStruct((ML_HEADS * ML_DIM, n), BF16),
        compiler_params=_cparams(("parallel", "arbitrary")),
        name="ml_k_transposed",
    )(w_kt, hn)


def _log_sigmoid(x):
    return jnp.minimum(x, 0.0) - jnp.log1p(jnp.exp(-jnp.abs(x)))


def _split3(x):
    hi = x.astype(BF16)
    r = x - hi.astype(F32)
    mid = r.astype(BF16)
    lo = (r - mid.astype(F32)).astype(BF16)
    return hi, mid, lo


def _gates_kernel(h_ref, w_ref, wt_ref, b_ref, bt_ref, g_ref, gt_ref):
    h = h_ref[...]
    g = jnp.dot(h, w_ref[...], preferred_element_type=F32) + b_ref[...]
    gt = lax.dot_general(wt_ref[...], h, (((1,), (1,)), ((), ())),
                         preferred_element_type=F32) + bt_ref[...]
    L = L_MLSTM
    n_in = 2 * ML_HEADS
    n_fw = 3 * ML_HEADS
    lane = lax.broadcasted_iota(I32, (L, LANES), 1)
    sub = lax.broadcasted_iota(I32, (N_GATES, L), 0)
    ri = lax.broadcasted_iota(I32, (L, L), 0)
    ci = lax.broadcasted_iota(I32, (L, L), 1)
    lower = jnp.where(ci <= ri, 1.0, 0.0).astype(BF16)
    upper = jnp.where(ci >= ri, 1.0, 0.0).astype(BF16)
    nt = (((1,), (1,)), ((), ()))
    for c in range(g.shape[0] // L):
        gc = g[c * L:(c + 1) * L, :]
        pre = jnp.zeros((L, LANES), F32)
        suf = jnp.zeros((L, LANES), F32)
        for piece in _split3(jnp.where(lane >= n_in, _log_sigmoid(gc), 0.0)):
            pre = pre + jnp.dot(lower, piece, preferred_element_type=F32)
            suf = suf + jnp.dot(upper, piece, preferred_element_type=F32)
        g_ref[c * L:(c + 1) * L, :] = jnp.where(lane < n_in, gc, jnp.where(lane < n_fw, pre, suf))

        gtc = gt[:, c * L:(c + 1) * L]
        pre = jnp.zeros((N_GATES, L), F32)
        suf = jnp.zeros((N_GATES, L), F32)
        for piece in _split3(jnp.where(sub >= n_in, _log_sigmoid(gtc), 0.0)):
            pre = pre + lax.dot_general(piece, lower, nt, preferred_element_type=F32)
            suf = suf + lax.dot_general(piece, upper, nt, preferred_element_type=F32)
        gt_ref[:, c * L:(c + 1) * L] = jnp.where(sub < n_in, gtc, jnp.where(sub < n_fw, pre, suf))


def _gates(hn, w_g, w_gt, b_g, b_gt):
    n = hn.shape[0]
    tm = TM_PROJ
    return pl.pallas_call(
        _gates_kernel,
        grid=(n // tm,),
        in_specs=[pl.BlockSpec((tm, D_MODEL), lambda i: (i, 0)),
                  pl.BlockSpec((D_MODEL, LANES), lambda i: (0, 0)),
                  pl.BlockSpec((N_GATES, D_MODEL), lambda i: (0, 0)),
                  pl.BlockSpec((1, LANES), lambda i: (0, 0)),
                  pl.BlockSpec((N_GATES, 1), lambda i: (0, 0))],
        out_specs=[pl.BlockSpec((tm, LANES), lambda i: (i, 0)),
                   pl.BlockSpec((N_GATES, tm), lambda i: (0, i))],
        out_shape=[jax.ShapeDtypeStruct((n, LANES), F32),
                   jax.ShapeDtypeStruct((N_GATES, n), F32)],
        compiler_params=_cparams(("parallel",)),
        name="ml_gates",
    )(hn, w_g, w_gt, b_g, b_gt)


def _diffattn_kernel(q_ref, k_ref, v_ref, lamv_ref, g_ref, o_ref, va_ref, *, seq, lam_init):
    tq = q_ref.shape[0]
    tk = min(TK_ATT, seq)
    n_chunks = seq // tk
    unroll = min(n_chunks, ATT_UNROLL)

    @pl.when(pl.program_id(2) == 0)
    def _():
        va_ref[:, :DA_HEAD_W] = v_ref[...]
        va_ref[:, DA_HEAD_W:] = jnp.ones((seq, DA_HEAD_W), BF16)

    q = q_ref[...]
    lane = lax.broadcasted_iota(I32, q.shape, 1)
    zero = jnp.zeros_like(q)
    qs = (jnp.where(lane < DA_DIM, q, zero), jnp.where(lane >= DA_DIM, q, zero))

    def chunk(start, carry):
        kc = k_ref[pl.ds(start, tk), :]
        vc = va_ref[pl.ds(start, tk), :]
        out = []
        for qm, (m, l, a) in zip(qs, carry):
            s = lax.dot_general(qm, kc, (((1,), (1,)), ((), ())), preferred_element_type=F32)
            mn = jnp.maximum(m, jnp.max(s, axis=-1, keepdims=True))
            alpha = jnp.exp2(m - mn)
            p = jnp.exp2((s - mn).astype(BF16))
            pv = jnp.dot(p, vc, preferred_element_type=F32)
            out.append((mn, alpha * l + pv[:, DA_HEAD_W:],
                        alpha * a + pv[:, :DA_HEAD_W]))
        return tuple(out)

    def body(c, carry):
        for u in range(unroll):
            carry = chunk(pl.multiple_of((c * unroll + u) * tk, tk), carry)
        return carry

    init = (jnp.full((tq, 1), -jnp.inf, F32), jnp.zeros((tq, DA_HEAD_W), F32),
            jnp.zeros((tq, DA_HEAD_W), F32))
    (m0, l0, a0), (m1, l1, a1) = lax.fori_loop(0, n_chunks // unroll, body, (init, init))

    lv = lamv_ref[...]
    lam = (jnp.exp(jnp.sum(lv[0:1, :] * lv[1:2, :], axis=-1, keepdims=True))
           - jnp.exp(jnp.sum(lv[2:3, :] * lv[3:4, :], axis=-1, keepdims=True)) + lam_init)
    o = a0 * (1.0 / l0) - lam * (a1 * (1.0 / l1))
    o_ref[...] = (_rms(o, g_ref[...]) * (1.0 - lam_init)).astype(BF16)


def _diffattn(z, lamv, g_da, batch, seq, lam_init):
    n = z.shape[0]
    tq = TQ_ATT
    nq = seq // tq
    hq = Z_DA_Q // DA_HEAD_W
    hk = Z_DA_K // DA_HEAD_W
    hv = Z_DA_V // DA_HEAD_W
    return pl.pallas_call(
        functools.partial(_diffattn_kernel, seq=seq, lam_init=lam_init),
        grid=(batch, DA_HEADS, nq),
        in_specs=[
            pl.BlockSpec((tq, DA_HEAD_W), lambda b, h, i: (b * nq + i, hq + h)),
            pl.BlockSpec((seq, DA_HEAD_W), lambda b, h, i: (b, hk + h)),
            pl.BlockSpec((seq, DA_HEAD_W), lambda b, h, i: (b, hv + h)),
            pl.BlockSpec((4, DA_DIM), lambda b, h, i: (0, 0)),
            pl.BlockSpec((1, DA_HEAD_W), lambda b, h, i: (0, 0)),
        ],
        out_specs=pl.BlockSpec((tq, DA_HEAD_W), lambda b, h, i: (b * nq + i, h)),
        out_shape=jax.ShapeDtypeStruct((n, DA_HEADS * DA_HEAD_W), BF16),
        scratch_shapes=[pltpu.VMEM((seq, 2 * DA_HEAD_W), BF16)],
        compiler_params=_cparams(("parallel", "parallel", "arbitrary")),
        name="diff_attention",
    )(z, z, z, lamv, g_da)


def _mlstm_chunk(q, kt, v, g, gt, cta_ref, m_ref, h_ref, head, reverse):
    L = q.shape[0]
    col_i = head + (ML_HEADS if reverse else 0)
    col_f = col_i + 2 * ML_HEADS
    lane = lax.broadcasted_iota(I32, g.shape, 1)
    sub = lax.broadcasted_iota(I32, gt.shape, 0)
    i_col = jnp.sum(jnp.where(lane == col_i, g, 0.0), axis=1, keepdims=True)
    g_col = jnp.sum(jnp.where(lane == col_f, g, 0.0), axis=1, keepdims=True)
    i_row = jnp.sum(jnp.where(sub == col_i, gt, 0.0), axis=0, keepdims=True)
    g_row = jnp.sum(jnp.where(sub == col_f, gt, 0.0), axis=0, keepdims=True)

    ri = lax.broadcasted_iota(I32, (L, L), 0)
    ci = lax.broadcasted_iota(I32, (L, L), 1)
    tri = (ci >= ri) if reverse else (ci <= ri)
    last = 0 if reverse else L - 1
    lane_l = lax.broadcasted_iota(I32, (1, L), 1)
    g_last = jnp.sum(jnp.where(lane_l == last, g_row, 0.0), axis=1, keepdims=True)

    m_prev = m_ref[...]
    dmat = jnp.where(tri, g_col + (i_row - g_row), -jnp.inf)
    a_col = g_col + m_prev
    mj = jnp.maximum(a_col, jnp.max(dmat, axis=1, keepdims=True))
    w_inter = jnp.exp(a_col - mj)
    s = jnp.dot(q, kt, preferred_element_type=F32) * jnp.exp(dmat - mj)
    ones = jnp.ones((L, LANES), BF16)
    cta = cta_ref[...]
    num = (w_inter * jnp.dot(q, cta.astype(BF16), preferred_element_type=F32)
           + jnp.dot(s.astype(BF16), jnp.concatenate([v, ones], axis=1), preferred_element_type=F32))
    inv = 1.0 / jnp.maximum(jnp.abs(num[:, ML_DIM:]), jnp.exp(-mj))
    h_ref[...] = (num[:, :ML_DIM] * jnp.concatenate([inv] * (ML_DIM // LANES), axis=1)).astype(BF16)

    dec_row = g_last - g_row + i_row
    dec_col = g_last - g_col + i_col
    m_new = jnp.maximum(g_last + m_prev, jnp.max(dec_row, axis=1, keepdims=True))
    wk_col = jnp.exp(dec_col - m_new)
    carry_scale = jnp.exp(g_last + m_prev - m_new)
    wv = jnp.concatenate([(wk_col * v.astype(F32)).astype(BF16),
                          jnp.broadcast_to(wk_col, (L, LANES)).astype(BF16)], axis=1)
    cta_ref[...] = carry_scale * cta + jnp.dot(kt, wv, preferred_element_type=F32)
    m_ref[...] = m_new


def _mlstm_kernel(qf_ref, ktf_ref, vf_ref, gf_ref, gtf_ref, qb_ref, ktb_ref, vb_ref, gb_ref, gtb_ref,
                  hf_ref, hb_ref, ctf_ref, mf_ref, ctb_ref, mb_ref):
    @pl.when(pl.program_id(1) == 0)
    def _():
        ctf_ref[...] = jnp.zeros_like(ctf_ref)
        mf_ref[...] = jnp.zeros_like(mf_ref)
        ctb_ref[...] = jnp.zeros_like(ctb_ref)
        mb_ref[...] = jnp.zeros_like(mb_ref)

    for head in range(ML_HEADS):
        sl = slice(head * ML_DIM, (head + 1) * ML_DIM)
        _mlstm_chunk(qf_ref[:, sl], ktf_ref[sl, :], vf_ref[:, sl], gf_ref[...], gtf_ref[...],
                     ctf_ref.at[head], mf_ref.at[head], hf_ref.at[:, sl], head, False)
        _mlstm_chunk(qb_ref[:, sl], ktb_ref[sl, :], vb_ref[:, sl], gb_ref[...], gtb_ref[...],
                     ctb_ref.at[head], mb_ref.at[head], hb_ref.at[:, sl], head, True)


def _mlstm(z, kt, gates, gates_t, batch, seq):
    n = z.shape[0]
    L = L_MLSTM
    nc = seq // L
    width = ML_HEADS * ML_DIM
    cq = Z_ML_Q // width
    cv = Z_ML_V // width

    def specs(tix):
        return [
            pl.BlockSpec((L, width), lambda b, t: (b * nc + tix(t), cq)),
            pl.BlockSpec((width, L), lambda b, t: (0, b * nc + tix(t))),
            pl.BlockSpec((L, width), lambda b, t: (b * nc + tix(t), cv)),
            pl.BlockSpec((L, LANES), lambda b, t: (b * nc + tix(t), 0)),
            pl.BlockSpec((N_GATES, L), lambda b, t: (0, b * nc + tix(t))),
        ]

    fwd = lambda t: t
    bwd = lambda t: nc - 1 - t
    out = jax.ShapeDtypeStruct((n, width), BF16)
    state = [pltpu.VMEM((ML_HEADS, ML_DIM, ML_DIM + LANES), F32), pltpu.VMEM((ML_HEADS, 1, 1), F32)]
    return pl.pallas_call(
        _mlstm_kernel,
        grid=(batch, nc),
        in_specs=specs(fwd) + specs(bwd),
        out_specs=[pl.BlockSpec((L, width), lambda b, t: (b * nc + fwd(t), 0)),
                   pl.BlockSpec((L, width), lambda b, t: (b * nc + bwd(t), 0))],
        out_shape=[out, out],
        scratch_shapes=state + state,
        compiler_params=_cparams(("parallel", "arbitrary")),
        name="mlstm",
    )(z, kt, z, gates, gates_t, z, kt, z, gates, gates_t)


def _memkv_kernel(m_ref, g_ref, w_ref, o_ref):
    mn = _rms(m_ref[...], g_ref[...]).astype(BF16)
    o_ref[...] = jnp.dot(mn, w_ref[...], preferred_element_type=F32).astype(BF16)


def _memkv(mem2d, g_mem, w_xkv, mem_tokens):
    n = mem2d.shape[0]
    return pl.pallas_call(
        _memkv_kernel,
        grid=(n // mem_tokens,),
        in_specs=[pl.BlockSpec((mem_tokens, D_MODEL), lambda b: (b, 0)),
                  pl.BlockSpec((1, D_MODEL), lambda b: (0, 0)),
                  pl.BlockSpec((D_MODEL, 2 * D_MODEL), lambda b: (0, 0))],
        out_specs=pl.BlockSpec((mem_tokens, 2 * D_MODEL), lambda b: (b, 0)),
        out_shape=jax.ShapeDtypeStruct((n, 2 * D_MODEL), BF16),
        compiler_params=_cparams(("parallel",)),
        name="mem_kv",
    )(mem2d, g_mem, w_xkv)


def _sigmoid(x):
    return 1.0 / (1.0 + jnp.exp(-x))


def _mix_kernel(x_ref, oa_ref, hf_ref, hb_ref, mo_ref, bga_ref, bgb_ref, kv_ref,
                gml_ref, gxa_ref, gffn_ref, wpa_ref, wpb_ref, wo_ref, wxq_ref, wxo_ref, wrt_ref,
                x2_ref, hn_ref, afft_ref):
    hsum = hf_ref[...].astype(F32) + hb_ref[...].astype(F32)
    gml = gml_ref[...]
    parts = []
    for hh in range(ML_HEADS):
        sl = slice(hh * ML_DIM, (hh + 1) * ML_DIM)
        parts.append(_rms(hsum[:, sl], gml[:, sl]))
    hb = jnp.concatenate(parts, axis=1)
    hb = hb * _sigmoid(mo_ref[...].astype(F32))
    y_a = jnp.dot(oa_ref[...], wpa_ref[...], preferred_element_type=F32)
    y_b = jnp.dot(hb.astype(BF16), wpb_ref[...], preferred_element_type=F32)
    merged = (_sigmoid(bga_ref[...].astype(F32)) * y_a
              + _sigmoid(bgb_ref[...].astype(F32)) * y_b)
    x1 = x_ref[...] + jnp.dot(merged.astype(BF16), wo_ref[...], preferred_element_type=F32)

    hq = _rms(x1, gxa_ref[...]).astype(BF16)
    qx = (jnp.dot(hq, wxq_ref[...], preferred_element_type=F32) * (XA_DIM ** -0.5)).astype(BF16)
    kv = kv_ref[...]
    outs = []
    for hh in range(XA_HEADS):
        sl = slice(hh * XA_DIM, (hh + 1) * XA_DIM)
        kh = kv[:, sl]
        vh = kv[:, D_MODEL + hh * XA_DIM:D_MODEL + (hh + 1) * XA_DIM]
        s = lax.dot_general(qx[:, sl], kh, (((1,), (1,)), ((), ())), preferred_element_type=F32)
        p = jnp.exp(s - jnp.max(s, axis=-1, keepdims=True))
        p = p / jnp.sum(p, axis=-1, keepdims=True)
        outs.append(jnp.dot(p.astype(BF16), vh, preferred_element_type=F32).astype(BF16))
    o = jnp.concatenate(outs, axis=1)
    x2 = x1 + jnp.dot(o, wxo_ref[...], preferred_element_type=F32)
    x2_ref[...] = x2

    hn = _rms(x2, gffn_ref[...])
    hn_ref[...] = hn.astype(BF16)
    logits_t = lax.dot_general(wrt_ref[...], hn, (((1,), (1,)), ((), ())),
                               precision=lax.Precision.HIGHEST, preferred_element_type=F32)
    e = jnp.exp(logits_t - jnp.max(logits_t, axis=0, keepdims=True))
    afft_ref[...] = e / jnp.sum(e, axis=0, keepdims=True)


def _mix(x2d, oa, h_fw, h_bw, z, kv, p, batch, seq, mem_tokens):
    n = x2d.shape[0]
    tm = TM_MIX
    nt = seq // tm
    cmo = Z_ML_O // D_MODEL
    cbg = Z_BR_G // D_MODEL
    full = lambda r, c: pl.BlockSpec((r, c), lambda i: (0, 0), pipeline_mode=pl.Buffered(1))
    return pl.pallas_call(
        _mix_kernel,
        grid=(n // tm,),
        in_specs=[
            pl.BlockSpec((tm, D_MODEL), lambda i: (i, 0)),
            pl.BlockSpec((tm, D_MODEL), lambda i: (i, 0)),
            pl.BlockSpec((tm, D_MODEL), lambda i: (i, 0)),
            pl.BlockSpec((tm, D_MODEL), lambda i: (i, 0)),
            pl.BlockSpec((tm, D_MODEL), lambda i: (i, cmo)),
            pl.BlockSpec((tm, D_MODEL), lambda i: (i, cbg)),
            pl.BlockSpec((tm, D_MODEL), lambda i: (i, cbg + 1)),
            pl.BlockSpec((mem_tokens, 2 * D_MODEL), lambda i: (i // nt, 0)),
            full(1, D_MODEL), full(1, D_MODEL), full(1, D_MODEL),
            full(D_MODEL, D_MODEL), full(D_MODEL, D_MODEL), full(D_MODEL, D_MODEL),
            full(D_MODEL, D_MODEL), full(D_MODEL, D_MODEL),
            full(N_EXPERTS, D_MODEL),
        ],
        out_specs=[pl.BlockSpec((tm, D_MODEL), lambda i: (i, 0)),
                   pl.BlockSpec((tm, D_MODEL), lambda i: (i, 0)),
                   pl.BlockSpec((N_EXPERTS, tm), lambda i: (0, i))],
        out_shape=[jax.ShapeDtypeStruct((n, D_MODEL), F32),
                   jax.ShapeDtypeStruct((n, D_MODEL), BF16),
                   jax.ShapeDtypeStruct((N_EXPERTS, n), F32)],
        compiler_params=_cparams(("parallel",)),
        name="mix_xattn_router",
    )(x2d, oa, h_fw, h_bw, z, z, z, kv,
      p["g_ml_out"], p["g_xa"], p["g_ffn"], p["w_proj_a"], p["w_proj_b"], p["w_out"],
      p["w_xq"], p["w_xo"], p["w_router_t"])


def _prefix_incl(mask_f, upper, ones, lower):
    mb = mask_f.astype(BF16)
    lane_incl = jnp.dot(mb, upper, preferred_element_type=F32)
    row_tot = jnp.dot(mb, ones, preferred_element_type=F32)
    row_off = jnp.dot(lower, row_tot.astype(BF16), preferred_element_type=F32)
    return lane_incl + row_off, row_off


def _select_kernel(a_ref, pos_ref, off_ref, *, cap):
    a = a_ref[...]
    rows = a.shape[0]
    bits = pltpu.bitcast(a, I32)

    def body(i, thr):
        cand = thr | jnp.left_shift(jnp.int32(1), 30 - i)
        cnt = jnp.sum(jnp.where(bits >= cand, 1.0, 0.0), keepdims=True)
        return jnp.where(cnt >= cap, cand, thr)

    thr = lax.fori_loop(0, 31, body, jnp.zeros((1, 1), I32))
    gt = bits > thr
    eq = bits == thr
    n_gt = jnp.sum(jnp.where(gt, 1.0, 0.0), keepdims=True)
    need = cap - n_gt

    li = lax.broadcasted_iota(I32, (LANES, LANES), 0)
    lj = lax.broadcasted_iota(I32, (LANES, LANES), 1)
    upper = (li <= lj).astype(BF16)
    ones = jnp.ones((LANES, LANES), BF16)
    ri = lax.broadcasted_iota(I32, (rows, rows), 0)
    rj = lax.broadcasted_iota(I32, (rows, rows), 1)
    lower = (rj < ri).astype(BF16)

    eq_rank, _ = _prefix_incl(jnp.where(eq, 1.0, 0.0), upper, ones, lower)
    sel = gt | (eq & (eq_rank <= need))
    incl, row_off = _prefix_incl(jnp.where(sel, 1.0, 0.0), upper, ones, lower)
    pos_ref[...] = jnp.where(sel, incl - 1.0, -1.0).astype(I32)
    off_ref[...] = row_off.astype(I32)


def _select(aff_t, n_tokens, cap):
    rows = n_tokens // LANES
    a3 = aff_t.reshape(N_EXPERTS, rows, LANES)
    spec = pl.BlockSpec((None, rows, LANES), lambda e: (e, 0, 0))
    return pl.pallas_call(
        functools.partial(_select_kernel, cap=float(cap)),
        grid=(N_EXPERTS,),
        in_specs=[spec],
        out_specs=[spec, spec],
        out_shape=[jax.ShapeDtypeStruct((N_EXPERTS, rows, LANES), I32),
                   jax.ShapeDtypeStruct((N_EXPERTS, rows, LANES), I32)],
        compiler_params=_cparams(("parallel",)),
        name="expert_choice_select",
    )(a3)


def _dispatch_kernel(s_ref, x_ref, pos_ref, xe_ref, buf_ref, sem_ref, *, n_blocks):
    t = pl.program_id(0)
    shift = B_SLOT.bit_length() - 1
    x = x_ref[...]
    tt = x.shape[0]
    slot = lax.broadcasted_iota(I32, (B_SLOT, tt), 0)
    row = lax.broadcasted_iota(I32, (B_SLOT, 1), 0)

    def block_copy(e, par, blk):
        first = blk * B_SLOT
        if not isinstance(first, int):
            first = pl.multiple_of(first, B_SLOT)
        return pltpu.make_async_copy(buf_ref.at[e, par], xe_ref.at[e, pl.ds(first, B_SLOT)],
                                     sem_ref.at[e, par])

    def gather_rows(e, blk):
        rel = pos_ref[e:e + 1, :] - blk * B_SLOT
        onehot = jnp.where(slot == rel, 1.0, 0.0).astype(BF16)
        return jnp.dot(onehot, x, preferred_element_type=F32).astype(BF16)

    for e in range(N_EXPERTS):
        s0 = s_ref[t * N_EXPERTS + e]
        s1 = s_ref[(t + 1) * N_EXPERTS + e]
        b0 = lax.shift_right_logical(s0, shift)
        r0 = s0 - b0 * B_SLOT
        par = b0 & 1

        @pl.when(s1 > s0)
        def _(e=e, s1=s1, b0=b0, r0=r0, par=par):
            @pl.when(r0 == 0)
            def _():
                @pl.when(b0 >= 2)
                def _():
                    block_copy(e, par, b0 - 2).wait()
                buf_ref[e, par] = gather_rows(e, b0)

            @pl.when(r0 != 0)
            def _():
                mine = (row >= r0) & (row < s1 - b0 * B_SLOT)
                buf_ref[e, par] = jnp.where(mine, gather_rows(e, b0), buf_ref[e, par])

            @pl.when(s1 > (b0 + 1) * B_SLOT)
            def _():
                @pl.when(b0 >= 1)
                def _():
                    block_copy(e, 1 - par, b0 - 1).wait()
                buf_ref[e, 1 - par] = gather_rows(e, b0 + 1)

            @pl.when(s1 >= (b0 + 1) * B_SLOT)
            def _():
                block_copy(e, par, b0).start()

    @pl.when(t == pl.num_programs(0) - 1)
    def _():
        for e in range(N_EXPERTS):
            for blk in range(max(n_blocks - 2, 0), n_blocks):
                block_copy(e, blk & 1, blk).wait()


def _dispatch(hn, pos_t, slot_starts, cap):
    n = hn.shape[0]
    tt = T_DISP
    n_blocks = cap // B_SLOT
    grid_spec = pltpu.PrefetchScalarGridSpec(
        num_scalar_prefetch=1,
        grid=(n // tt,),
        in_specs=[pl.BlockSpec((tt, D_MODEL), lambda t, s: (t, 0)),
                  pl.BlockSpec((N_EXPERTS, tt), lambda t, s: (0, t))],
        out_specs=pl.BlockSpec(memory_space=pl.ANY),
        scratch_shapes=[pltpu.VMEM((N_EXPERTS, 2, B_SLOT, D_MODEL), BF16),
                        pltpu.SemaphoreType.DMA((N_EXPERTS, 2))],
    )
    return pl.pallas_call(
        functools.partial(_dispatch_kernel, n_blocks=n_blocks),
        grid_spec=grid_spec,
        out_shape=jax.ShapeDtypeStruct((N_EXPERTS, cap, D_MODEL), BF16),
        compiler_params=_cparams(("arbitrary",)),
        name="dispatch",
    )(slot_starts, hn, pos_t)


def _slot_starts(row_off, tile, cap):
    start = row_off[:, ::tile // LANES, 0]
    start = jnp.concatenate([start, jnp.full((N_EXPERTS, 1), cap, I32)], axis=1)
    return start.T.reshape(-1).astype(I32)


def _ffn_kernel(x_ref, wg_ref, wu_ref, wd_ref, o_ref):
    x = x_ref[...]
    acc = jnp.zeros((x.shape[0], D_MODEL), F32)
    for c in range(D_EXPERT // F_CHUNK):
        sl = slice(c * F_CHUNK, (c + 1) * F_CHUNK)
        gte = jnp.dot(x, wg_ref[:, sl], preferred_element_type=F32)
        up = jnp.dot(x, wu_ref[:, sl], preferred_element_type=F32)
        hid = gte * (1.0 / (1.0 + jnp.exp(-gte))) * up
        acc = acc + jnp.dot(hid.astype(BF16), wd_ref[sl, :], preferred_element_type=F32)
    o_ref[...] = acc.astype(BF16)


def _ffn(xe, w_gate, w_up, w_down, cap):
    tm = min(TM_FFN, cap)
    return pl.pallas_call(
        _ffn_kernel,
        grid=(N_EXPERTS, cap // tm),
        in_specs=[pl.BlockSpec((None, tm, D_MODEL), lambda e, i: (e, i, 0)),
                  pl.BlockSpec((None, D_MODEL, D_EXPERT), lambda e, i: (e, 0, 0)),
                  pl.BlockSpec((None, D_MODEL, D_EXPERT), lambda e, i: (e, 0, 0)),
                  pl.BlockSpec((None, D_EXPERT, D_MODEL), lambda e, i: (e, 0, 0))],
        out_specs=pl.BlockSpec((None, tm, D_MODEL), lambda e, i: (e, i, 0)),
        out_shape=jax.ShapeDtypeStruct((N_EXPERTS, cap, D_MODEL), BF16),
        compiler_params=_cparams(("parallel", "arbitrary")),
        name="expert_ffn",
    )(xe, w_gate, w_up, w_down)


def _combine_kernel(b0_ref, st_ref, x_ref, pos_ref, aff_ref, g_ref, *rest):
    ye_refs = rest[:2 * N_EXPERTS]
    o_ref = rest[2 * N_EXPERTS]
    t = pl.program_id(0)
    tt = pos_ref.shape[0]
    slot = lax.broadcasted_iota(I32, (tt, B_SLOT), 1)

    def gated_onehot(e, kk):
        rel = pos_ref[:, e:e + 1] - (b0_ref[t * N_EXPERTS + e] + kk) * B_SLOT
        return jnp.where(slot == rel, aff_ref[:, e:e + 1], 0.0).astype(BF16)

    acc = x_ref[...]
    for e in range(N_EXPERTS):
        acc = acc + jnp.dot(gated_onehot(e, 0), ye_refs[2 * e][...], preferred_element_type=F32)
    o_ref[...] = acc
    for e in range(N_EXPERTS):
        @pl.when(st_ref[t * N_EXPERTS + e] != 0)
        def _(e=e):
            o_ref[...] += jnp.dot(gated_onehot(e, 1), ye_refs[2 * e + 1][...],
                                  preferred_element_type=F32)
    o_ref[...] = _rms(o_ref[...], g_ref[...])


def _combine(x2, pos_c, aff_c, ye, b0, straddle, g_final, cap):
    n = x2.shape[0]
    tt = T_COMB
    n_blocks = cap // B_SLOT

    def ye_spec(e, kk):
        return pl.BlockSpec(
            (None, B_SLOT, D_MODEL),
            lambda t, b0r, st: (e, jnp.minimum(b0r[t * N_EXPERTS + e] + kk, n_blocks - 1), 0))

    ye_specs = [ye_spec(e, kk) for e in range(N_EXPERTS) for kk in range(2)]
    grid_spec = pltpu.PrefetchScalarGridSpec(
        num_scalar_prefetch=2,
        grid=(n // tt,),
        in_specs=[pl.BlockSpec((tt, D_MODEL), lambda t, b0r, st: (t, 0)),
                  pl.BlockSpec((tt, N_EXPERTS), lambda t, b0r, st: (t, 0)),
                  pl.BlockSpec((tt, N_EXPERTS), lambda t, b0r, st: (t, 0)),
                  pl.BlockSpec((1, D_MODEL), lambda t, b0r, st: (0, 0))] + ye_specs,
        out_specs=pl.BlockSpec((tt, D_MODEL), lambda t, b0r, st: (t, 0)),
    )
    return pl.pallas_call(
        _combine_kernel,
        grid_spec=grid_spec,
        out_shape=jax.ShapeDtypeStruct((n, D_MODEL), F32),
        compiler_params=_cparams(("arbitrary",)),
        name="combine_final_norm",
    )(b0, straddle, x2, pos_c, aff_c, g_final, *([ye] * (2 * N_EXPERTS)))


def _encoder(x, mem, p):
    batch, seq, _ = x.shape
    mem_tokens = mem.shape[1]
    n = batch * seq
    cap = CAPACITY_FACTOR * n // N_EXPERTS
    x2d = x.reshape(n, D_MODEL)

    half = DA_DIM // 2
    pos = jnp.arange(seq, dtype=F32)
    inv = ROPE_THETA ** (-jnp.arange(half, dtype=F32) / half)
    ang = pos[:, None] * inv[None, :]
    cos = jnp.cos(ang)
    sin = jnp.sin(ang)
    cos_t = jnp.tile(cos, (1, LANES // half))
    sin_t = jnp.tile(jnp.concatenate([-sin, sin], axis=1), (1, LANES // DA_DIM))

    z, hn = _inproj(x2d, p["g_mix"], p["w_main"], p["colscale"], cos_t, sin_t, seq)
    kt = _ktrans(hn, p["w_kt"])
    gates, gates_t = _gates(hn, p["w_g"], p["w_gt"], p["b_g"], p["b_gt"])

    oa = _diffattn(z, p["lamv"], p["g_da_out"], batch, seq, p["lam_init"])
    h_fw, h_bw = _mlstm(z, kt, gates, gates_t, batch, seq)

    kv = _memkv(mem.reshape(batch * mem_tokens, D_MODEL), p["g_mem"], p["w_xkv"], mem_tokens)
    x2, hffn, aff_t = _mix(x2d, oa, h_fw, h_bw, z, kv, p, batch, seq, mem_tokens)

    pos3, off3 = _select(aff_t, n, cap)
    pos_t = pos3.reshape(N_EXPERTS, n)
    xe = _dispatch(hffn, pos_t, _slot_starts(off3, T_DISP, cap), cap)
    ye = _ffn(xe, p["w_e_gate"], p["w_e_up"], p["w_e_down"], cap)

    n_blocks = cap // B_SLOT
    start = off3[:, ::T_COMB // LANES, 0]
    end = jnp.concatenate([start[:, 1:], jnp.full((N_EXPERTS, 1), cap, I32)], axis=1)
    b0 = jnp.minimum(start // B_SLOT, n_blocks - 1)
    straddle = (end > start) & ((end - 1) // B_SLOT > b0)
    y = _combine(x2, pos_t.T, aff_t.T, ye, b0.T.reshape(-1).astype(I32),
                 straddle.T.reshape(-1).astype(I32), p["g_final"], cap)
    return y.reshape(batch, seq, D_MODEL)


def kernel(x_prompt, x_sample, mem_prompt, mem_sample, g_mix, w_in, lam_q1, lam_k1, lam_q2, lam_k2,
           g_da_out, b_ml_gates, g_ml_out, w_proj_a, w_proj_b, w_out, g_xa, g_mem, w_xq, w_xkv, w_xo,
           g_ffn, w_router, w_e_gate, w_e_up, w_e_down, g_final):
    l = 0
    w = w_in[l]
    assert w_in.shape[0] == 1, "single-layer encoder"
    gate_lo = W_GATES
    gate_hi = gate_lo + N_GATES
    colscale = jnp.ones((1, Z_WIDTH), F32)
    colscale = colscale.at[:, Z_DA_Q:Z_DA_K].set(DA_DIM ** -0.5 * math.log2(math.e))
    w_g = w[:, gate_lo:gate_hi]
    b = b_ml_gates[l].astype(F32)
    p = {
        "g_mix": g_mix[l][None, :],
        "w_main": jnp.concatenate([w[:, :W_ML_K], w[:, W_ML_V:gate_lo], w[:, gate_hi:]],
                                  axis=1).astype(BF16),
        "colscale": colscale,
        "w_kt": w[:, W_ML_K:W_ML_V].T.astype(BF16),
        "w_g": jnp.pad(w_g, ((0, 0), (0, LANES - N_GATES))).astype(BF16),
        "w_gt": w_g.T.astype(BF16),
        "b_g": jnp.pad(b, (0, LANES - N_GATES))[None, :],
        "b_gt": b[:, None],
        "lamv": jnp.stack([lam_q1[l], lam_k1[l], lam_q2[l], lam_k2[l]]).astype(F32),
        "lam_init": 0.8 - 0.6 * math.exp(-0.3 * l),
        "g_da_out": g_da_out[l][None, :],
        "g_ml_out": g_ml_out[l][None, :],
        "w_proj_a": w_proj_a[l].astype(BF16),
        "w_proj_b": w_proj_b[l].astype(BF16),
        "w_out": w_out[l].astype(BF16),
        "g_xa": g_xa[l][None, :],
        "g_mem": g_mem[l][None, :],
        "w_xq": w_xq[l].astype(BF16),
        "w_xkv": w_xkv[l].astype(BF16),
        "w_xo": w_xo[l].astype(BF16),
        "g_ffn": g_ffn[l][None, :],
        "w_router_t": w_router[l].T.astype(F32),
        "w_e_gate": w_e_gate[l].astype(BF16),
        "w_e_up": w_e_up[l].astype(BF16),
        "w_e_down": w_e_down[l].astype(BF16),
        "g_final": g_final[None, :],
    }
    y_prompt = _encoder(x_prompt, mem_prompt, p)
    y_sample = _encoder(x_sample, mem_sample, p)
    return (y_prompt, y_sample)
```

```python
import functools
import math

import jax
import jax.numpy as jnp
from jax import lax
from jax.experimental import pallas as pl
from jax.experimental.pallas import tpu as pltpu

F32 = jnp.float32
BF16 = jnp.bfloat16
I32 = jnp.int32

D_MODEL = 1024
DA_HEADS = 8
DA_DIM = 64
DA_HEAD_W = 2 * DA_DIM
ML_HEADS = 4
ML_DIM = 256
XA_HEADS = 4
XA_DIM = D_MODEL // XA_HEADS
N_EXPERTS = 16
CAPACITY_FACTOR = 2
D_EXPERT = 2048
EPS = 1e-6
ROPE_THETA = 10000.0
N_GATES = 4 * ML_HEADS

QK_Q = 0
QK_K = 1024
QK_WIDTH = 2048
Z_DA_V = 0
Z_ML_Q = 1024
Z_ML_V = 2048
Z_ML_O = 3072
Z_BR_G = 4096
Z_WIDTH = 6144
W_DA_V = 2048
W_ML_K = 4096
W_ML_V = 5120
W_GATES = 7168

LANES = 128
VMEM_LIMIT = 56 * 1024 * 1024

TM_PROJ = 1024
TN_PROJ = 512
TQ_ATT = 2048
TK_ATT = 512
ATT_UNROLL = 8
L_MLSTM = 256
TM_MIX = 512
T_DISP = 256
B_SLOT = 256
TM_FFN = 512
F_CHUNK = 512
T_COMB = 256


def _cparams(sem):
    return pltpu.CompilerParams(dimension_semantics=sem, vmem_limit_bytes=VMEM_LIMIT)


def _rms(xf, g):
    return xf * lax.rsqrt(jnp.mean(xf * xf, axis=-1, keepdims=True) + EPS) * g


def _qk_proj_kernel(x_ref, g_ref, w_ref, cos_ref, sin_ref, z_ref, hn_ref, hs_ref, *, n_q_tiles):
    j = pl.program_id(1)

    @pl.when(j == 0)
    def _():
        hb = _rms(x_ref[...], g_ref[...]).astype(BF16)
        hs_ref[...] = hb
        hn_ref[...] = hb

    scale = jnp.where(j < n_q_tiles, DA_DIM ** -0.5 * math.log2(math.e), 1.0)
    acc = jnp.dot(hs_ref[...], w_ref[...], preferred_element_type=F32) * scale
    cos = cos_ref[...]
    sin = sin_ref[...]
    for hh in range(acc.shape[1] // LANES):
        a = acc[:, hh * LANES:(hh + 1) * LANES]
        z_ref[:, hh * LANES:(hh + 1) * LANES] = (a * cos + pltpu.roll(a, LANES // 2, 1) * sin).astype(BF16)


def _qk_proj(x2d, g_mix, w_qk, cos_t, sin_t, seq):
    n = x2d.shape[0]
    tm, tn = TM_PROJ, TN_PROJ
    n_seq_tiles = seq // tm
    return pl.pallas_call(
        functools.partial(_qk_proj_kernel, n_q_tiles=D_MODEL // tn),
        grid=(n // tm, QK_WIDTH // tn),
        in_specs=[
            pl.BlockSpec((tm, D_MODEL), lambda i, j: (i, 0)),
            pl.BlockSpec((1, D_MODEL), lambda i, j: (0, 0)),
            pl.BlockSpec((D_MODEL, tn), lambda i, j: (0, j)),
            pl.BlockSpec((tm, LANES), lambda i, j: (i % n_seq_tiles, 0)),
            pl.BlockSpec((tm, LANES), lambda i, j: (i % n_seq_tiles, 0)),
        ],
        out_specs=[
            pl.BlockSpec((tm, tn), lambda i, j: (i, j)),
            pl.BlockSpec((tm, D_MODEL), lambda i, j: (i, 0)),
        ],
        out_shape=[jax.ShapeDtypeStruct((n, QK_WIDTH), BF16),
                   jax.ShapeDtypeStruct((n, D_MODEL), BF16)],
        scratch_shapes=[pltpu.VMEM((tm, D_MODEL), BF16)],
        compiler_params=_cparams(("parallel", "arbitrary")),
        name="inproj_qk",
    )(x2d, g_mix, w_qk, cos_t, sin_t)


def _proj_kernel(h_ref, w_ref, z_ref):
    z_ref[...] = jnp.dot(h_ref[...], w_ref[...], preferred_element_type=F32).astype(BF16)


def _proj(hn, w_rest):
    n = hn.shape[0]
    tm, tn = TM_PROJ, TN_PROJ
    return pl.pallas_call(
        _proj_kernel,
        grid=(n // tm, Z_WIDTH // tn),
        in_specs=[pl.BlockSpec((tm, D_MODEL), lambda i, j: (i, 0)),
                  pl.BlockSpec((D_MODEL, tn), lambda i, j: (0, j))],
        out_specs=pl.BlockSpec((tm, tn), lambda i, j: (i, j)),
        out_shape=jax.ShapeDtypeStruct((n, Z_WIDTH), BF16),
        compiler_params=_cparams(("parallel", "arbitrary")),
        name="inproj_rest",
    )(hn, w_rest)


def _ktrans_kernel(w_ref, h_ref, o_ref):
    kt = lax.dot_general(w_ref[...], h_ref[...], (((1,), (1,)), ((), ())),
                         preferred_element_type=F32)
    o_ref[...] = (kt * (ML_DIM ** -0.5)).astype(BF16)


def _ktrans(hn, w_kt):
    n = hn.shape[0]
    tm = TM_PROJ
    return pl.pallas_call(
        _ktrans_kernel,
        grid=(n // tm, ML_HEADS),
        in_specs=[pl.BlockSpec((ML_DIM, D_MODEL), lambda i, h: (h, 0)),
                  pl.BlockSpec((tm, D_MODEL), lambda i, h: (i, 0))],
        out_specs=pl.BlockSpec((ML_DIM, tm), lambda i, h: (h, i)),
        out_shape=jax.ShapeDtypeStruct((ML_HEADS * ML_DIM, n), BF16),
        compiler_params=_cparams(("parallel", "arbitrary")),
        name="ml_k_transposed",
    )(w_kt, hn)


def _log_sigmoid(x):
    return jnp.minimum(x, 0.0) - jnp.log1p(jnp.exp(-jnp.abs(x)))


def _split3(x):
    hi = x.astype(BF16)
    r = x - hi.astype(F32)
    mid = r.astype(BF16)
    lo = (r - mid.astype(F32)).astype(BF16)
    return hi, mid, lo


def _gates_kernel(h_ref, w_ref, wt_ref, b_ref, bt_ref, g_ref, gt_ref):
    h = h_ref[...]
    g = jnp.dot(h, w_ref[...], preferred_element_type=F32) + b_ref[...]
    gt = lax.dot_general(wt_ref[...], h, (((1,), (1,)), ((), ())),
                         preferred_element_type=F32) + bt_ref[...]
    L = L_MLSTM
    n_in = 2 * ML_HEADS
    n_fw = 3 * ML_HEADS
    lane = lax.broadcasted_iota(I32, (L, LANES), 1)
    sub = lax.broadcasted_iota(I32, (N_GATES, L), 0)
    ri = lax.broadcasted_iota(I32, (L, L), 0)
    ci = lax.broadcasted_iota(I32, (L, L), 1)
    lower = jnp.where(ci <= ri, 1.0, 0.0).astype(BF16)
    upper = jnp.where(ci >= ri, 1.0, 0.0).astype(BF16)
    nt = (((1,), (1,)), ((), ()))
    for c in range(g.shape[0] // L):
        gc = g[c * L:(c + 1) * L, :]
        pre = jnp.zeros((L, LANES), F32)
        suf = jnp.zeros((L, LANES), F32)
        for piece in _split3(jnp.where(lane >= n_in, _log_sigmoid(gc), 0.0)):
            pre = pre + jnp.dot(lower, piece, preferred_element_type=F32)
            suf = suf + jnp.dot(upper, piece, preferred_element_type=F32)
        g_ref[c * L:(c + 1) * L, :] = jnp.where(lane < n_in, gc, jnp.where(lane < n_fw, pre, suf))

        gtc = gt[:, c * L:(c + 1) * L]
        pre = jnp.zeros((N_GATES, L), F32)
        suf = jnp.zeros((N_GATES, L), F32)
        for piece in _split3(jnp.where(sub >= n_in, _log_sigmoid(gtc), 0.0)):
            pre = pre + lax.dot_general(piece, lower, nt, preferred_element_type=F32)
            suf = suf + lax.dot_general(piece, upper, nt, preferred_element_type=F32)
        gt_ref[:, c * L:(c + 1) * L] = jnp.where(sub < n_in, gtc, jnp.where(sub < n_fw, pre, suf))


def _gates(hn, w_g, w_gt, b_g, b_gt):
    n = hn.shape[0]
    tm = TM_PROJ
    return pl.pallas_call(
        _gates_kernel,
        grid=(n // tm,),
        in_specs=[pl.BlockSpec((tm, D_MODEL), lambda i: (i, 0)),
                  pl.BlockSpec((D_MODEL, LANES), lambda i: (0, 0)),
                  pl.BlockSpec((N_GATES, D_MODEL), lambda i: (0, 0)),
                  pl.BlockSpec((1, LANES), lambda i: (0, 0)),
                  pl.BlockSpec((N_GATES, 1), lambda i: (0, 0))],
        out_specs=[pl.BlockSpec((tm, LANES), lambda i: (i, 0)),
                   pl.BlockSpec((N_GATES, tm), lambda i: (0, i))],
        out_shape=[jax.ShapeDtypeStruct((n, LANES), F32),
                   jax.ShapeDtypeStruct((N_GATES, n), F32)],
        compiler_params=_cparams(("parallel",)),
        name="ml_gates",
    )(hn, w_g, w_gt, b_g, b_gt)


def _diffattn_kernel(q_ref, k_ref, v_ref, lamv_ref, g_ref, o_ref, va_ref, *, seq, lam_init):
    tq = q_ref.shape[0]
    tk = min(TK_ATT, seq)
    n_chunks = seq // tk
    unroll = min(n_chunks, ATT_UNROLL)

    @pl.when(pl.program_id(2) == 0)
    def _():
        va_ref[:, :DA_HEAD_W] = v_ref[...]
        va_ref[:, DA_HEAD_W:] = jnp.ones((seq, DA_HEAD_W), BF16)

    q = q_ref[...]
    lane = lax.broadcasted_iota(I32, q.shape, 1)
    zero = jnp.zeros_like(q)
    in_map0 = ((lane // (DA_DIM // 2)) % 2) == 0
    qs = (jnp.where(in_map0, q, zero), jnp.where(in_map0, zero, q))

    def chunk(start, carry):
        kc = k_ref[pl.ds(start, tk), :]
        vc = va_ref[pl.ds(start, tk), :]
        out = []
        for qm, (m, l, a) in zip(qs, carry):
            s = lax.dot_general(qm, kc, (((1,), (1,)), ((), ())), preferred_element_type=F32)
            mn = jnp.maximum(m, jnp.max(s, axis=-1, keepdims=True))
            alpha = jnp.exp2(m - mn)
            p = jnp.exp2((s - mn).astype(BF16))
            pv = jnp.dot(p, vc, preferred_element_type=F32)
            out.append((mn, alpha * l + pv[:, DA_HEAD_W:],
                        alpha * a + pv[:, :DA_HEAD_W]))
        return tuple(out)

    def body(c, carry):
        for u in range(unroll):
            carry = chunk(pl.multiple_of((c * unroll + u) * tk, tk), carry)
        return carry

    init = (jnp.full((tq, 1), -jnp.inf, F32), jnp.zeros((tq, DA_HEAD_W), F32),
            jnp.zeros((tq, DA_HEAD_W), F32))
    (m0, l0, a0), (m1, l1, a1) = lax.fori_loop(0, n_chunks // unroll, body, (init, init))

    lv = lamv_ref[...]
    lam = (jnp.exp(jnp.sum(lv[0:1, :] * lv[1:2, :], axis=-1, keepdims=True))
           - jnp.exp(jnp.sum(lv[2:3, :] * lv[3:4, :], axis=-1, keepdims=True)) + lam_init)
    o = a0 * (1.0 / l0) - lam * (a1 * (1.0 / l1))
    o_ref[...] = (_rms(o, g_ref[...]) * (1.0 - lam_init)).astype(BF16)


def _diffattn(zqk, z, lamv, g_da, batch, seq, lam_init):
    n = z.shape[0]
    tq = min(TQ_ATT, seq)
    nq = seq // tq
    hq = QK_Q // DA_HEAD_W
    hk = QK_K // DA_HEAD_W
    hv = Z_DA_V // DA_HEAD_W
    return pl.pallas_call(
        functools.partial(_diffattn_kernel, seq=seq, lam_init=lam_init),
        grid=(batch, DA_HEADS, nq),
        in_specs=[
            pl.BlockSpec((tq, DA_HEAD_W), lambda b, h, i: (b * nq + i, hq + h)),
            pl.BlockSpec((seq, DA_HEAD_W), lambda b, h, i: (b, hk + h)),
            pl.BlockSpec((seq, DA_HEAD_W), lambda b, h, i: (b, hv + h)),
            pl.BlockSpec((4, DA_DIM), lambda b, h, i: (0, 0)),
            pl.BlockSpec((1, DA_HEAD_W), lambda b, h, i: (0, 0)),
        ],
        out_specs=pl.BlockSpec((tq, DA_HEAD_W), lambda b, h, i: (b * nq + i, h)),
        out_shape=jax.ShapeDtypeStruct((n, DA_HEADS * DA_HEAD_W), BF16),
        scratch_shapes=[pltpu.VMEM((seq, 2 * DA_HEAD_W), BF16)],
        compiler_params=_cparams(("parallel", "parallel", "arbitrary")),
        name="diff_attention",
    )(zqk, zqk, z, lamv, g_da)


def _mlstm_chunk(q, kt, v, g, gt, cta_ref, m_ref, h_ref, head, reverse):
    L = q.shape[0]
    col_i = head + (ML_HEADS if reverse else 0)
    col_f = col_i + 2 * ML_HEADS
    lane = lax.broadcasted_iota(I32, g.shape, 1)
    sub = lax.broadcasted_iota(I32, gt.shape, 0)
    i_col = jnp.sum(jnp.where(lane == col_i, g, 0.0), axis=1, keepdims=True)
    g_col = jnp.sum(jnp.where(lane == col_f, g, 0.0), axis=1, keepdims=True)
    i_row = jnp.sum(jnp.where(sub == col_i, gt, 0.0), axis=0, keepdims=True)
    g_row = jnp.sum(jnp.where(sub == col_f, gt, 0.0), axis=0, keepdims=True)

    ri = lax.broadcasted_iota(I32, (L, L), 0)
    ci = lax.broadcasted_iota(I32, (L, L), 1)
    tri = (ci >= ri) if reverse else (ci <= ri)
    last = 0 if reverse else L - 1
    lane_l = lax.broadcasted_iota(I32, (1, L), 1)
    g_last = jnp.sum(jnp.where(lane_l == last, g_row, 0.0), axis=1, keepdims=True)

    m_prev = m_ref[...]
    dmat = jnp.where(tri, g_col + (i_row - g_row), -jnp.inf)
    a_col = g_col + m_prev
    mj = jnp.maximum(a_col, jnp.max(dmat, axis=1, keepdims=True))
    w_inter = jnp.exp(a_col - mj)
    s = jnp.dot(q, kt, preferred_element_type=F32) * jnp.exp(dmat - mj)
    ones = jnp.ones((L, LANES), BF16)
    cta = cta_ref[...]
    num = (w_inter * jnp.dot(q, cta.astype(BF16), preferred_element_type=F32)
           + jnp.dot(s.astype(BF16), jnp.concatenate([v, ones], axis=1), preferred_element_type=F32))
    inv = 1.0 / jnp.maximum(jnp.abs(num[:, ML_DIM:]), jnp.exp(-mj))
    h_ref[...] = (num[:, :ML_DIM] * jnp.concatenate([inv] * (ML_DIM // LANES), axis=1)).astype(BF16)

    dec_row = g_last - g_row + i_row
    dec_col = g_last - g_col + i_col
    m_new = jnp.maximum(g_last + m_prev, jnp.max(dec_row, axis=1, keepdims=True))
    wk_col = jnp.exp(dec_col - m_new)
    carry_scale = jnp.exp(g_last + m_prev - m_new)
    wv = jnp.concatenate([(wk_col * v.astype(F32)).astype(BF16),
                          jnp.broadcast_to(wk_col, (L, LANES)).astype(BF16)], axis=1)
    cta_ref[...] = carry_scale * cta + jnp.dot(kt, wv, preferred_element_type=F32)
    m_ref[...] = m_new


def _mlstm_kernel(qf_ref, ktf_ref, vf_ref, gf_ref, gtf_ref, qb_ref, ktb_ref, vb_ref, gb_ref, gtb_ref,
                  hf_ref, hb_ref, ctf_ref, mf_ref, ctb_ref, mb_ref):
    @pl.when(pl.program_id(1) == 0)
    def _():
        ctf_ref[...] = jnp.zeros_like(ctf_ref)
        mf_ref[...] = jnp.zeros_like(mf_ref)
        ctb_ref[...] = jnp.zeros_like(ctb_ref)
        mb_ref[...] = jnp.zeros_like(mb_ref)

    for head in range(ML_HEADS):
        sl = slice(head * ML_DIM, (head + 1) * ML_DIM)
        _mlstm_chunk(qf_ref[:, sl], ktf_ref[sl, :], vf_ref[:, sl], gf_ref[...], gtf_ref[...],
                     ctf_ref.at[head], mf_ref.at[head], hf_ref.at[:, sl], head, False)
        _mlstm_chunk(qb_ref[:, sl], ktb_ref[sl, :], vb_ref[:, sl], gb_ref[...], gtb_ref[...],
                     ctb_ref.at[head], mb_ref.at[head], hb_ref.at[:, sl], head, True)


def _mlstm(z, kt, gates, gates_t, batch, seq):
    n = z.shape[0]
    L = L_MLSTM
    nc = seq // L
    width = ML_HEADS * ML_DIM
    cq = Z_ML_Q // width
    cv = Z_ML_V // width

    def specs(tix):
        return [
            pl.BlockSpec((L, width), lambda b, t: (b * nc + tix(t), cq)),
            pl.BlockSpec((width, L), lambda b, t: (0, b * nc + tix(t))),
            pl.BlockSpec((L, width), lambda b, t: (b * nc + tix(t), cv)),
            pl.BlockSpec((L, LANES), lambda b, t: (b * nc + tix(t), 0)),
            pl.BlockSpec((N_GATES, L), lambda b, t: (0, b * nc + tix(t))),
        ]

    fwd = lambda t: t
    bwd = lambda t: nc - 1 - t
    out = jax.ShapeDtypeStruct((n, width), BF16)
    state = [pltpu.VMEM((ML_HEADS, ML_DIM, ML_DIM + LANES), F32), pltpu.VMEM((ML_HEADS, 1, 1), F32)]
    return pl.pallas_call(
        _mlstm_kernel,
        grid=(batch, nc),
        in_specs=specs(fwd) + specs(bwd),
        out_specs=[pl.BlockSpec((L, width), lambda b, t: (b * nc + fwd(t), 0)),
                   pl.BlockSpec((L, width), lambda b, t: (b * nc + bwd(t), 0))],
        out_shape=[out, out],
        scratch_shapes=state + state,
        compiler_params=_cparams(("parallel", "arbitrary")),
        name="mlstm",
    )(z, kt, z, gates, gates_t, z, kt, z, gates, gates_t)


def _memkv_kernel(m_ref, g_ref, w_ref, o_ref):
    mn = _rms(m_ref[...], g_ref[...]).astype(BF16)
    o_ref[...] = jnp.dot(mn, w_ref[...], preferred_element_type=F32).astype(BF16)


def _memkv(mem2d, g_mem, w_xkv, mem_tokens):
    n = mem2d.shape[0]
    return pl.pallas_call(
        _memkv_kernel,
        grid=(n // mem_tokens,),
        in_specs=[pl.BlockSpec((mem_tokens, D_MODEL), lambda b: (b, 0)),
                  pl.BlockSpec((1, D_MODEL), lambda b: (0, 0)),
                  pl.BlockSpec((D_MODEL, 2 * D_MODEL), lambda b: (0, 0))],
        out_specs=pl.BlockSpec((mem_tokens, 2 * D_MODEL), lambda b: (b, 0)),
        out_shape=jax.ShapeDtypeStruct((n, 2 * D_MODEL), BF16),
        compiler_params=_cparams(("parallel",)),
        name="mem_kv",
    )(mem2d, g_mem, w_xkv)


def _sigmoid(x):
    return 1.0 / (1.0 + jnp.exp(-x))


def _mix_kernel(x_ref, oa_ref, hf_ref, hb_ref, mo_ref, bga_ref, bgb_ref, kv_ref,
                gml_ref, gxa_ref, gffn_ref, wpa_ref, wpb_ref, wo_ref, wxq_ref, wxo_ref, wrt_ref,
                x2_ref, hn_ref, afft_ref):
    hsum = hf_ref[...].astype(F32) + hb_ref[...].astype(F32)
    gml = gml_ref[...]
    parts = []
    for hh in range(ML_HEADS):
        sl = slice(hh * ML_DIM, (hh + 1) * ML_DIM)
        parts.append(_rms(hsum[:, sl], gml[:, sl]))
    hb = jnp.concatenate(parts, axis=1)
    hb = hb * _sigmoid(mo_ref[...].astype(F32))
    y_a = jnp.dot(oa_ref[...], wpa_ref[...], preferred_element_type=F32)
    y_b = jnp.dot(hb.astype(BF16), wpb_ref[...], preferred_element_type=F32)
    merged = (_sigmoid(bga_ref[...].astype(F32)) * y_a
              + _sigmoid(bgb_ref[...].astype(F32)) * y_b)
    x1 = x_ref[...] + jnp.dot(merged.astype(BF16), wo_ref[...], preferred_element_type=F32)

    hq = _rms(x1, gxa_ref[...]).astype(BF16)
    qx = (jnp.dot(hq, wxq_ref[...], preferred_element_type=F32) * (XA_DIM ** -0.5)).astype(BF16)
    kv = kv_ref[...]
    outs = []
    for hh in range(XA_HEADS):
        sl = slice(hh * XA_DIM, (hh + 1) * XA_DIM)
        kh = kv[:, sl]
        vh = kv[:, D_MODEL + hh * XA_DIM:D_MODEL + (hh + 1) * XA_DIM]
        s = lax.dot_general(qx[:, sl], kh, (((1,), (1,)), ((), ())), preferred_element_type=F32)
        p = jnp.exp(s - jnp.max(s, axis=-1, keepdims=True))
        p = p / jnp.sum(p, axis=-1, keepdims=True)
        outs.append(jnp.dot(p.astype(BF16), vh, preferred_element_type=F32).astype(BF16))
    o = jnp.concatenate(outs, axis=1)
    x2 = x1 + jnp.dot(o, wxo_ref[...], preferred_element_type=F32)
    x2_ref[...] = x2

    hn = _rms(x2, gffn_ref[...])
    hn_ref[...] = hn.astype(BF16)
    logits_t = lax.dot_general(wrt_ref[...], hn, (((1,), (1,)), ((), ())),
                               precision=lax.Precision.HIGHEST, preferred_element_type=F32)
    e = jnp.exp(logits_t - jnp.max(logits_t, axis=0, keepdims=True))
    afft_ref[...] = e / jnp.sum(e, axis=0, keepdims=True)


def _mix(x2d, oa, h_fw, h_bw, z, kv, p, batch, seq, mem_tokens):
    n = x2d.shape[0]
    tm = TM_MIX
    nt = seq // tm
    cmo = Z_ML_O // D_MODEL
    cbg = Z_BR_G // D_MODEL
    full = lambda r, c: pl.BlockSpec((r, c), lambda i: (0, 0), pipeline_mode=pl.Buffered(1))
    return pl.pallas_call(
        _mix_kernel,
        grid=(n // tm,),
        in_specs=[
            pl.BlockSpec((tm, D_MODEL), lambda i: (i, 0)),
            pl.BlockSpec((tm, D_MODEL), lambda i: (i, 0)),
            pl.BlockSpec((tm, D_MODEL), lambda i: (i, 0)),
            pl.BlockSpec((tm, D_MODEL), lambda i: (i, 0)),
            pl.BlockSpec((tm, D_MODEL), lambda i: (i, cmo)),
            pl.BlockSpec((tm, D_MODEL), lambda i: (i, cbg)),
            pl.BlockSpec((tm, D_MODEL), lambda i: (i, cbg + 1)),
            pl.BlockSpec((mem_tokens, 2 * D_MODEL), lambda i: (i // nt, 0)),
            full(1, D_MODEL), full(1, D_MODEL), full(1, D_MODEL),
            full(D_MODEL, D_MODEL), full(D_MODEL, D_MODEL), full(D_MODEL, D_MODEL),
            full(D_MODEL, D_MODEL), full(D_MODEL, D_MODEL),
            full(N_EXPERTS, D_MODEL),
        ],
        out_specs=[pl.BlockSpec((tm, D_MODEL), lambda i: (i, 0)),
                   pl.BlockSpec((tm, D_MODEL), lambda i: (i, 0)),
                   pl.BlockSpec((N_EXPERTS, tm), lambda i: (0, i))],
        out_shape=[jax.ShapeDtypeStruct((n, D_MODEL), F32),
                   jax.ShapeDtypeStruct((n, D_MODEL), BF16),
                   jax.ShapeDtypeStruct((N_EXPERTS, n), F32)],
        compiler_params=_cparams(("parallel",)),
        name="mix_xattn_router",
    )(x2d, oa, h_fw, h_bw, z, z, z, kv,
      p["g_ml_out"], p["g_xa"], p["g_ffn"], p["w_proj_a"], p["w_proj_b"], p["w_out"],
      p["w_xq"], p["w_xo"], p["w_router_t"])


def _prefix_incl(mask_f, upper, ones, lower):
    mb = mask_f.astype(BF16)
    lane_incl = jnp.dot(mb, upper, preferred_element_type=F32)
    row_tot = jnp.dot(mb, ones, preferred_element_type=F32)
    row_off = jnp.dot(lower, row_tot.astype(BF16), preferred_element_type=F32)
    return lane_incl + row_off, row_off


def _select_kernel(a_ref, pos_ref, off_ref, *, cap):
    a = a_ref[...]
    rows = a.shape[0]
    bits = pltpu.bitcast(a, I32)

    def body(i, thr):
        cand = thr | jnp.left_shift(jnp.int32(1), 30 - i)
        cnt = jnp.sum(jnp.where(bits >= cand, 1.0, 0.0), keepdims=True)
        return jnp.where(cnt >= cap, cand, thr)

    thr = lax.fori_loop(0, 31, body, jnp.zeros((1, 1), I32))
    gt = bits > thr
    eq = bits == thr
    n_gt = jnp.sum(jnp.where(gt, 1.0, 0.0), keepdims=True)
    need = cap - n_gt

    li = lax.broadcasted_iota(I32, (LANES, LANES), 0)
    lj = lax.broadcasted_iota(I32, (LANES, LANES), 1)
    upper = (li <= lj).astype(BF16)
    ones = jnp.ones((LANES, LANES), BF16)
    ri = lax.broadcasted_iota(I32, (rows, rows), 0)
    rj = lax.broadcasted_iota(I32, (rows, rows), 1)
    lower = (rj < ri).astype(BF16)

    eq_rank, _ = _prefix_incl(jnp.where(eq, 1.0, 0.0), upper, ones, lower)
    sel = gt | (eq & (eq_rank <= need))
    incl, row_off = _prefix_incl(jnp.where(sel, 1.0, 0.0), upper, ones, lower)
    pos_ref[...] = jnp.where(sel, incl - 1.0, -1.0).astype(I32)
    off_ref[...] = row_off.astype(I32)


def _select(aff_t, n_tokens, cap):
    rows = n_tokens // LANES
    a3 = aff_t.reshape(N_EXPERTS, rows, LANES)
    spec = pl.BlockSpec((None, rows, LANES), lambda e: (e, 0, 0))
    return pl.pallas_call(
        functools.partial(_select_kernel, cap=float(cap)),
        grid=(N_EXPERTS,),
        in_specs=[spec],
        out_specs=[spec, spec],
        out_shape=[jax.ShapeDtypeStruct((N_EXPERTS, rows, LANES), I32),
                   jax.ShapeDtypeStruct((N_EXPERTS, rows, LANES), I32)],
        compiler_params=_cparams(("parallel",)),
        name="expert_choice_select",
    )(a3)


def _dispatch_kernel(s_ref, x_ref, pos_ref, xe_ref, buf_ref, sem_ref, *, n_blocks):
    t = pl.program_id(0)
    shift = B_SLOT.bit_length() - 1
    x = x_ref[...]
    tt = x.shape[0]
    slot = lax.broadcasted_iota(I32, (B_SLOT, tt), 0)
    row = lax.broadcasted_iota(I32, (B_SLOT, 1), 0)

    def block_copy(e, par, blk):
        first = blk * B_SLOT
        if not isinstance(first, int):
            first = pl.multiple_of(first, B_SLOT)
        return pltpu.make_async_copy(buf_ref.at[e, par], xe_ref.at[e, pl.ds(first, B_SLOT)],
                                     sem_ref.at[e, par])

    def gather_rows(e, blk):
        rel = pos_ref[e:e + 1, :] - blk * B_SLOT
        onehot = jnp.where(slot == rel, 1.0, 0.0).astype(BF16)
        return jnp.dot(onehot, x, preferred_element_type=F32).astype(BF16)

    for e in range(N_EXPERTS):
        s0 = s_ref[t * N_EXPERTS + e]
        s1 = s_ref[(t + 1) * N_EXPERTS + e]
        b0 = lax.shift_right_logical(s0, shift)
        r0 = s0 - b0 * B_SLOT
        par = b0 & 1

        @pl.when(s1 > s0)
        def _(e=e, s1=s1, b0=b0, r0=r0, par=par):
            @pl.when(r0 == 0)
            def _():
                @pl.when(b0 >= 2)
                def _():
                    block_copy(e, par, b0 - 2).wait()
                buf_ref[e, par] = gather_rows(e, b0)

            @pl.when(r0 != 0)
            def _():
                mine = (row >= r0) & (row < s1 - b0 * B_SLOT)
                buf_ref[e, par] = jnp.where(mine, gather_rows(e, b0), buf_ref[e, par])

            @pl.when(s1 > (b0 + 1) * B_SLOT)
            def _():
                @pl.when(b0 >= 1)
                def _():
                    block_copy(e, 1 - par, b0 - 1).wait()
                buf_ref[e, 1 - par] = gather_rows(e, b0 + 1)

            @pl.when(s1 >= (b0 + 1) * B_SLOT)
            def _():
                block_copy(e, par, b0).start()

    @pl.when(t == pl.num_programs(0) - 1)
    def _():
        for e in range(N_EXPERTS):
            for blk in range(max(n_blocks - 2, 0), n_blocks):
                block_copy(e, blk & 1, blk).wait()


def _dispatch(hn, pos_t, slot_starts, cap):
    n = hn.shape[0]
    tt = T_DISP
    n_blocks = cap // B_SLOT
    grid_spec = pltpu.PrefetchScalarGridSpec(
        num_scalar_prefetch=1,
        grid=(n // tt,),
        in_specs=[pl.BlockSpec((tt, D_MODEL), lambda t, s: (t, 0)),
                  pl.BlockSpec((N_EXPERTS, tt), lambda t, s: (0, t))],
        out_specs=pl.BlockSpec(memory_space=pl.ANY),
        scratch_shapes=[pltpu.VMEM((N_EXPERTS, 2, B_SLOT, D_MODEL), BF16),
                        pltpu.SemaphoreType.DMA((N_EXPERTS, 2))],
    )
    return pl.pallas_call(
        functools.partial(_dispatch_kernel, n_blocks=n_blocks),
        grid_spec=grid_spec,
        out_shape=jax.ShapeDtypeStruct((N_EXPERTS, cap, D_MODEL), BF16),
        compiler_params=_cparams(("arbitrary",)),
        name="dispatch",
    )(slot_starts, hn, pos_t)


def _slot_starts(row_off, tile, cap):
    start = row_off[:, ::tile // LANES, 0]
    start = jnp.concatenate([start, jnp.full((N_EXPERTS, 1), cap, I32)], axis=1)
    return start.T.reshape(-1).astype(I32)


def _ffn_kernel(x_ref, wg_ref, wu_ref, wd_ref, o_ref):
    x = x_ref[...]
    acc = jnp.zeros((x.shape[0], D_MODEL), F32)
    for c in range(D_EXPERT // F_CHUNK):
        sl = slice(c * F_CHUNK, (c + 1) * F_CHUNK)
        gte = jnp.dot(x, wg_ref[:, sl], preferred_element_type=F32)
        up = jnp.dot(x, wu_ref[:, sl], preferred_element_type=F32)
        hid = gte * (1.0 / (1.0 + jnp.exp(-gte))) * up
        acc = acc + jnp.dot(hid.astype(BF16), wd_ref[sl, :], preferred_element_type=F32)
    o_ref[...] = acc.astype(BF16)


def _ffn(xe, w_gate, w_up, w_down, cap):
    tm = min(TM_FFN, cap)
    return pl.pallas_call(
        _ffn_kernel,
        grid=(N_EXPERTS, cap // tm),
        in_specs=[pl.BlockSpec((None, tm, D_MODEL), lambda e, i: (e, i, 0)),
                  pl.BlockSpec((None, D_MODEL, D_EXPERT), lambda e, i: (e, 0, 0)),
                  pl.BlockSpec((None, D_MODEL, D_EXPERT), lambda e, i: (e, 0, 0)),
                  pl.BlockSpec((None, D_EXPERT, D_MODEL), lambda e, i: (e, 0, 0))],
        out_specs=pl.BlockSpec((None, tm, D_MODEL), lambda e, i: (e, i, 0)),
        out_shape=jax.ShapeDtypeStruct((N_EXPERTS, cap, D_MODEL), BF16),
        compiler_params=_cparams(("parallel", "arbitrary")),
        name="expert_ffn",
    )(xe, w_gate, w_up, w_down)


def _combine_kernel(b0_ref, st_ref, x_ref, pos_ref, aff_ref, g_ref, *rest):
    ye_refs = rest[:2 * N_EXPERTS]
    o_ref = rest[2 * N_EXPERTS]
    t = pl.program_id(0)
    tt = pos_ref.shape[0]
    slot = lax.broadcasted_iota(I32, (tt, B_SLOT), 1)

    def gated_onehot(e, kk):
        rel = pos_ref[:, e:e + 1] - (b0_ref[t * N_EXPERTS + e] + kk) * B_SLOT
        return jnp.where(slot == rel, aff_ref[:, e:e + 1], 0.0).astype(BF16)

    acc = x_ref[...]
    for e in range(N_EXPERTS):
        acc = acc + jnp.dot(gated_onehot(e, 0), ye_refs[2 * e][...], preferred_element_type=F32)
    o_ref[...] = acc
    for e in range(N_EXPERTS):
        @pl.when(st_ref[t * N_EXPERTS + e] != 0)
        def _(e=e):
            o_ref[...] += jnp.dot(gated_onehot(e, 1), ye_refs[2 * e + 1][...],
                                  preferred_element_type=F32)
    o_ref[...] = _rms(o_ref[...], g_ref[...])


def _combine(x2, pos_c, aff_c, ye, b0, straddle, g_final, cap):
    n = x2.shape[0]
    tt = T_COMB
    n_blocks = cap // B_SLOT

    def ye_spec(e, kk):
        return pl.BlockSpec(
            (None, B_SLOT, D_MODEL),
            lambda t, b0r, st: (e, jnp.minimum(b0r[t * N_EXPERTS + e] + kk, n_blocks - 1), 0))

    ye_specs = [ye_spec(e, kk) for e in range(N_EXPERTS) for kk in range(2)]
    grid_spec = pltpu.PrefetchScalarGridSpec(
        num_scalar_prefetch=2,
        grid=(n // tt,),
        in_specs=[pl.BlockSpec((tt, D_MODEL), lambda t, b0r, st: (t, 0)),
                  pl.BlockSpec((tt, N_EXPERTS), lambda t, b0r, st: (t, 0)),
                  pl.BlockSpec((tt, N_EXPERTS), lambda t, b0r, st: (t, 0)),
                  pl.BlockSpec((1, D_MODEL), lambda t, b0r, st: (0, 0))] + ye_specs,
        out_specs=pl.BlockSpec((tt, D_MODEL), lambda t, b0r, st: (t, 0)),
    )
    return pl.pallas_call(
        _combine_kernel,
        grid_spec=grid_spec,
        out_shape=jax.ShapeDtypeStruct((n, D_MODEL), F32),
        compiler_params=_cparams(("arbitrary",)),
        name="combine_final_norm",
    )(b0, straddle, x2, pos_c, aff_c, g_final, *([ye] * (2 * N_EXPERTS)))


def _encoder(x, mem, p):
    batch, seq, _ = x.shape
    mem_tokens = mem.shape[1]
    n = batch * seq
    cap = CAPACITY_FACTOR * n // N_EXPERTS
    x2d = x.reshape(n, D_MODEL)

    half = DA_DIM // 2
    pos = jnp.arange(seq, dtype=F32)
    inv = ROPE_THETA ** (-jnp.arange(half, dtype=F32) / half)
    ang = pos[:, None] * inv[None, :]
    cos = jnp.cos(ang)
    sin = jnp.sin(ang)
    cos_t = jnp.tile(cos, (1, LANES // half))
    sin_t = jnp.concatenate([jnp.tile(-sin, (1, 2)), jnp.tile(sin, (1, 2))], axis=1)

    zqk, hn = _qk_proj(x2d, p["g_mix"], p["w_qk"], cos_t, sin_t, seq)
    z = _proj(hn, p["w_rest"])
    kt = _ktrans(hn, p["w_kt"])
    gates, gates_t = _gates(hn, p["w_g"], p["w_gt"], p["b_g"], p["b_gt"])

    oa = _diffattn(zqk, z, p["lamv"], p["g_da_out"], batch, seq, p["lam_init"])
    h_fw, h_bw = _mlstm(z, kt, gates, gates_t, batch, seq)

    kv = _memkv(mem.reshape(batch * mem_tokens, D_MODEL), p["g_mem"], p["w_xkv"], mem_tokens)
    x2, hffn, aff_t = _mix(x2d, oa, h_fw, h_bw, z, kv, p, batch, seq, mem_tokens)

    pos3, off3 = _select(aff_t, n, cap)
    pos_t = pos3.reshape(N_EXPERTS, n)
    xe = _dispatch(hffn, pos_t, _slot_starts(off3, T_DISP, cap), cap)
    ye = _ffn(xe, p["w_e_gate"], p["w_e_up"], p["w_e_down"], cap)

    n_blocks = cap // B_SLOT
    start = off3[:, ::T_COMB // LANES, 0]
    end = jnp.concatenate([start[:, 1:], jnp.full((N_EXPERTS, 1), cap, I32)], axis=1)
    b0 = jnp.minimum(start // B_SLOT, n_blocks - 1)
    straddle = (end > start) & ((end - 1) // B_SLOT > b0)
    y = _combine(x2, pos_t.T, aff_t.T, ye, b0.T.reshape(-1).astype(I32),
                 straddle.T.reshape(-1).astype(I32), p["g_final"], cap)
    return y.reshape(batch, seq, D_MODEL)


def kernel(x_prompt, x_sample, mem_prompt, mem_sample, g_mix, w_in, lam_q1, lam_k1, lam_q2, lam_k2,
           g_da_out, b_ml_gates, g_ml_out, w_proj_a, w_proj_b, w_out, g_xa, g_mem, w_xq, w_xkv, w_xo,
           g_ffn, w_router, w_e_gate, w_e_up, w_e_down, g_final):
    l = 0
    w = w_in[l]
    assert w_in.shape[0] == 1, "single-layer encoder"
    gate_lo = W_GATES
    gate_hi = gate_lo + N_GATES
    lane = jnp.arange(LANES)
    part = lane // (DA_DIM // 2)
    in_head = (part % 2) * DA_DIM + (part // 2) * (DA_DIM // 2) + lane % (DA_DIM // 2)
    qk_cols = (jnp.arange(QK_WIDTH // LANES)[:, None] * LANES + in_head[None, :]).reshape(-1)
    w_g = w[:, gate_lo:gate_hi]
    b = b_ml_gates[l].astype(F32)
    p = {
        "g_mix": g_mix[l][None, :],
        "w_qk": w[:, :W_DA_V][:, qk_cols].astype(BF16),
        "w_rest": jnp.concatenate([w[:, W_DA_V:W_ML_K], w[:, W_ML_V:gate_lo], w[:, gate_hi:]],
                                  axis=1).astype(BF16),
        "w_kt": w[:, W_ML_K:W_ML_V].T.astype(BF16),
        "w_g": jnp.pad(w_g, ((0, 0), (0, LANES - N_GATES))).astype(BF16),
        "w_gt": w_g.T.astype(BF16),
        "b_g": jnp.pad(b, (0, LANES - N_GATES))[None, :],
        "b_gt": b[:, None],
        "lamv": jnp.stack([lam_q1[l], lam_k1[l], lam_q2[l], lam_k2[l]]).astype(F32),
        "lam_init": 0.8 - 0.6 * math.exp(-0.3 * l),
        "g_da_out": g_da_out[l][None, :],
        "g_ml_out": g_ml_out[l][None, :],
        "w_proj_a": w_proj_a[l].astype(BF16),
        "w_proj_b": w_proj_b[l].astype(BF16),
        "w_out": w_out[l].astype(BF16),
        "g_xa": g_xa[l][None, :],
        "g_mem": g_mem[l][None, :],
        "w_xq": w_xq[l].astype(BF16),
        "w_xkv": w_xkv[l].astype(BF16),
        "w_xo": w_xo[l].astype(BF16),
        "g_ffn": g_ffn[l][None, :],
        "w_router_t": w_router[l].T.astype(F32),
        "w_e_gate": w_e_gate[l].astype(BF16),
        "w_e_up": w_e_up[l].astype(BF16),
        "w_e_down": w_e_down[l].astype(BF16),
        "g_final": g_final[None, :],
    }
    y_prompt = _encoder(x_prompt, mem_prompt, p)
    y_sample = _encoder(x_sample, mem_sample, p)
    return (y_prompt, y_sample)
```

```python
import functools
import math

import jax
import jax.numpy as jnp
from jax import lax
from jax.experimental import pallas as pl
from jax.experimental.pallas import tpu as pltpu

F32 = jnp.float32
BF16 = jnp.bfloat16
I32 = jnp.int32

D_MODEL = 1024
DA_HEADS = 8
DA_DIM = 64
DA_HEAD_W = 2 * DA_DIM
ML_HEADS = 4
ML_DIM = 256
XA_HEADS = 4
XA_DIM = D_MODEL // XA_HEADS
N_EXPERTS = 16
CAPACITY_FACTOR = 2
D_EXPERT = 2048
EPS = 1e-6
ROPE_THETA = 10000.0
N_GATES = 4 * ML_HEADS

QK_Q = 0
QK_K = 1024
QK_WIDTH = 2048
Z_DA_V = 0
Z_ML_Q = 1024
Z_ML_V = 2048
Z_ML_O = 3072
Z_BR_G = 4096
Z_WIDTH = 6144
W_DA_V = 2048
W_ML_K = 4096
W_ML_V = 5120
W_GATES = 7168

LANES = 128
VMEM_LIMIT = 56 * 1024 * 1024

TM_PROJ = 1024
TN_PROJ = 512
TN_REST = 2048
TQ_ATT = 2048
TK_ATT = 512
ATT_UNROLL = 8
L_MLSTM = 256
TM_MIX = 512
T_DISP = 256
B_SLOT = 256
TM_FFN = 512
F_CHUNK = 512
T_COMB = 256
COMB_STRIP = 256


def _cparams(sem):
    return pltpu.CompilerParams(dimension_semantics=sem, vmem_limit_bytes=VMEM_LIMIT)


def _rms(xf, g):
    return xf * lax.rsqrt(jnp.mean(xf * xf, axis=-1, keepdims=True) + EPS) * g


def _qk_proj_kernel(x_ref, g_ref, w_ref, cos_ref, sin_ref, z_ref, hn_ref, hs_ref, *, n_q_tiles):
    j = pl.program_id(1)

    @pl.when(j == 0)
    def _():
        hb = _rms(x_ref[...], g_ref[...]).astype(BF16)
        hs_ref[...] = hb
        hn_ref[...] = hb

    scale = jnp.where(j < n_q_tiles, DA_DIM ** -0.5 * math.log2(math.e), 1.0)
    acc = jnp.dot(hs_ref[...], w_ref[...], preferred_element_type=F32) * scale
    cos = cos_ref[...]
    sin = sin_ref[...]
    for hh in range(acc.shape[1] // LANES):
        a = acc[:, hh * LANES:(hh + 1) * LANES]
        z_ref[:, hh * LANES:(hh + 1) * LANES] = (a * cos + pltpu.roll(a, LANES // 2, 1) * sin).astype(BF16)


def _qk_proj(x2d, g_mix, w_qk, cos_t, sin_t, seq):
    n = x2d.shape[0]
    tm, tn = TM_PROJ, TN_PROJ
    n_seq_tiles = seq // tm
    return pl.pallas_call(
        functools.partial(_qk_proj_kernel, n_q_tiles=D_MODEL // tn),
        grid=(n // tm, QK_WIDTH // tn),
        in_specs=[
            pl.BlockSpec((tm, D_MODEL), lambda i, j: (i, 0)),
            pl.BlockSpec((1, D_MODEL), lambda i, j: (0, 0)),
            pl.BlockSpec((D_MODEL, tn), lambda i, j: (0, j)),
            pl.BlockSpec((tm, LANES), lambda i, j: (i % n_seq_tiles, 0)),
            pl.BlockSpec((tm, LANES), lambda i, j: (i % n_seq_tiles, 0)),
        ],
        out_specs=[
            pl.BlockSpec((tm, tn), lambda i, j: (i, j)),
            pl.BlockSpec((tm, D_MODEL), lambda i, j: (i, 0)),
        ],
        out_shape=[jax.ShapeDtypeStruct((n, QK_WIDTH), BF16),
                   jax.ShapeDtypeStruct((n, D_MODEL), BF16)],
        scratch_shapes=[pltpu.VMEM((tm, D_MODEL), BF16)],
        compiler_params=_cparams(("parallel", "arbitrary")),
        name="inproj_qk",
    )(x2d, g_mix, w_qk, cos_t, sin_t)


def _proj_kernel(h_ref, w_ref, z_ref):
    z_ref[...] = jnp.dot(h_ref[...], w_ref[...], preferred_element_type=F32).astype(BF16)


def _proj(hn, w_rest):
    n = hn.shape[0]
    tm, tn = TM_PROJ, TN_REST
    return pl.pallas_call(
        _proj_kernel,
        grid=(n // tm, Z_WIDTH // tn),
        in_specs=[pl.BlockSpec((tm, D_MODEL), lambda i, j: (i, 0)),
                  pl.BlockSpec((D_MODEL, tn), lambda i, j: (0, j))],
        out_specs=pl.BlockSpec((tm, tn), lambda i, j: (i, j)),
        out_shape=jax.ShapeDtypeStruct((n, Z_WIDTH), BF16),
        compiler_params=_cparams(("parallel", "arbitrary")),
        name="inproj_rest",
    )(hn, w_rest)


def _ktrans_kernel(w_ref, h_ref, o_ref):
    kt = lax.dot_general(w_ref[...], h_ref[...], (((1,), (1,)), ((), ())),
                         preferred_element_type=F32)
    o_ref[...] = (kt * (ML_DIM ** -0.5)).astype(BF16)


def _ktrans(hn, w_kt):
    n = hn.shape[0]
    tm = TM_PROJ
    return pl.pallas_call(
        _ktrans_kernel,
        grid=(n // tm,),
        in_specs=[pl.BlockSpec((ML_HEADS * ML_DIM, D_MODEL), lambda i: (0, 0)),
                  pl.BlockSpec((tm, D_MODEL), lambda i: (i, 0))],
        out_specs=pl.BlockSpec((ML_HEADS * ML_DIM, tm), lambda i: (0, i)),
        out_shape=jax.ShapeDtypeStruct((ML_HEADS * ML_DIM, n), BF16),
        compiler_params=_cparams(("parallel",)),
        name="ml_k_transposed",
    )(w_kt, hn)


def _log_sigmoid(x):
    return jnp.minimum(x, 0.0) - jnp.log1p(jnp.exp(-jnp.abs(x)))


def _split3(x):
    hi = x.astype(BF16)
    r = x - hi.astype(F32)
    mid = r.astype(BF16)
    lo = (r - mid.astype(F32)).astype(BF16)
    return hi, mid, lo


def _gates_kernel(h_ref, w_ref, wt_ref, b_ref, bt_ref, g_ref, gt_ref):
    h = h_ref[...]
    g = jnp.dot(h, w_ref[...], preferred_element_type=F32) + b_ref[...]
    gt = lax.dot_general(wt_ref[...], h, (((1,), (1,)), ((), ())),
                         preferred_element_type=F32) + bt_ref[...]
    L = L_MLSTM
    n_in = 2 * ML_HEADS
    n_fw = 3 * ML_HEADS
    lane = lax.broadcasted_iota(I32, (L, LANES), 1)
    sub = lax.broadcasted_iota(I32, (N_GATES, L), 0)
    ri = lax.broadcasted_iota(I32, (L, L), 0)
    ci = lax.broadcasted_iota(I32, (L, L), 1)
    lower = jnp.where(ci <= ri, 1.0, 0.0).astype(BF16)
    upper = jnp.where(ci >= ri, 1.0, 0.0).astype(BF16)
    nt = (((1,), (1,)), ((), ()))
    for c in range(g.shape[0] // L):
        gc = g[c * L:(c + 1) * L, :]
        pre = jnp.zeros((L, LANES), F32)
        suf = jnp.zeros((L, LANES), F32)
        for piece in _split3(jnp.where(lane >= n_in, _log_sigmoid(gc), 0.0)):
            pre = pre + jnp.dot(lower, piece, preferred_element_type=F32)
            suf = suf + jnp.dot(upper, piece, preferred_element_type=F32)
        g_ref[c * L:(c + 1) * L, :] = jnp.where(lane < n_in, gc, jnp.where(lane < n_fw, pre, suf))

        gtc = gt[:, c * L:(c + 1) * L]
        pre = jnp.zeros((N_GATES, L), F32)
        suf = jnp.zeros((N_GATES, L), F32)
        for piece in _split3(jnp.where(sub >= n_in, _log_sigmoid(gtc), 0.0)):
            pre = pre + lax.dot_general(piece, lower, nt, preferred_element_type=F32)
            suf = suf + lax.dot_general(piece, upper, nt, preferred_element_type=F32)
        gt_ref[:, c * L:(c + 1) * L] = jnp.where(sub < n_in, gtc, jnp.where(sub < n_fw, pre, suf))


def _gates(hn, w_g, w_gt, b_g, b_gt):
    n = hn.shape[0]
    tm = TM_PROJ
    return pl.pallas_call(
        _gates_kernel,
        grid=(n // tm,),
        in_specs=[pl.BlockSpec((tm, D_MODEL), lambda i: (i, 0)),
                  pl.BlockSpec((D_MODEL, LANES), lambda i: (0, 0)),
                  pl.BlockSpec((N_GATES, D_MODEL), lambda i: (0, 0)),
                  pl.BlockSpec((1, LANES), lambda i: (0, 0)),
                  pl.BlockSpec((N_GATES, 1), lambda i: (0, 0))],
        out_specs=[pl.BlockSpec((tm, LANES), lambda i: (i, 0)),
                   pl.BlockSpec((N_GATES, tm), lambda i: (0, i))],
        out_shape=[jax.ShapeDtypeStruct((n, LANES), F32),
                   jax.ShapeDtypeStruct((N_GATES, n), F32)],
        compiler_params=_cparams(("parallel",)),
        name="ml_gates",
    )(hn, w_g, w_gt, b_g, b_gt)


def _diffattn_kernel(q_ref, k_ref, v_ref, lamv_ref, g_ref, o_ref, va_ref, *, seq, lam_init):
    tq = q_ref.shape[0]
    tk = min(TK_ATT, seq)
    n_chunks = seq // tk
    unroll = min(n_chunks, ATT_UNROLL)

    @pl.when(pl.program_id(2) == 0)
    def _():
        va_ref[:, :DA_HEAD_W] = v_ref[...]
        va_ref[:, DA_HEAD_W:] = jnp.ones((seq, DA_HEAD_W), BF16)

    q = q_ref[...]
    lane = lax.broadcasted_iota(I32, q.shape, 1)
    zero = jnp.zeros_like(q)
    in_map0 = ((lane // (DA_DIM // 2)) % 2) == 0
    qs = (jnp.where(in_map0, q, zero), jnp.where(in_map0, zero, q))

    def chunk(start, carry):
        kc = k_ref[pl.ds(start, tk), :]
        vc = va_ref[pl.ds(start, tk), :]
        out = []
        for qm, (m, l, a) in zip(qs, carry):
            s = lax.dot_general(qm, kc, (((1,), (1,)), ((), ())), preferred_element_type=F32)
            mn = jnp.maximum(m, jnp.max(s, axis=-1, keepdims=True))
            alpha = jnp.exp2(m - mn)
            p = jnp.exp2((s - mn).astype(BF16))
            pv = jnp.dot(p, vc, preferred_element_type=F32)
            out.append((mn, alpha * l + pv[:, DA_HEAD_W:],
                        alpha * a + pv[:, :DA_HEAD_W]))
        return tuple(out)

    def body(c, carry):
        for u in range(unroll):
            carry = chunk(pl.multiple_of((c * unroll + u) * tk, tk), carry)
        return carry

    init = (jnp.full((tq, 1), -jnp.inf, F32), jnp.zeros((tq, DA_HEAD_W), F32),
            jnp.zeros((tq, DA_HEAD_W), F32))
    (m0, l0, a0), (m1, l1, a1) = lax.fori_loop(0, n_chunks // unroll, body, (init, init))

    lv = lamv_ref[...]
    lam = (jnp.exp(jnp.sum(lv[0:1, :] * lv[1:2, :], axis=-1, keepdims=True))
           - jnp.exp(jnp.sum(lv[2:3, :] * lv[3:4, :], axis=-1, keepdims=True)) + lam_init)
    o = a0 * (1.0 / l0) - lam * (a1 * (1.0 / l1))
    o_ref[...] = (_rms(o, g_ref[...]) * (1.0 - lam_init)).astype(BF16)


def _diffattn(zqk, z, lamv, g_da, batch, seq, lam_init):
    n = z.shape[0]
    tq = min(TQ_ATT, seq)
    nq = seq // tq
    hq = QK_Q // DA_HEAD_W
    hk = QK_K // DA_HEAD_W
    hv = Z_DA_V // DA_HEAD_W
    return pl.pallas_call(
        functools.partial(_diffattn_kernel, seq=seq, lam_init=lam_init),
        grid=(batch, DA_HEADS, nq),
        in_specs=[
            pl.BlockSpec((tq, DA_HEAD_W), lambda b, h, i: (b * nq + i, hq + h)),
            pl.BlockSpec((seq, DA_HEAD_W), lambda b, h, i: (b, hk + h)),
            pl.BlockSpec((seq, DA_HEAD_W), lambda b, h, i: (b, hv + h)),
            pl.BlockSpec((4, DA_DIM), lambda b, h, i: (0, 0)),
            pl.BlockSpec((1, DA_HEAD_W), lambda b, h, i: (0, 0)),
        ],
        out_specs=pl.BlockSpec((tq, DA_HEAD_W), lambda b, h, i: (b * nq + i, h)),
        out_shape=jax.ShapeDtypeStruct((n, DA_HEADS * DA_HEAD_W), BF16),
        scratch_shapes=[pltpu.VMEM((seq, 2 * DA_HEAD_W), BF16)],
        compiler_params=_cparams(("parallel", "parallel", "arbitrary")),
        name="diff_attention",
    )(zqk, zqk, z, lamv, g_da)


def _mlstm_chunk(q, kt, v, g, gt, cta_ref, m_ref, h_ref, head, reverse):
    L = q.shape[0]
    col_i = head + (ML_HEADS if reverse else 0)
    col_f = col_i + 2 * ML_HEADS
    lane = lax.broadcasted_iota(I32, g.shape, 1)
    sub = lax.broadcasted_iota(I32, gt.shape, 0)
    i_col = jnp.sum(jnp.where(lane == col_i, g, 0.0), axis=1, keepdims=True)
    g_col = jnp.sum(jnp.where(lane == col_f, g, 0.0), axis=1, keepdims=True)
    i_row = jnp.sum(jnp.where(sub == col_i, gt, 0.0), axis=0, keepdims=True)
    g_row = jnp.sum(jnp.where(sub == col_f, gt, 0.0), axis=0, keepdims=True)

    ri = lax.broadcasted_iota(I32, (L, L), 0)
    ci = lax.broadcasted_iota(I32, (L, L), 1)
    tri = (ci >= ri) if reverse else (ci <= ri)
    last = 0 if reverse else L - 1
    lane_l = lax.broadcasted_iota(I32, (1, L), 1)
    g_last = jnp.sum(jnp.where(lane_l == last, g_row, 0.0), axis=1, keepdims=True)

    m_prev = m_ref[...]
    dmat = jnp.where(tri, g_col + (i_row - g_row), -jnp.inf)
    a_col = g_col + m_prev
    mj = jnp.maximum(a_col, jnp.max(dmat, axis=1, keepdims=True))
    w_inter = jnp.exp(a_col - mj)
    s = jnp.dot(q, kt, preferred_element_type=F32) * jnp.exp(dmat - mj)
    ones = jnp.ones((L, LANES), BF16)
    cta = cta_ref[...]
    num = (w_inter * jnp.dot(q, cta.astype(BF16), preferred_element_type=F32)
           + jnp.dot(s.astype(BF16), jnp.concatenate([v, ones], axis=1), preferred_element_type=F32))
    inv = 1.0 / jnp.maximum(jnp.abs(num[:, ML_DIM:]), jnp.exp(-mj))
    h_ref[...] = (num[:, :ML_DIM] * jnp.concatenate([inv] * (ML_DIM // LANES), axis=1)).astype(BF16)

    dec_row = g_last - g_row + i_row
    dec_col = g_last - g_col + i_col
    m_new = jnp.maximum(g_last + m_prev, jnp.max(dec_row, axis=1, keepdims=True))
    wk_col = jnp.exp(dec_col - m_new)
    carry_scale = jnp.exp(g_last + m_prev - m_new)
    wv = jnp.concatenate([(wk_col * v.astype(F32)).astype(BF16),
                          jnp.broadcast_to(wk_col, (L, LANES)).astype(BF16)], axis=1)
    cta_ref[...] = carry_scale * cta + jnp.dot(kt, wv, preferred_element_type=F32)
    m_ref[...] = m_new


def _mlstm_kernel(qf_ref, ktf_ref, vf_ref, gf_ref, gtf_ref, qb_ref, ktb_ref, vb_ref, gb_ref, gtb_ref,
                  hf_ref, hb_ref, ctf_ref, mf_ref, ctb_ref, mb_ref):
    @pl.when(pl.program_id(1) == 0)
    def _():
        ctf_ref[...] = jnp.zeros_like(ctf_ref)
        mf_ref[...] = jnp.zeros_like(mf_ref)
        ctb_ref[...] = jnp.zeros_like(ctb_ref)
        mb_ref[...] = jnp.zeros_like(mb_ref)

    for head in range(ML_HEADS):
        sl = slice(head * ML_DIM, (head + 1) * ML_DIM)
        _mlstm_chunk(qf_ref[:, sl], ktf_ref[sl, :], vf_ref[:, sl], gf_ref[...], gtf_ref[...],
                     ctf_ref.at[head], mf_ref.at[head], hf_ref.at[:, sl], head, False)
        _mlstm_chunk(qb_ref[:, sl], ktb_ref[sl, :], vb_ref[:, sl], gb_ref[...], gtb_ref[...],
                     ctb_ref.at[head], mb_ref.at[head], hb_ref.at[:, sl], head, True)


def _mlstm(z, kt, gates, gates_t, batch, seq):
    n = z.shape[0]
    L = L_MLSTM
    nc = seq // L
    width = ML_HEADS * ML_DIM
    cq = Z_ML_Q // width
    cv = Z_ML_V // width

    def specs(tix):
        return [
            pl.BlockSpec((L, width), lambda b, t: (b * nc + tix(t), cq)),
            pl.BlockSpec((width, L), lambda b, t: (0, b * nc + tix(t))),
            pl.BlockSpec((L, width), lambda b, t: (b * nc + tix(t), cv)),
            pl.BlockSpec((L, LANES), lambda b, t: (b * nc + tix(t), 0)),
            pl.BlockSpec((N_GATES, L), lambda b, t: (0, b * nc + tix(t))),
        ]

    fwd = lambda t: t
    bwd = lambda t: nc - 1 - t
    out = jax.ShapeDtypeStruct((n, width), BF16)
    state = [pltpu.VMEM((ML_HEADS, ML_DIM, ML_DIM + LANES), F32), pltpu.VMEM((ML_HEADS, 1, 1), F32)]
    return pl.pallas_call(
        _mlstm_kernel,
        grid=(batch, nc),
        in_specs=specs(fwd) + specs(bwd),
        out_specs=[pl.BlockSpec((L, width), lambda b, t: (b * nc + fwd(t), 0)),
                   pl.BlockSpec((L, width), lambda b, t: (b * nc + bwd(t), 0))],
        out_shape=[out, out],
        scratch_shapes=state + state,
        compiler_params=_cparams(("parallel", "arbitrary")),
        name="mlstm",
    )(z, kt, z, gates, gates_t, z, kt, z, gates, gates_t)


def _memkv_kernel(m_ref, g_ref, w_ref, o_ref):
    mn = _rms(m_ref[...], g_ref[...]).astype(BF16)
    o_ref[...] = jnp.dot(mn, w_ref[...], preferred_element_type=F32).astype(BF16)


def _memkv(mem2d, g_mem, w_xkv, mem_tokens):
    n = mem2d.shape[0]
    return pl.pallas_call(
        _memkv_kernel,
        grid=(n // mem_tokens,),
        in_specs=[pl.BlockSpec((mem_tokens, D_MODEL), lambda b: (b, 0)),
                  pl.BlockSpec((1, D_MODEL), lambda b: (0, 0)),
                  pl.BlockSpec((D_MODEL, 2 * D_MODEL), lambda b: (0, 0))],
        out_specs=pl.BlockSpec((mem_tokens, 2 * D_MODEL), lambda b: (b, 0)),
        out_shape=jax.ShapeDtypeStruct((n, 2 * D_MODEL), BF16),
        compiler_params=_cparams(("parallel",)),
        name="mem_kv",
    )(mem2d, g_mem, w_xkv)


def _sigmoid(x):
    return 1.0 / (1.0 + jnp.exp(-x))


def _mix_kernel(x_ref, oa_ref, hf_ref, hb_ref, mo_ref, bga_ref, bgb_ref, kv_ref,
                gml_ref, gxa_ref, gffn_ref, wpa_ref, wpb_ref, wo_ref, wxq_ref, wxo_ref, wrt_ref,
                x2_ref, hn_ref, afft_ref):
    hsum = hf_ref[...].astype(F32) + hb_ref[...].astype(F32)
    gml = gml_ref[...]
    parts = []
    for hh in range(ML_HEADS):
        sl = slice(hh * ML_DIM, (hh + 1) * ML_DIM)
        parts.append(_rms(hsum[:, sl], gml[:, sl]))
    hb = jnp.concatenate(parts, axis=1)
    hb = hb * _sigmoid(mo_ref[...].astype(F32))
    y_a = jnp.dot(oa_ref[...], wpa_ref[...], preferred_element_type=F32)
    y_b = jnp.dot(hb.astype(BF16), wpb_ref[...], preferred_element_type=F32)
    merged = (_sigmoid(bga_ref[...].astype(F32)) * y_a
              + _sigmoid(bgb_ref[...].astype(F32)) * y_b)
    x1 = x_ref[...] + jnp.dot(merged.astype(BF16), wo_ref[...], preferred_element_type=F32)

    hq = _rms(x1, gxa_ref[...]).astype(BF16)
    qx = (jnp.dot(hq, wxq_ref[...], preferred_element_type=F32) * (XA_DIM ** -0.5)).astype(BF16)
    kv = kv_ref[...]
    outs = []
    for hh in range(XA_HEADS):
        sl = slice(hh * XA_DIM, (hh + 1) * XA_DIM)
        kh = kv[:, sl]
        vh = kv[:, D_MODEL + hh * XA_DIM:D_MODEL + (hh + 1) * XA_DIM]
        s = lax.dot_general(qx[:, sl], kh, (((1,), (1,)), ((), ())), preferred_element_type=F32)
        p = jnp.exp(s - jnp.max(s, axis=-1, keepdims=True))
        p = p / jnp.sum(p, axis=-1, keepdims=True)
        outs.append(jnp.dot(p.astype(BF16), vh, preferred_element_type=F32).astype(BF16))
    o = jnp.concatenate(outs, axis=1)
    x2 = x1 + jnp.dot(o, wxo_ref[...], preferred_element_type=F32)
    x2_ref[...] = x2

    hn = _rms(x2, gffn_ref[...])
    hn_ref[...] = hn.astype(BF16)
    logits_t = lax.dot_general(wrt_ref[...], hn, (((1,), (1,)), ((), ())),
                               precision=lax.Precision.HIGHEST, preferred_element_type=F32)
    e = jnp.exp(logits_t - jnp.max(logits_t, axis=0, keepdims=True))
    afft_ref[...] = e / jnp.sum(e, axis=0, keepdims=True)


def _mix(x2d, oa, h_fw, h_bw, z, kv, p, batch, seq, mem_tokens):
    n = x2d.shape[0]
    tm = TM_MIX
    nt = seq // tm
    cmo = Z_ML_O // D_MODEL
    cbg = Z_BR_G // D_MODEL
    full = lambda r, c: pl.BlockSpec((r, c), lambda i: (0, 0), pipeline_mode=pl.Buffered(1))
    return pl.pallas_call(
        _mix_kernel,
        grid=(n // tm,),
        in_specs=[
            pl.BlockSpec((tm, D_MODEL), lambda i: (i, 0)),
            pl.BlockSpec((tm, D_MODEL), lambda i: (i, 0)),
            pl.BlockSpec((tm, D_MODEL), lambda i: (i, 0)),
            pl.BlockSpec((tm, D_MODEL), lambda i: (i, 0)),
            pl.BlockSpec((tm, D_MODEL), lambda i: (i, cmo)),
            pl.BlockSpec((tm, D_MODEL), lambda i: (i, cbg)),
            pl.BlockSpec((tm, D_MODEL), lambda i: (i, cbg + 1)),
            pl.BlockSpec((mem_tokens, 2 * D_MODEL), lambda i: (i // nt, 0)),
            full(1, D_MODEL), full(1, D_MODEL), full(1, D_MODEL),
            full(D_MODEL, D_MODEL), full(D_MODEL, D_MODEL), full(D_MODEL, D_MODEL),
            full(D_MODEL, D_MODEL), full(D_MODEL, D_MODEL),
            full(N_EXPERTS, D_MODEL),
        ],
        out_specs=[pl.BlockSpec((tm, D_MODEL), lambda i: (i, 0)),
                   pl.BlockSpec((tm, D_MODEL), lambda i: (i, 0)),
                   pl.BlockSpec((N_EXPERTS, tm), lambda i: (0, i))],
        out_shape=[jax.ShapeDtypeStruct((n, D_MODEL), F32),
                   jax.ShapeDtypeStruct((n, D_MODEL), BF16),
                   jax.ShapeDtypeStruct((N_EXPERTS, n), F32)],
        compiler_params=_cparams(("parallel",)),
        name="mix_xattn_router",
    )(x2d, oa, h_fw, h_bw, z, z, z, kv,
      p["g_ml_out"], p["g_xa"], p["g_ffn"], p["w_proj_a"], p["w_proj_b"], p["w_out"],
      p["w_xq"], p["w_xo"], p["w_router_t"])


def _prefix_incl(mask_f, upper, ones, lower):
    mb = mask_f.astype(BF16)
    lane_incl = jnp.dot(mb, upper, preferred_element_type=F32)
    row_tot = jnp.dot(mb, ones, preferred_element_type=F32)
    row_off = jnp.dot(lower, row_tot.astype(BF16), preferred_element_type=F32)
    return lane_incl + row_off, row_off


def _select_kernel(a_ref, pos_ref, off_ref, *, cap):
    a = a_ref[...]
    rows = a.shape[0]
    bits = pltpu.bitcast(a, I32)

    def body(i, thr):
        cand = thr | jnp.left_shift(jnp.int32(1), 30 - i)
        cnt = jnp.sum(jnp.where(bits >= cand, 1.0, 0.0), keepdims=True)
        return jnp.where(cnt >= cap, cand, thr)

    thr = lax.fori_loop(0, 31, body, jnp.zeros((1, 1), I32))
    gt = bits > thr
    eq = bits == thr
    n_gt = jnp.sum(jnp.where(gt, 1.0, 0.0), keepdims=True)
    need = cap - n_gt

    li = lax.broadcasted_iota(I32, (LANES, LANES), 0)
    lj = lax.broadcasted_iota(I32, (LANES, LANES), 1)
    upper = (li <= lj).astype(BF16)
    ones = jnp.ones((LANES, LANES), BF16)
    ri = lax.broadcasted_iota(I32, (rows, rows), 0)
    rj = lax.broadcasted_iota(I32, (rows, rows), 1)
    lower = (rj < ri).astype(BF16)

    eq_rank, _ = _prefix_incl(jnp.where(eq, 1.0, 0.0), upper, ones, lower)
    sel = gt | (eq & (eq_rank <= need))
    incl, row_off = _prefix_incl(jnp.where(sel, 1.0, 0.0), upper, ones, lower)
    pos_ref[...] = jnp.where(sel, incl - 1.0, -1.0).astype(I32)
    off_ref[...] = row_off.astype(I32)


def _select(aff_t, n_tokens, cap):
    rows = n_tokens // LANES
    a3 = aff_t.reshape(N_EXPERTS, rows, LANES)
    spec = pl.BlockSpec((None, rows, LANES), lambda e: (e, 0, 0))
    return pl.pallas_call(
        functools.partial(_select_kernel, cap=float(cap)),
        grid=(N_EXPERTS,),
        in_specs=[spec],
        out_specs=[spec, spec],
        out_shape=[jax.ShapeDtypeStruct((N_EXPERTS, rows, LANES), I32),
                   jax.ShapeDtypeStruct((N_EXPERTS, rows, LANES), I32)],
        compiler_params=_cparams(("parallel",)),
        name="expert_choice_select",
    )(a3)


def _dispatch_kernel(s_ref, x_ref, pos_ref, xe_ref, buf_ref, stage_ref, sem_ref, *, n_blocks):
    t = pl.program_id(0)
    shift = B_SLOT.bit_length() - 1
    x = x_ref[...]
    tt = x.shape[0]
    slot = lax.broadcasted_iota(I32, (B_SLOT, tt), 0)
    row = lax.broadcasted_iota(I32, (B_SLOT, 1), 0)

    def block_copy(e, par, blk):
        first = blk * B_SLOT
        if not isinstance(first, int):
            first = pl.multiple_of(first, B_SLOT)
        return pltpu.make_async_copy(buf_ref.at[e, par], xe_ref.at[e, pl.ds(first, B_SLOT)],
                                     sem_ref.at[e, par])

    def gather_rows(e, blk):
        rel = pos_ref[e:e + 1, :] - blk * B_SLOT
        onehot = jnp.where(slot == rel, 1.0, 0.0).astype(BF16)
        return jnp.dot(onehot, x, preferred_element_type=F32).astype(BF16)

    for e in range(N_EXPERTS):
        stage_ref[e] = gather_rows(e, lax.shift_right_logical(s_ref[t * N_EXPERTS + e], shift))

    for e in range(N_EXPERTS):
        s0 = s_ref[t * N_EXPERTS + e]
        s1 = s_ref[(t + 1) * N_EXPERTS + e]
        b0 = lax.shift_right_logical(s0, shift)
        r0 = s0 - b0 * B_SLOT
        par = b0 & 1

        @pl.when(s1 > s0)
        def _(e=e, s1=s1, b0=b0, r0=r0, par=par):
            @pl.when(r0 == 0)
            def _():
                @pl.when(b0 >= 2)
                def _():
                    block_copy(e, par, b0 - 2).wait()
                buf_ref[e, par] = stage_ref[e]

            @pl.when(r0 != 0)
            def _():
                mine = (row >= r0) & (row < s1 - b0 * B_SLOT)
                buf_ref[e, par] = jnp.where(mine, stage_ref[e], buf_ref[e, par])

            @pl.when(s1 > (b0 + 1) * B_SLOT)
            def _():
                @pl.when(b0 >= 1)
                def _():
                    block_copy(e, 1 - par, b0 - 1).wait()
                buf_ref[e, 1 - par] = gather_rows(e, b0 + 1)

            @pl.when(s1 >= (b0 + 1) * B_SLOT)
            def _():
                block_copy(e, par, b0).start()

    @pl.when(t == pl.num_programs(0) - 1)
    def _():
        for e in range(N_EXPERTS):
            for blk in range(max(n_blocks - 2, 0), n_blocks):
                block_copy(e, blk & 1, blk).wait()


def _dispatch(hn, pos_t, slot_starts, cap):
    n = hn.shape[0]
    tt = T_DISP
    n_blocks = cap // B_SLOT
    grid_spec = pltpu.PrefetchScalarGridSpec(
        num_scalar_prefetch=1,
        grid=(n // tt,),
        in_specs=[pl.BlockSpec((tt, D_MODEL), lambda t, s: (t, 0)),
                  pl.BlockSpec((N_EXPERTS, tt), lambda t, s: (0, t))],
        out_specs=pl.BlockSpec(memory_space=pl.ANY),
        scratch_shapes=[pltpu.VMEM((N_EXPERTS, 2, B_SLOT, D_MODEL), BF16),
                        pltpu.VMEM((N_EXPERTS, B_SLOT, D_MODEL), BF16),
                        pltpu.SemaphoreType.DMA((N_EXPERTS, 2))],
    )
    return pl.pallas_call(
        functools.partial(_dispatch_kernel, n_blocks=n_blocks),
        grid_spec=grid_spec,
        out_shape=jax.ShapeDtypeStruct((N_EXPERTS, cap, D_MODEL), BF16),
        compiler_params=_cparams(("arbitrary",)),
        name="dispatch",
    )(slot_starts, hn, pos_t)


def _slot_starts(row_off, tile, cap):
    start = row_off[:, ::tile // LANES, 0]
    start = jnp.concatenate([start, jnp.full((N_EXPERTS, 1), cap, I32)], axis=1)
    return start.T.reshape(-1).astype(I32)


def _ffn_kernel(x_ref, wg_ref, wu_ref, wd_ref, o_ref):
    x = x_ref[...]
    acc = jnp.zeros((x.shape[0], D_MODEL), F32)
    for c in range(D_EXPERT // F_CHUNK):
        sl = slice(c * F_CHUNK, (c + 1) * F_CHUNK)
        gte = jnp.dot(x, wg_ref[:, sl], preferred_element_type=F32)
        up = jnp.dot(x, wu_ref[:, sl], preferred_element_type=F32)
        hid = gte * (1.0 / (1.0 + jnp.exp(-gte))) * up
        acc = acc + jnp.dot(hid.astype(BF16), wd_ref[sl, :], preferred_element_type=F32)
    o_ref[...] = acc.astype(BF16)


def _ffn(xe, w_gate, w_up, w_down, cap):
    tm = min(TM_FFN, cap)
    return pl.pallas_call(
        _ffn_kernel,
        grid=(N_EXPERTS, cap // tm),
        in_specs=[pl.BlockSpec((None, tm, D_MODEL), lambda e, i: (e, i, 0)),
                  pl.BlockSpec((None, D_MODEL, D_EXPERT), lambda e, i: (e, 0, 0)),
                  pl.BlockSpec((None, D_MODEL, D_EXPERT), lambda e, i: (e, 0, 0)),
                  pl.BlockSpec((None, D_EXPERT, D_MODEL), lambda e, i: (e, 0, 0))],
        out_specs=pl.BlockSpec((None, tm, D_MODEL), lambda e, i: (e, i, 0)),
        out_shape=jax.ShapeDtypeStruct((N_EXPERTS, cap, D_MODEL), BF16),
        compiler_params=_cparams(("parallel", "arbitrary")),
        name="expert_ffn",
    )(xe, w_gate, w_up, w_down)


def _combine_kernel(b0_ref, st_ref, x_ref, pos_ref, aff_ref, g_ref, *rest):
    ye_refs = rest[:2 * N_EXPERTS]
    o_ref = rest[2 * N_EXPERTS]
    t = pl.program_id(0)
    tt = pos_ref.shape[0]
    slot = lax.broadcasted_iota(I32, (tt, B_SLOT), 1)

    def gated_onehot(e, kk):
        rel = pos_ref[:, e:e + 1] - (b0_ref[t * N_EXPERTS + e] + kk) * B_SLOT
        return jnp.where(slot == rel, aff_ref[:, e:e + 1], 0.0).astype(BF16)

    onehots = [gated_onehot(e, 0) for e in range(N_EXPERTS)]
    for c in range(D_MODEL // COMB_STRIP):
        sl = slice(c * COMB_STRIP, (c + 1) * COMB_STRIP)
        acc = x_ref[:, sl]
        for e in range(N_EXPERTS):
            acc = acc + jnp.dot(onehots[e], ye_refs[2 * e][:, sl], preferred_element_type=F32)
        o_ref[:, sl] = acc
    for e in range(N_EXPERTS):
        @pl.when(st_ref[t * N_EXPERTS + e] != 0)
        def _(e=e):
            o_ref[...] += jnp.dot(gated_onehot(e, 1), ye_refs[2 * e + 1][...],
                                  preferred_element_type=F32)
    o_ref[...] = _rms(o_ref[...], g_ref[...])


def _combine(x2, pos_c, aff_c, ye, b0, straddle, g_final, cap):
    n = x2.shape[0]
    tt = T_COMB
    n_blocks = cap // B_SLOT

    def ye_spec(e, kk):
        return pl.BlockSpec(
            (None, B_SLOT, D_MODEL),
            lambda t, b0r, st: (e, jnp.minimum(b0r[t * N_EXPERTS + e] + kk, n_blocks - 1), 0))

    ye_specs = [ye_spec(e, kk) for e in range(N_EXPERTS) for kk in range(2)]
    grid_spec = pltpu.PrefetchScalarGridSpec(
        num_scalar_prefetch=2,
        grid=(n // tt,),
        in_specs=[pl.BlockSpec((tt, D_MODEL), lambda t, b0r, st: (t, 0)),
                  pl.BlockSpec((tt, N_EXPERTS), lambda t, b0r, st: (t, 0)),
                  pl.BlockSpec((tt, N_EXPERTS), lambda t, b0r, st: (t, 0)),
                  pl.BlockSpec((1, D_MODEL), lambda t, b0r, st: (0, 0))] + ye_specs,
        out_specs=pl.BlockSpec((tt, D_MODEL), lambda t, b0r, st: (t, 0)),
    )
    return pl.pallas_call(
        _combine_kernel,
        grid_spec=grid_spec,
        out_shape=jax.ShapeDtypeStruct((n, D_MODEL), F32),
        compiler_params=_cparams(("arbitrary",)),
        name="combine_final_norm",
    )(b0, straddle, x2, pos_c, aff_c, g_final, *([ye] * (2 * N_EXPERTS)))


def _encoder(x, mem, p):
    batch, seq, _ = x.shape
    mem_tokens = mem.shape[1]
    n = batch * seq
    cap = CAPACITY_FACTOR * n // N_EXPERTS
    x2d = x.reshape(n, D_MODEL)

    half = DA_DIM // 2
    pos = jnp.arange(seq, dtype=F32)
    inv = ROPE_THETA ** (-jnp.arange(half, dtype=F32) / half)
    ang = pos[:, None] * inv[None, :]
    cos = jnp.cos(ang)
    sin = jnp.sin(ang)
    cos_t = jnp.tile(cos, (1, LANES // half))
    sin_t = jnp.concatenate([jnp.tile(-sin, (1, 2)), jnp.tile(sin, (1, 2))], axis=1)

    zqk, hn = _qk_proj(x2d, p["g_mix"], p["w_qk"], cos_t, sin_t, seq)
    z = _proj(hn, p["w_rest"])
    kt = _ktrans(hn, p["w_kt"])
    gates, gates_t = _gates(hn, p["w_g"], p["w_gt"], p["b_g"], p["b_gt"])

    oa = _diffattn(zqk, z, p["lamv"], p["g_da_out"], batch, seq, p["lam_init"])
    h_fw, h_bw = _mlstm(z, kt, gates, gates_t, batch, seq)

    kv = _memkv(mem.reshape(batch * mem_tokens, D_MODEL), p["g_mem"], p["w_xkv"], mem_tokens)
    x2, hffn, aff_t = _mix(x2d, oa, h_fw, h_bw, z, kv, p, batch, seq, mem_tokens)

    pos3, off3 = _select(aff_t, n, cap)
    pos_t = pos3.reshape(N_EXPERTS, n)
    xe = _dispatch(hffn, pos_t, _slot_starts(off3, T_DISP, cap), cap)
    ye = _ffn(xe, p["w_e_gate"], p["w_e_up"], p["w_e_down"], cap)

    n_blocks = cap // B_SLOT
    start = off3[:, ::T_COMB // LANES, 0]
    end = jnp.concatenate([start[:, 1:], jnp.full((N_EXPERTS, 1), cap, I32)], axis=1)
    b0 = jnp.minimum(start // B_SLOT, n_blocks - 1)
    straddle = (end > start) & ((end - 1) // B_SLOT > b0)
    y = _combine(x2, pos_t.T, aff_t.T, ye, b0.T.reshape(-1).astype(I32),
                 straddle.T.reshape(-1).astype(I32), p["g_final"], cap)
    return y.reshape(batch, seq, D_MODEL)


def kernel(x_prompt, x_sample, mem_prompt, mem_sample, g_mix, w_in, lam_q1, lam_k1, lam_q2, lam_k2,
           g_da_out, b_ml_gates, g_ml_out, w_proj_a, w_proj_b, w_out, g_xa, g_mem, w_xq, w_xkv, w_xo,
           g_ffn, w_router, w_e_gate, w_e_up, w_e_down, g_final):
    l = 0
    w = w_in[l]
    assert w_in.shape[0] == 1, "single-layer encoder"
    gate_lo = W_GATES
    gate_hi = gate_lo + N_GATES
    lane = jnp.arange(LANES)
    part = lane // (DA_DIM // 2)
    in_head = (part % 2) * DA_DIM + (part // 2) * (DA_DIM // 2) + lane % (DA_DIM // 2)
    qk_cols = (jnp.arange(QK_WIDTH // LANES)[:, None] * LANES + in_head[None, :]).reshape(-1)
    w_g = w[:, gate_lo:gate_hi]
    b = b_ml_gates[l].astype(F32)
    p = {
        "g_mix": g_mix[l][None, :],
        "w_qk": w[:, :W_DA_V][:, qk_cols].astype(BF16),
        "w_rest": jnp.concatenate([w[:, W_DA_V:W_ML_K], w[:, W_ML_V:gate_lo], w[:, gate_hi:]],
                                  axis=1).astype(BF16),
        "w_kt": w[:, W_ML_K:W_ML_V].T.astype(BF16),
        "w_g": jnp.pad(w_g, ((0, 0), (0, LANES - N_GATES))).astype(BF16),
        "w_gt": w_g.T.astype(BF16),
        "b_g": jnp.pad(b, (0, LANES - N_GATES))[None, :],
        "b_gt": b[:, None],
        "lamv": jnp.stack([lam_q1[l], lam_k1[l], lam_q2[l], lam_k2[l]]).astype(F32),
        "lam_init": 0.8 - 0.6 * math.exp(-0.3 * l),
        "g_da_out": g_da_out[l][None, :],
        "g_ml_out": g_ml_out[l][None, :],
        "w_proj_a": w_proj_a[l].astype(BF16),
        "w_proj_b": w_proj_b[l].astype(BF16),
        "w_out": w_out[l].astype(BF16),
        "g_xa": g_xa[l][None, :],
        "g_mem": g_mem[l][None, :],
        "w_xq": w_xq[l].astype(BF16),
        "w_xkv": w_xkv[l].astype(BF16),
        "w_xo": w_xo[l].astype(BF16),
        "g_ffn": g_ffn[l][None, :],
        "w_router_t": w_router[l].T.astype(F32),
        "w_e_gate": w_e_gate[l].astype(BF16),
        "w_e_up": w_e_up[l].astype(BF16),
        "w_e_down": w_e_down[l].astype(BF16),
        "g_final": g_final[None, :],
    }
    y_prompt = _encoder(x_prompt, mem_prompt, p)
    y_sample = _encoder(x_sample, mem_sample, p)
    return (y_prompt, y_sample)
```

```python
import functools
import math

import jax
import jax.numpy as jnp
from jax import lax
from jax.experimental import pallas as pl
from jax.experimental.pallas import tpu as pltpu

F32 = jnp.float32
BF16 = jnp.bfloat16
I32 = jnp.int32

D_MODEL = 1024
DA_HEADS = 8
DA_DIM = 64
DA_HEAD_W = 2 * DA_DIM
ML_HEADS = 4
ML_DIM = 256
XA_HEADS = 4
XA_DIM = D_MODEL // XA_HEADS
N_EXPERTS = 16
CAPACITY_FACTOR = 2
D_EXPERT = 2048
EPS = 1e-6
ROPE_THETA = 10000.0
N_GATES = 4 * ML_HEADS

QK_Q = 0
QK_K = 1024
QK_WIDTH = 2048
Z_DA_V = 0
Z_ML_Q = 1024
Z_ML_V = 2048
Z_ML_O = 3072
Z_BR_G = 4096
Z_WIDTH = 6144
W_DA_V = 2048
W_ML_K = 4096
W_ML_V = 5120
W_GATES = 7168

LANES = 128
VMEM_LIMIT = 56 * 1024 * 1024

TM_PROJ = 1024
TN_PROJ = 1024
TN_REST = 2048
TQ_ATT = 2048
TK_ATT = 512
ATT_UNROLL = 8
L_MLSTM = 256
TM_MIX = 512
T_DISP = 256
B_SLOT = 256
TM_FFN = 512
F_CHUNK = 512
T_COMB = 256
COMB_STRIP = 256


def _cparams(sem):
    return pltpu.CompilerParams(dimension_semantics=sem, vmem_limit_bytes=VMEM_LIMIT)


def _rms(xf, g):
    return xf * lax.rsqrt(jnp.mean(xf * xf, axis=-1, keepdims=True) + EPS) * g


def _qk_proj_kernel(x_ref, g_ref, w_ref, cos_ref, sin_ref, z_ref, hn_ref, hs_ref, *, n_q_tiles):
    j = pl.program_id(1)

    @pl.when(j == 0)
    def _():
        hb = _rms(x_ref[...], g_ref[...]).astype(BF16)
        hs_ref[...] = hb
        hn_ref[...] = hb

    scale = jnp.where(j < n_q_tiles, DA_DIM ** -0.5 * math.log2(math.e), 1.0)
    acc = jnp.dot(hs_ref[...], w_ref[...], preferred_element_type=F32) * scale
    cos = cos_ref[...]
    sin = sin_ref[...]
    for hh in range(acc.shape[1] // LANES):
        a = acc[:, hh * LANES:(hh + 1) * LANES]
        z_ref[:, hh * LANES:(hh + 1) * LANES] = (a * cos + pltpu.roll(a, LANES // 2, 1) * sin).astype(BF16)


def _qk_proj(x2d, g_mix, w_qk, cos_t, sin_t, seq):
    n = x2d.shape[0]
    tm, tn = TM_PROJ, TN_PROJ
    n_seq_tiles = seq // tm
    return pl.pallas_call(
        functools.partial(_qk_proj_kernel, n_q_tiles=D_MODEL // tn),
        grid=(n // tm, QK_WIDTH // tn),
        in_specs=[
            pl.BlockSpec((tm, D_MODEL), lambda i, j: (i, 0)),
            pl.BlockSpec((1, D_MODEL), lambda i, j: (0, 0)),
            pl.BlockSpec((D_MODEL, tn), lambda i, j: (0, j)),
            pl.BlockSpec((tm, LANES), lambda i, j: (i % n_seq_tiles, 0)),
            pl.BlockSpec((tm, LANES), lambda i, j: (i % n_seq_tiles, 0)),
        ],
        out_specs=[
            pl.BlockSpec((tm, tn), lambda i, j: (i, j)),
            pl.BlockSpec((tm, D_MODEL), lambda i, j: (i, 0)),
        ],
        out_shape=[jax.ShapeDtypeStruct((n, QK_WIDTH), BF16),
                   jax.ShapeDtypeStruct((n, D_MODEL), BF16)],
        scratch_shapes=[pltpu.VMEM((tm, D_MODEL), BF16)],
        compiler_params=_cparams(("parallel", "arbitrary")),
        name="inproj_qk",
    )(x2d, g_mix, w_qk, cos_t, sin_t)


def _proj_kernel(h_ref, w_ref, z_ref):
    z_ref[...] = jnp.dot(h_ref[...], w_ref[...], preferred_element_type=F32).astype(BF16)


def _proj(hn, w_rest):
    n = hn.shape[0]
    tm, tn = TM_PROJ, TN_REST
    return pl.pallas_call(
        _proj_kernel,
        grid=(n // tm, Z_WIDTH // tn),
        in_specs=[pl.BlockSpec((tm, D_MODEL), lambda i, j: (i, 0)),
                  pl.BlockSpec((D_MODEL, tn), lambda i, j: (0, j))],
        out_specs=pl.BlockSpec((tm, tn), lambda i, j: (i, j)),
        out_shape=jax.ShapeDtypeStruct((n, Z_WIDTH), BF16),
        compiler_params=_cparams(("parallel", "arbitrary")),
        name="inproj_rest",
    )(hn, w_rest)


def _ktrans_kernel(w_ref, h_ref, o_ref):
    kt = lax.dot_general(w_ref[...], h_ref[...], (((1,), (1,)), ((), ())),
                         preferred_element_type=F32)
    o_ref[...] = (kt * (ML_DIM ** -0.5)).astype(BF16)


def _ktrans(hn, w_kt):
    n = hn.shape[0]
    tm = TM_PROJ
    return pl.pallas_call(
        _ktrans_kernel,
        grid=(n // tm,),
        in_specs=[pl.BlockSpec((ML_HEADS * ML_DIM, D_MODEL), lambda i: (0, 0)),
                  pl.BlockSpec((tm, D_MODEL), lambda i: (i, 0))],
        out_specs=pl.BlockSpec((ML_HEADS * ML_DIM, tm), lambda i: (0, i)),
        out_shape=jax.ShapeDtypeStruct((ML_HEADS * ML_DIM, n), BF16),
        compiler_params=_cparams(("parallel",)),
        name="ml_k_transposed",
    )(w_kt, hn)


def _log_sigmoid(x):
    return jnp.minimum(x, 0.0) - jnp.log1p(jnp.exp(-jnp.abs(x)))


def _split3(x):
    hi = x.astype(BF16)
    r = x - hi.astype(F32)
    mid = r.astype(BF16)
    lo = (r - mid.astype(F32)).astype(BF16)
    return hi, mid, lo


def _gates_kernel(h_ref, w_ref, wt_ref, b_ref, bt_ref, g_ref, gt_ref):
    h = h_ref[...]
    g = jnp.dot(h, w_ref[...], preferred_element_type=F32) + b_ref[...]
    gt = lax.dot_general(wt_ref[...], h, (((1,), (1,)), ((), ())),
                         preferred_element_type=F32) + bt_ref[...]
    L = L_MLSTM
    n_in = 2 * ML_HEADS
    n_fw = 3 * ML_HEADS
    lane = lax.broadcasted_iota(I32, (L, LANES), 1)
    sub = lax.broadcasted_iota(I32, (N_GATES, L), 0)
    ri = lax.broadcasted_iota(I32, (L, L), 0)
    ci = lax.broadcasted_iota(I32, (L, L), 1)
    lower = jnp.where(ci <= ri, 1.0, 0.0).astype(BF16)
    upper = jnp.where(ci >= ri, 1.0, 0.0).astype(BF16)
    nt = (((1,), (1,)), ((), ()))
    for c in range(g.shape[0] // L):
        gc = g[c * L:(c + 1) * L, :]
        pre = jnp.zeros((L, LANES), F32)
        suf = jnp.zeros((L, LANES), F32)
        for piece in _split3(jnp.where(lane >= n_in, _log_sigmoid(gc), 0.0)):
            pre = pre + jnp.dot(lower, piece, preferred_element_type=F32)
            suf = suf + jnp.dot(upper, piece, preferred_element_type=F32)
        g_ref[c * L:(c + 1) * L, :] = jnp.where(lane < n_in, gc, jnp.where(lane < n_fw, pre, suf))

        gtc = gt[:, c * L:(c + 1) * L]
        pre = jnp.zeros((N_GATES, L), F32)
        suf = jnp.zeros((N_GATES, L), F32)
        for piece in _split3(jnp.where(sub >= n_in, _log_sigmoid(gtc), 0.0)):
            pre = pre + lax.dot_general(piece, lower, nt, preferred_element_type=F32)
            suf = suf + lax.dot_general(piece, upper, nt, preferred_element_type=F32)
        gt_ref[:, c * L:(c + 1) * L] = jnp.where(sub < n_in, gtc, jnp.where(sub < n_fw, pre, suf))


def _gates(hn, w_g, w_gt, b_g, b_gt):
    n = hn.shape[0]
    tm = TM_PROJ
    return pl.pallas_call(
        _gates_kernel,
        grid=(n // tm,),
        in_specs=[pl.BlockSpec((tm, D_MODEL), lambda i: (i, 0)),
                  pl.BlockSpec((D_MODEL, LANES), lambda i: (0, 0)),
                  pl.BlockSpec((N_GATES, D_MODEL), lambda i: (0, 0)),
                  pl.BlockSpec((1, LANES), lambda i: (0, 0)),
                  pl.BlockSpec((N_GATES, 1), lambda i: (0, 0))],
        out_specs=[pl.BlockSpec((tm, LANES), lambda i: (i, 0)),
                   pl.BlockSpec((N_GATES, tm), lambda i: (0, i))],
        out_shape=[jax.ShapeDtypeStruct((n, LANES), F32),
                   jax.ShapeDtypeStruct((N_GATES, n), F32)],
        compiler_params=_cparams(("parallel",)),
        name="ml_gates",
    )(hn, w_g, w_gt, b_g, b_gt)


def _diffattn_kernel(q_ref, k_ref, v_ref, lamv_ref, g_ref, o_ref, va_ref, *, seq, lam_init):
    tq = q_ref.shape[0]
    tk = min(TK_ATT, seq)
    n_chunks = seq // tk
    unroll = min(n_chunks, ATT_UNROLL)

    @pl.when(pl.program_id(2) == 0)
    def _():
        va_ref[:, :DA_HEAD_W] = v_ref[...]
        va_ref[:, DA_HEAD_W:] = jnp.ones((seq, DA_HEAD_W), BF16)

    q = q_ref[...]
    lane = lax.broadcasted_iota(I32, q.shape, 1)
    zero = jnp.zeros_like(q)
    in_map0 = ((lane // (DA_DIM // 2)) % 2) == 0
    qs = (jnp.where(in_map0, q, zero), jnp.where(in_map0, zero, q))

    def chunk(start, carry):
        kc = k_ref[pl.ds(start, tk), :]
        vc = va_ref[pl.ds(start, tk), :]
        out = []
        for qm, (m, l, a) in zip(qs, carry):
            s = lax.dot_general(qm, kc, (((1,), (1,)), ((), ())), preferred_element_type=F32)
            mn = jnp.maximum(m, jnp.max(s, axis=-1, keepdims=True))
            alpha = jnp.exp2(m - mn)
            p = jnp.exp2((s - mn).astype(BF16))
            pv = jnp.dot(p, vc, preferred_element_type=F32)
            out.append((mn, alpha * l + pv[:, DA_HEAD_W:],
                        alpha * a + pv[:, :DA_HEAD_W]))
        return tuple(out)

    def body(c, carry):
        for u in range(unroll):
            carry = chunk(pl.multiple_of((c * unroll + u) * tk, tk), carry)
        return carry

    init = (jnp.full((tq, 1), -jnp.inf, F32), jnp.zeros((tq, DA_HEAD_W), F32),
            jnp.zeros((tq, DA_HEAD_W), F32))
    (m0, l0, a0), (m1, l1, a1) = lax.fori_loop(0, n_chunks // unroll, body, (init, init))

    lv = lamv_ref[...]
    lam = (jnp.exp(jnp.sum(lv[0:1, :] * lv[1:2, :], axis=-1, keepdims=True))
           - jnp.exp(jnp.sum(lv[2:3, :] * lv[3:4, :], axis=-1, keepdims=True)) + lam_init)
    o = a0 * (1.0 / l0) - lam * (a1 * (1.0 / l1))
    o_ref[...] = (_rms(o, g_ref[...]) * (1.0 - lam_init)).astype(BF16)


def _diffattn(zqk, z, lamv, g_da, batch, seq, lam_init):
    n = z.shape[0]
    tq = min(TQ_ATT, seq)
    nq = seq // tq
    hq = QK_Q // DA_HEAD_W
    hk = QK_K // DA_HEAD_W
    hv = Z_DA_V // DA_HEAD_W
    return pl.pallas_call(
        functools.partial(_diffattn_kernel, seq=seq, lam_init=lam_init),
        grid=(batch, DA_HEADS, nq),
        in_specs=[
            pl.BlockSpec((tq, DA_HEAD_W), lambda b, h, i: (b * nq + i, hq + h)),
            pl.BlockSpec((seq, DA_HEAD_W), lambda b, h, i: (b, hk + h)),
            pl.BlockSpec((seq, DA_HEAD_W), lambda b, h, i: (b, hv + h)),
            pl.BlockSpec((4, DA_DIM), lambda b, h, i: (0, 0)),
            pl.BlockSpec((1, DA_HEAD_W), lambda b, h, i: (0, 0)),
        ],
        out_specs=pl.BlockSpec((tq, DA_HEAD_W), lambda b, h, i: (b * nq + i, h)),
        out_shape=jax.ShapeDtypeStruct((n, DA_HEADS * DA_HEAD_W), BF16),
        scratch_shapes=[pltpu.VMEM((seq, 2 * DA_HEAD_W), BF16)],
        compiler_params=_cparams(("parallel", "parallel", "arbitrary")),
        name="diff_attention",
    )(zqk, zqk, z, lamv, g_da)


def _mlstm_chunk(q, kt, v, g, gt, cta_ref, m_ref, h_ref, head, reverse):
    L = q.shape[0]
    col_i = head + (ML_HEADS if reverse else 0)
    col_f = col_i + 2 * ML_HEADS
    lane = lax.broadcasted_iota(I32, g.shape, 1)
    sub = lax.broadcasted_iota(I32, gt.shape, 0)
    i_col = jnp.sum(jnp.where(lane == col_i, g, 0.0), axis=1, keepdims=True)
    g_col = jnp.sum(jnp.where(lane == col_f, g, 0.0), axis=1, keepdims=True)
    i_row = jnp.sum(jnp.where(sub == col_i, gt, 0.0), axis=0, keepdims=True)
    g_row = jnp.sum(jnp.where(sub == col_f, gt, 0.0), axis=0, keepdims=True)

    ri = lax.broadcasted_iota(I32, (L, L), 0)
    ci = lax.broadcasted_iota(I32, (L, L), 1)
    tri = (ci >= ri) if reverse else (ci <= ri)
    last = 0 if reverse else L - 1
    lane_l = lax.broadcasted_iota(I32, (1, L), 1)
    g_last = jnp.sum(jnp.where(lane_l == last, g_row, 0.0), axis=1, keepdims=True)

    m_prev = m_ref[...]
    dmat = jnp.where(tri, g_col + (i_row - g_row), -jnp.inf)
    a_col = g_col + m_prev
    mj = jnp.maximum(a_col, jnp.max(dmat, axis=1, keepdims=True))
    w_inter = jnp.exp(a_col - mj)
    s = jnp.dot(q, kt, preferred_element_type=F32) * jnp.exp(dmat - mj)
    ones = jnp.ones((L, LANES), BF16)
    cta = cta_ref[...]
    num = (w_inter * jnp.dot(q, cta.astype(BF16), preferred_element_type=F32)
           + jnp.dot(s.astype(BF16), jnp.concatenate([v, ones], axis=1), preferred_element_type=F32))
    inv = 1.0 / jnp.maximum(jnp.abs(num[:, ML_DIM:]), jnp.exp(-mj))
    h_ref[...] = (num[:, :ML_DIM] * jnp.concatenate([inv] * (ML_DIM // LANES), axis=1)).astype(BF16)

    dec_row = g_last - g_row + i_row
    dec_col = g_last - g_col + i_col
    m_new = jnp.maximum(g_last + m_prev, jnp.max(dec_row, axis=1, keepdims=True))
    wk_col = jnp.exp(dec_col - m_new)
    carry_scale = jnp.exp(g_last + m_prev - m_new)
    wv = jnp.concatenate([(wk_col * v.astype(F32)).astype(BF16),
                          jnp.broadcast_to(wk_col, (L, LANES)).astype(BF16)], axis=1)
    cta_ref[...] = carry_scale * cta + jnp.dot(kt, wv, preferred_element_type=F32)
    m_ref[...] = m_new


def _mlstm_kernel(qf_ref, ktf_ref, vf_ref, gf_ref, gtf_ref, qb_ref, ktb_ref, vb_ref, gb_ref, gtb_ref,
                  hf_ref, hb_ref, ctf_ref, mf_ref, ctb_ref, mb_ref):
    @pl.when(pl.program_id(1) == 0)
    def _():
        ctf_ref[...] = jnp.zeros_like(ctf_ref)
        mf_ref[...] = jnp.zeros_like(mf_ref)
        ctb_ref[...] = jnp.zeros_like(ctb_ref)
        mb_ref[...] = jnp.zeros_like(mb_ref)

    for head in range(ML_HEADS):
        sl = slice(head * ML_DIM, (head + 1) * ML_DIM)
        _mlstm_chunk(qf_ref[:, sl], ktf_ref[sl, :], vf_ref[:, sl], gf_ref[...], gtf_ref[...],
                     ctf_ref.at[head], mf_ref.at[head], hf_ref.at[:, sl], head, False)
        _mlstm_chunk(qb_ref[:, sl], ktb_ref[sl, :], vb_ref[:, sl], gb_ref[...], gtb_ref[...],
                     ctb_ref.at[head], mb_ref.at[head], hb_ref.at[:, sl], head, True)


def _mlstm(z, kt, gates, gates_t, batch, seq):
    n = z.shape[0]
    L = L_MLSTM
    nc = seq // L
    width = ML_HEADS * ML_DIM
    cq = Z_ML_Q // width
    cv = Z_ML_V // width

    def specs(tix):
        return [
            pl.BlockSpec((L, width), lambda b, t: (b * nc + tix(t), cq)),
            pl.BlockSpec((width, L), lambda b, t: (0, b * nc + tix(t))),
            pl.BlockSpec((L, width), lambda b, t: (b * nc + tix(t), cv)),
            pl.BlockSpec((L, LANES), lambda b, t: (b * nc + tix(t), 0)),
            pl.BlockSpec((N_GATES, L), lambda b, t: (0, b * nc + tix(t))),
        ]

    fwd = lambda t: t
    bwd = lambda t: nc - 1 - t
    out = jax.ShapeDtypeStruct((n, width), BF16)
    state = [pltpu.VMEM((ML_HEADS, ML_DIM, ML_DIM + LANES), F32), pltpu.VMEM((ML_HEADS, 1, 1), F32)]
    return pl.pallas_call(
        _mlstm_kernel,
        grid=(batch, nc),
        in_specs=specs(fwd) + specs(bwd),
        out_specs=[pl.BlockSpec((L, width), lambda b, t: (b * nc + fwd(t), 0)),
                   pl.BlockSpec((L, width), lambda b, t: (b * nc + bwd(t), 0))],
        out_shape=[out, out],
        scratch_shapes=state + state,
        compiler_params=_cparams(("parallel", "arbitrary")),
        name="mlstm",
    )(z, kt, z, gates, gates_t, z, kt, z, gates, gates_t)


def _memkv_kernel(m_ref, g_ref, w_ref, o_ref):
    mn = _rms(m_ref[...], g_ref[...]).astype(BF16)
    o_ref[...] = jnp.dot(mn, w_ref[...], preferred_element_type=F32).astype(BF16)


def _memkv(mem2d, g_mem, w_xkv, mem_tokens):
    n = mem2d.shape[0]
    return pl.pallas_call(
        _memkv_kernel,
        grid=(n // mem_tokens,),
        in_specs=[pl.BlockSpec((mem_tokens, D_MODEL), lambda b: (b, 0)),
                  pl.BlockSpec((1, D_MODEL), lambda b: (0, 0)),
                  pl.BlockSpec((D_MODEL, 2 * D_MODEL), lambda b: (0, 0))],
        out_specs=pl.BlockSpec((mem_tokens, 2 * D_MODEL), lambda b: (b, 0)),
        out_shape=jax.ShapeDtypeStruct((n, 2 * D_MODEL), BF16),
        compiler_params=_cparams(("parallel",)),
        name="mem_kv",
    )(mem2d, g_mem, w_xkv)


def _sigmoid(x):
    return 1.0 / (1.0 + jnp.exp(-x))


def _mix_kernel(x_ref, oa_ref, hf_ref, hb_ref, mo_ref, bga_ref, bgb_ref, kv_ref,
                gml_ref, gxa_ref, gffn_ref, wpa_ref, wpb_ref, wo_ref, wxq_ref, wxo_ref, wrt_ref,
                x2_ref, hn_ref, afft_ref):
    hsum = hf_ref[...].astype(F32) + hb_ref[...].astype(F32)
    gml = gml_ref[...]
    parts = []
    for hh in range(ML_HEADS):
        sl = slice(hh * ML_DIM, (hh + 1) * ML_DIM)
        parts.append(_rms(hsum[:, sl], gml[:, sl]))
    hb = jnp.concatenate(parts, axis=1)
    hb = hb * _sigmoid(mo_ref[...].astype(F32))
    y_a = jnp.dot(oa_ref[...], wpa_ref[...], preferred_element_type=F32)
    y_b = jnp.dot(hb.astype(BF16), wpb_ref[...], preferred_element_type=F32)
    merged = (_sigmoid(bga_ref[...].astype(F32)) * y_a
              + _sigmoid(bgb_ref[...].astype(F32)) * y_b)
    x1 = x_ref[...] + jnp.dot(merged.astype(BF16), wo_ref[...], preferred_element_type=F32)

    hq = _rms(x1, gxa_ref[...]).astype(BF16)
    qx = (jnp.dot(hq, wxq_ref[...], preferred_element_type=F32) * (XA_DIM ** -0.5)).astype(BF16)
    kv = kv_ref[...]
    outs = []
    for hh in range(XA_HEADS):
        sl = slice(hh * XA_DIM, (hh + 1) * XA_DIM)
        kh = kv[:, sl]
        vh = kv[:, D_MODEL + hh * XA_DIM:D_MODEL + (hh + 1) * XA_DIM]
        s = lax.dot_general(qx[:, sl], kh, (((1,), (1,)), ((), ())), preferred_element_type=F32)
        p = jnp.exp(s - jnp.max(s, axis=-1, keepdims=True))
        p = p / jnp.sum(p, axis=-1, keepdims=True)
        outs.append(jnp.dot(p.astype(BF16), vh, preferred_element_type=F32).astype(BF16))
    o = jnp.concatenate(outs, axis=1)
    x2 = x1 + jnp.dot(o, wxo_ref[...], preferred_element_type=F32)
    x2_ref[...] = x2

    hn = _rms(x2, gffn_ref[...])
    hn_ref[...] = hn.astype(BF16)
    logits_t = lax.dot_general(wrt_ref[...], hn, (((1,), (1,)), ((), ())),
                               precision=lax.Precision.HIGHEST, preferred_element_type=F32)
    e = jnp.exp(logits_t - jnp.max(logits_t, axis=0, keepdims=True))
    afft_ref[...] = e / jnp.sum(e, axis=0, keepdims=True)


def _mix(x2d, oa, h_fw, h_bw, z, kv, p, batch, seq, mem_tokens):
    n = x2d.shape[0]
    tm = TM_MIX
    nt = seq // tm
    cmo = Z_ML_O // D_MODEL
    cbg = Z_BR_G // D_MODEL
    full = lambda r, c: pl.BlockSpec((r, c), lambda i: (0, 0), pipeline_mode=pl.Buffered(1))
    return pl.pallas_call(
        _mix_kernel,
        grid=(n // tm,),
        in_specs=[
            pl.BlockSpec((tm, D_MODEL), lambda i: (i, 0)),
            pl.BlockSpec((tm, D_MODEL), lambda i: (i, 0)),
            pl.BlockSpec((tm, D_MODEL), lambda i: (i, 0)),
            pl.BlockSpec((tm, D_MODEL), lambda i: (i, 0)),
            pl.BlockSpec((tm, D_MODEL), lambda i: (i, cmo)),
            pl.BlockSpec((tm, D_MODEL), lambda i: (i, cbg)),
            pl.BlockSpec((tm, D_MODEL), lambda i: (i, cbg + 1)),
            pl.BlockSpec((mem_tokens, 2 * D_MODEL), lambda i: (i // nt, 0)),
            full(1, D_MODEL), full(1, D_MODEL), full(1, D_MODEL),
            full(D_MODEL, D_MODEL), full(D_MODEL, D_MODEL), full(D_MODEL, D_MODEL),
            full(D_MODEL, D_MODEL), full(D_MODEL, D_MODEL),
            full(N_EXPERTS, D_MODEL),
        ],
        out_specs=[pl.BlockSpec((tm, D_MODEL), lambda i: (i, 0)),
                   pl.BlockSpec((tm, D_MODEL), lambda i: (i, 0)),
                   pl.BlockSpec((N_EXPERTS, tm), lambda i: (0, i))],
        out_shape=[jax.ShapeDtypeStruct((n, D_MODEL), F32),
                   jax.ShapeDtypeStruct((n, D_MODEL), BF16),
                   jax.ShapeDtypeStruct((N_EXPERTS, n), F32)],
        compiler_params=_cparams(("parallel",)),
        name="mix_xattn_router",
    )(x2d, oa, h_fw, h_bw, z, z, z, kv,
      p["g_ml_out"], p["g_xa"], p["g_ffn"], p["w_proj_a"], p["w_proj_b"], p["w_out"],
      p["w_xq"], p["w_xo"], p["w_router_t"])


def _prefix_incl(mask_f, upper, ones, lower):
    mb = mask_f.astype(BF16)
    lane_incl = jnp.dot(mb, upper, preferred_element_type=F32)
    row_tot = jnp.dot(mb, ones, preferred_element_type=F32)
    row_off = jnp.dot(lower, row_tot.astype(BF16), preferred_element_type=F32)
    return lane_incl + row_off, row_off


def _select_kernel(a_ref, pos_ref, off_ref, *, cap):
    bits = pltpu.bitcast(a_ref[...], I32)
    rows = bits.shape[1]

    def body(i, thr):
        cand = thr | jnp.left_shift(jnp.int32(1), 30 - i)
        cnt = jnp.sum(jnp.where(bits >= cand, 1.0, 0.0), axis=(1, 2), keepdims=True)
        return jnp.where(cnt >= cap, cand, thr)

    thr_all = lax.fori_loop(0, 31, body, jnp.zeros((N_EXPERTS, 1, 1), I32))

    li = lax.broadcasted_iota(I32, (LANES, LANES), 0)
    lj = lax.broadcasted_iota(I32, (LANES, LANES), 1)
    upper = (li <= lj).astype(BF16)
    ones = jnp.ones((LANES, LANES), BF16)
    ri = lax.broadcasted_iota(I32, (rows, rows), 0)
    rj = lax.broadcasted_iota(I32, (rows, rows), 1)
    lower = (rj < ri).astype(BF16)

    for e in range(N_EXPERTS):
        be = bits[e]
        thr = thr_all[e]
        gt = be > thr
        eq = be == thr
        need = cap - jnp.sum(jnp.where(gt, 1.0, 0.0), keepdims=True)
        eq_rank, _ = _prefix_incl(jnp.where(eq, 1.0, 0.0), upper, ones, lower)
        sel = gt | (eq & (eq_rank <= need))
        incl, row_off = _prefix_incl(jnp.where(sel, 1.0, 0.0), upper, ones, lower)
        pos_ref[e] = jnp.where(sel, incl - 1.0, -1.0).astype(I32)
        off_ref[e] = row_off.astype(I32)


def _select(aff_t, n_tokens, cap):
    rows = n_tokens // LANES
    a3 = aff_t.reshape(N_EXPERTS, rows, LANES)
    spec = pl.BlockSpec((N_EXPERTS, rows, LANES), lambda i: (0, 0, 0))
    return pl.pallas_call(
        functools.partial(_select_kernel, cap=float(cap)),
        grid=(1,),
        in_specs=[spec],
        out_specs=[spec, spec],
        out_shape=[jax.ShapeDtypeStruct((N_EXPERTS, rows, LANES), I32),
                   jax.ShapeDtypeStruct((N_EXPERTS, rows, LANES), I32)],
        compiler_params=_cparams(("arbitrary",)),
        name="expert_choice_select",
    )(a3)


def _dispatch_kernel(s_ref, x_ref, pos_ref, xe_ref, buf_ref, stage_ref, sem_ref, *, n_blocks):
    t = pl.program_id(0)
    shift = B_SLOT.bit_length() - 1
    x = x_ref[...]
    tt = x.shape[0]
    slot = lax.broadcasted_iota(I32, (B_SLOT, tt), 0)
    row = lax.broadcasted_iota(I32, (B_SLOT, 1), 0)

    def block_copy(e, par, blk):
        first = blk * B_SLOT
        if not isinstance(first, int):
            first = pl.multiple_of(first, B_SLOT)
        return pltpu.make_async_copy(buf_ref.at[e, par], xe_ref.at[e, pl.ds(first, B_SLOT)],
                                     sem_ref.at[e, par])

    def gather_rows(e, blk):
        rel = pos_ref[e:e + 1, :] - blk * B_SLOT
        onehot = jnp.where(slot == rel, 1.0, 0.0).astype(BF16)
        return jnp.dot(onehot, x, preferred_element_type=F32).astype(BF16)

    for e in range(N_EXPERTS):
        stage_ref[e] = gather_rows(e, lax.shift_right_logical(s_ref[t * N_EXPERTS + e], shift))

    for e in range(N_EXPERTS):
        s0 = s_ref[t * N_EXPERTS + e]
        s1 = s_ref[(t + 1) * N_EXPERTS + e]
        b0 = lax.shift_right_logical(s0, shift)
        r0 = s0 - b0 * B_SLOT
        par = b0 & 1

        @pl.when(s1 > s0)
        def _(e=e, s1=s1, b0=b0, r0=r0, par=par):
            @pl.when(r0 == 0)
            def _():
                @pl.when(b0 >= 2)
                def _():
                    block_copy(e, par, b0 - 2).wait()
                buf_ref[e, par] = stage_ref[e]

            @pl.when(r0 != 0)
            def _():
                mine = (row >= r0) & (row < s1 - b0 * B_SLOT)
                buf_ref[e, par] = jnp.where(mine, stage_ref[e], buf_ref[e, par])

            @pl.when(s1 > (b0 + 1) * B_SLOT)
            def _():
                @pl.when(b0 >= 1)
                def _():
                    block_copy(e, 1 - par, b0 - 1).wait()
                buf_ref[e, 1 - par] = gather_rows(e, b0 + 1)

            @pl.when(s1 >= (b0 + 1) * B_SLOT)
            def _():
                block_copy(e, par, b0).start()

    @pl.when(t == pl.num_programs(0) - 1)
    def _():
        for e in range(N_EXPERTS):
            for blk in range(max(n_blocks - 2, 0), n_blocks):
                block_copy(e, blk & 1, blk).wait()


def _dispatch(hn, pos_t, slot_starts, cap):
    n = hn.shape[0]
    tt = T_DISP
    n_blocks = cap // B_SLOT
    grid_spec = pltpu.PrefetchScalarGridSpec(
        num_scalar_prefetch=1,
        grid=(n // tt,),
        in_specs=[pl.BlockSpec((tt, D_MODEL), lambda t, s: (t, 0)),
                  pl.BlockSpec((N_EXPERTS, tt), lambda t, s: (0, t))],
        out_specs=pl.BlockSpec(memory_space=pl.ANY),
        scratch_shapes=[pltpu.VMEM((N_EXPERTS, 2, B_SLOT, D_MODEL), BF16),
                        pltpu.VMEM((N_EXPERTS, B_SLOT, D_MODEL), BF16),
                        pltpu.SemaphoreType.DMA((N_EXPERTS, 2))],
    )
    return pl.pallas_call(
        functools.partial(_dispatch_kernel, n_blocks=n_blocks),
        grid_spec=grid_spec,
        out_shape=jax.ShapeDtypeStruct((N_EXPERTS, cap, D_MODEL), BF16),
        compiler_params=_cparams(("arbitrary",)),
        name="dispatch",
    )(slot_starts, hn, pos_t)


def _slot_starts(row_off, tile, cap):
    start = row_off[:, ::tile // LANES, 0]
    start = jnp.concatenate([start, jnp.full((N_EXPERTS, 1), cap, I32)], axis=1)
    return start.T.reshape(-1).astype(I32)


def _ffn_kernel(x_ref, wg_ref, wu_ref, wd_ref, o_ref):
    x = x_ref[...]
    acc = jnp.zeros((x.shape[0], D_MODEL), F32)
    for c in range(D_EXPERT // F_CHUNK):
        sl = slice(c * F_CHUNK, (c + 1) * F_CHUNK)
        gte = jnp.dot(x, wg_ref[:, sl], preferred_element_type=F32)
        up = jnp.dot(x, wu_ref[:, sl], preferred_element_type=F32)
        hid = gte * (1.0 / (1.0 + jnp.exp(-gte))) * up
        acc = acc + jnp.dot(hid.astype(BF16), wd_ref[sl, :], preferred_element_type=F32)
    o_ref[...] = acc.astype(BF16)


def _ffn(xe, w_gate, w_up, w_down, cap):
    tm = min(TM_FFN, cap)
    return pl.pallas_call(
        _ffn_kernel,
        grid=(N_EXPERTS, cap // tm),
        in_specs=[pl.BlockSpec((None, tm, D_MODEL), lambda e, i: (e, i, 0)),
                  pl.BlockSpec((None, D_MODEL, D_EXPERT), lambda e, i: (e, 0, 0)),
                  pl.BlockSpec((None, D_MODEL, D_EXPERT), lambda e, i: (e, 0, 0)),
                  pl.BlockSpec((None, D_EXPERT, D_MODEL), lambda e, i: (e, 0, 0))],
        out_specs=pl.BlockSpec((None, tm, D_MODEL), lambda e, i: (e, i, 0)),
        out_shape=jax.ShapeDtypeStruct((N_EXPERTS, cap, D_MODEL), BF16),
        compiler_params=_cparams(("parallel", "arbitrary")),
        name="expert_ffn",
    )(xe, w_gate, w_up, w_down)


def _combine_kernel(b0_ref, st_ref, x_ref, pos_ref, aff_ref, g_ref, *rest):
    ye_refs = rest[:2 * N_EXPERTS]
    o_ref = rest[2 * N_EXPERTS]
    t = pl.program_id(0)
    tt = pos_ref.shape[0]
    slot = lax.broadcasted_iota(I32, (tt, B_SLOT), 1)

    def gated_onehot(e, kk):
        rel = pos_ref[:, e:e + 1] - (b0_ref[t * N_EXPERTS + e] + kk) * B_SLOT
        return jnp.where(slot == rel, aff_ref[:, e:e + 1], 0.0).astype(BF16)

    onehots = [gated_onehot(e, 0) for e in range(N_EXPERTS)]
    for c in range(D_MODEL // COMB_STRIP):
        sl = slice(c * COMB_STRIP, (c + 1) * COMB_STRIP)
        acc = x_ref[:, sl]
        for e in range(N_EXPERTS):
            acc = acc + jnp.dot(onehots[e], ye_refs[2 * e][:, sl], preferred_element_type=F32)
        o_ref[:, sl] = acc
    for e in range(N_EXPERTS):
        @pl.when(st_ref[t * N_EXPERTS + e] != 0)
        def _(e=e):
            o_ref[...] += jnp.dot(gated_onehot(e, 1), ye_refs[2 * e + 1][...],
                                  preferred_element_type=F32)
    o_ref[...] = _rms(o_ref[...], g_ref[...])


def _combine(x2, pos_c, aff_c, ye, b0, straddle, g_final, cap):
    n = x2.shape[0]
    tt = T_COMB
    n_blocks = cap // B_SLOT

    def ye_spec(e, kk):
        return pl.BlockSpec(
            (None, B_SLOT, D_MODEL),
            lambda t, b0r, st: (e, jnp.minimum(b0r[t * N_EXPERTS + e] + kk, n_blocks - 1), 0))

    ye_specs = [ye_spec(e, kk) for e in range(N_EXPERTS) for kk in range(2)]
    grid_spec = pltpu.PrefetchScalarGridSpec(
        num_scalar_prefetch=2,
        grid=(n // tt,),
        in_specs=[pl.BlockSpec((tt, D_MODEL), lambda t, b0r, st: (t, 0)),
                  pl.BlockSpec((tt, N_EXPERTS), lambda t, b0r, st: (t, 0)),
                  pl.BlockSpec((tt, N_EXPERTS), lambda t, b0r, st: (t, 0)),
                  pl.BlockSpec((1, D_MODEL), lambda t, b0r, st: (0, 0))] + ye_specs,
        out_specs=pl.BlockSpec((tt, D_MODEL), lambda t, b0r, st: (t, 0)),
    )
    return pl.pallas_call(
        _combine_kernel,
        grid_spec=grid_spec,
        out_shape=jax.ShapeDtypeStruct((n, D_MODEL), F32),
        compiler_params=_cparams(("arbitrary",)),
        name="combine_final_norm",
    )(b0, straddle, x2, pos_c, aff_c, g_final, *([ye] * (2 * N_EXPERTS)))


def _encoder(x, mem, p):
    batch, seq, _ = x.shape
    mem_tokens = mem.shape[1]
    n = batch * seq
    cap = CAPACITY_FACTOR * n // N_EXPERTS
    x2d = x.reshape(n, D_MODEL)

    half = DA_DIM // 2
    pos = jnp.arange(seq, dtype=F32)
    inv = ROPE_THETA ** (-jnp.arange(half, dtype=F32) / half)
    ang = pos[:, None] * inv[None, :]
    cos = jnp.cos(ang)
    sin = jnp.sin(ang)
    cos_t = jnp.tile(cos, (1, LANES // half))
    sin_t = jnp.concatenate([jnp.tile(-sin, (1, 2)), jnp.tile(sin, (1, 2))], axis=1)

    zqk, hn = _qk_proj(x2d, p["g_mix"], p["w_qk"], cos_t, sin_t, seq)
    z = _proj(hn, p["w_rest"])
    kt = _ktrans(hn, p["w_kt"])
    gates, gates_t = _gates(hn, p["w_g"], p["w_gt"], p["b_g"], p["b_gt"])

    oa = _diffattn(zqk, z, p["lamv"], p["g_da_out"], batch, seq, p["lam_init"])
    h_fw, h_bw = _mlstm(z, kt, gates, gates_t, batch, seq)

    kv = _memkv(mem.reshape(batch * mem_tokens, D_MODEL), p["g_mem"], p["w_xkv"], mem_tokens)
    x2, hffn, aff_t = _mix(x2d, oa, h_fw, h_bw, z, kv, p, batch, seq, mem_tokens)

    pos3, off3 = _select(aff_t, n, cap)
    pos_t = pos3.reshape(N_EXPERTS, n)
    xe = _dispatch(hffn, pos_t, _slot_starts(off3, T_DISP, cap), cap)
    ye = _ffn(xe, p["w_e_gate"], p["w_e_up"], p["w_e_down"], cap)

    n_blocks = cap // B_SLOT
    start = off3[:, ::T_COMB // LANES, 0]
    end = jnp.concatenate([start[:, 1:], jnp.full((N_EXPERTS, 1), cap, I32)], axis=1)
    b0 = jnp.minimum(start // B_SLOT, n_blocks - 1)
    straddle = (end > start) & ((end - 1) // B_SLOT > b0)
    y = _combine(x2, pos_t.T, aff_t.T, ye, b0.T.reshape(-1).astype(I32),
                 straddle.T.reshape(-1).astype(I32), p["g_final"], cap)
    return y.reshape(batch, seq, D_MODEL)


def kernel(x_prompt, x_sample, mem_prompt, mem_sample, g_mix, w_in, lam_q1, lam_k1, lam_q2, lam_k2,
           g_da_out, b_ml_gates, g_ml_out, w_proj_a, w_proj_b, w_out, g_xa, g_mem, w_xq, w_xkv, w_xo,
           g_ffn, w_router, w_e_gate, w_e_up, w_e_down, g_final):
    l = 0
    w = w_in[l]
    assert w_in.shape[0] == 1, "single-layer encoder"
    gate_lo = W_GATES
    gate_hi = gate_lo + N_GATES
    lane = jnp.arange(LANES)
    part = lane // (DA_DIM // 2)
    in_head = (part % 2) * DA_DIM + (part // 2) * (DA_DIM // 2) + lane % (DA_DIM // 2)
    qk_cols = (jnp.arange(QK_WIDTH // LANES)[:, None] * LANES + in_head[None, :]).reshape(-1)
    w_g = w[:, gate_lo:gate_hi]
    b = b_ml_gates[l].astype(F32)
    p = {
        "g_mix": g_mix[l][None, :],
        "w_qk": w[:, :W_DA_V][:, qk_cols].astype(BF16),
        "w_rest": jnp.concatenate([w[:, W_DA_V:W_ML_K], w[:, W_ML_V:gate_lo], w[:, gate_hi:]],
                                  axis=1).astype(BF16),
        "w_kt": w[:, W_ML_K:W_ML_V].T.astype(BF16),
        "w_g": jnp.pad(w_g, ((0, 0), (0, LANES - N_GATES))).astype(BF16),
        "w_gt": w_g.T.astype(BF16),
        "b_g": jnp.pad(b, (0, LANES - N_GATES))[None, :],
        "b_gt": b[:, None],
        "lamv": jnp.stack([lam_q1[l], lam_k1[l], lam_q2[l], lam_k2[l]]).astype(F32),
        "lam_init": 0.8 - 0.6 * math.exp(-0.3 * l),
        "g_da_out": g_da_out[l][None, :],
        "g_ml_out": g_ml_out[l][None, :],
        "w_proj_a": w_proj_a[l].astype(BF16),
        "w_proj_b": w_proj_b[l].astype(BF16),
        "w_out": w_out[l].astype(BF16),
        "g_xa": g_xa[l][None, :],
        "g_mem": g_mem[l][None, :],
        "w_xq": w_xq[l].astype(BF16),
        "w_xkv": w_xkv[l].astype(BF16),
        "w_xo": w_xo[l].astype(BF16),
        "g_ffn": g_ffn[l][None, :],
        "w_router_t": w_router[l].T.astype(F32),
        "w_e_gate": w_e_gate[l].astype(BF16),
        "w_e_up": w_e_up[l].astype(BF16),
        "w_e_down": w_e_down[l].astype(BF16),
        "g_final": g_final[None, :],
    }
    y_prompt = _encoder(x_prompt, mem_prompt, p)
    y_sample = _encoder(x_sample, mem_sample, p)
    return (y_prompt, y_sample)
```

```python
import functools
import math

import jax
import jax.numpy as jnp
from jax import lax
from jax.experimental import pallas as pl
from jax.experimental.pallas import tpu as pltpu

F32 = jnp.float32
BF16 = jnp.bfloat16
I32 = jnp.int32

D_MODEL = 1024
DA_HEADS = 8
DA_DIM = 64
DA_HEAD_W = 2 * DA_DIM
ML_HEADS = 4
ML_DIM = 256
XA_HEADS = 4
XA_DIM = D_MODEL // XA_HEADS
N_EXPERTS = 16
CAPACITY_FACTOR = 2
D_EXPERT = 2048
EPS = 1e-6
ROPE_THETA = 10000.0
N_GATES = 4 * ML_HEADS

QK_Q = 0
QK_K = 1024
QK_WIDTH = 2048
Z_DA_V = 0
Z_ML_Q = 1024
Z_ML_V = 2048
Z_ML_O = 3072
Z_BR_G = 4096
Z_WIDTH = 6144
W_DA_V = 2048
W_ML_K = 4096
W_ML_V = 5120
W_GATES = 7168

LANES = 128
VMEM_LIMIT = 56 * 1024 * 1024

TM_PROJ = 1024
TN_PROJ = 1024
TN_REST = 2048
TQ_ATT = 2048
TK_ATT = 512
ATT_UNROLL = 8
L_MLSTM = 256
TM_MIX = 512
T_DISP = 256
DISP_BLOCK = 128
DISP_SPAN = 3
DISP_RING = 4
B_SLOT = 256
TM_FFN = 512
F_CHUNK = 512
T_COMB = 256
COMB_STRIP = 256


def _cparams(sem):
    return pltpu.CompilerParams(dimension_semantics=sem, vmem_limit_bytes=VMEM_LIMIT)


def _rms(xf, g):
    return xf * lax.rsqrt(jnp.mean(xf * xf, axis=-1, keepdims=True) + EPS) * g


def _qk_proj_kernel(x_ref, g_ref, w_ref, cos_ref, sin_ref, z_ref, hn_ref, hs_ref, *, n_q_tiles):
    j = pl.program_id(1)

    @pl.when(j == 0)
    def _():
        hb = _rms(x_ref[...], g_ref[...]).astype(BF16)
        hs_ref[...] = hb
        hn_ref[...] = hb

    scale = jnp.where(j < n_q_tiles, DA_DIM ** -0.5 * math.log2(math.e), 1.0)
    acc = jnp.dot(hs_ref[...], w_ref[...], preferred_element_type=F32) * scale
    cos = cos_ref[...]
    sin = sin_ref[...]
    for hh in range(acc.shape[1] // LANES):
        a = acc[:, hh * LANES:(hh + 1) * LANES]
        z_ref[:, hh * LANES:(hh + 1) * LANES] = (a * cos + pltpu.roll(a, LANES // 2, 1) * sin).astype(BF16)


def _qk_proj(x2d, g_mix, w_qk, cos_t, sin_t, seq):
    n = x2d.shape[0]
    tm, tn = TM_PROJ, TN_PROJ
    n_seq_tiles = seq // tm
    return pl.pallas_call(
        functools.partial(_qk_proj_kernel, n_q_tiles=D_MODEL // tn),
        grid=(n // tm, QK_WIDTH // tn),
        in_specs=[
            pl.BlockSpec((tm, D_MODEL), lambda i, j: (i, 0)),
            pl.BlockSpec((1, D_MODEL), lambda i, j: (0, 0)),
            pl.BlockSpec((D_MODEL, tn), lambda i, j: (0, j)),
            pl.BlockSpec((tm, LANES), lambda i, j: (i % n_seq_tiles, 0)),
            pl.BlockSpec((tm, LANES), lambda i, j: (i % n_seq_tiles, 0)),
        ],
        out_specs=[
            pl.BlockSpec((tm, tn), lambda i, j: (i, j)),
            pl.BlockSpec((tm, D_MODEL), lambda i, j: (i, 0)),
        ],
        out_shape=[jax.ShapeDtypeStruct((n, QK_WIDTH), BF16),
                   jax.ShapeDtypeStruct((n, D_MODEL), BF16)],
        scratch_shapes=[pltpu.VMEM((tm, D_MODEL), BF16)],
        compiler_params=_cparams(("parallel", "arbitrary")),
        name="inproj_qk",
    )(x2d, g_mix, w_qk, cos_t, sin_t)


def _proj_kernel(h_ref, w_ref, z_ref):
    z_ref[...] = jnp.dot(h_ref[...], w_ref[...], preferred_element_type=F32).astype(BF16)


def _proj(hn, w_rest):
    n = hn.shape[0]
    tm, tn = TM_PROJ, TN_REST
    return pl.pallas_call(
        _proj_kernel,
        grid=(n // tm, Z_WIDTH // tn),
        in_specs=[pl.BlockSpec((tm, D_MODEL), lambda i, j: (i, 0)),
                  pl.BlockSpec((D_MODEL, tn), lambda i, j: (0, j))],
        out_specs=pl.BlockSpec((tm, tn), lambda i, j: (i, j)),
        out_shape=jax.ShapeDtypeStruct((n, Z_WIDTH), BF16),
        compiler_params=_cparams(("parallel", "arbitrary")),
        name="inproj_rest",
    )(hn, w_rest)


def _ktrans_kernel(w_ref, h_ref, o_ref):
    kt = lax.dot_general(w_ref[...], h_ref[...], (((1,), (1,)), ((), ())),
                         preferred_element_type=F32)
    o_ref[...] = (kt * (ML_DIM ** -0.5)).astype(BF16)


def _ktrans(hn, w_kt):
    n = hn.shape[0]
    tm = TM_PROJ
    return pl.pallas_call(
        _ktrans_kernel,
        grid=(n // tm,),
        in_specs=[pl.BlockSpec((ML_HEADS * ML_DIM, D_MODEL), lambda i: (0, 0)),
                  pl.BlockSpec((tm, D_MODEL), lambda i: (i, 0))],
        out_specs=pl.BlockSpec((ML_HEADS * ML_DIM, tm), lambda i: (0, i)),
        out_shape=jax.ShapeDtypeStruct((ML_HEADS * ML_DIM, n), BF16),
        compiler_params=_cparams(("parallel",)),
        name="ml_k_transposed",
    )(w_kt, hn)


def _log_sigmoid(x):
    return jnp.minimum(x, 0.0) - jnp.log1p(jnp.exp(-jnp.abs(x)))


def _split3(x):
    hi = x.astype(BF16)
    r = x - hi.astype(F32)
    mid = r.astype(BF16)
    lo = (r - mid.astype(F32)).astype(BF16)
    return hi, mid, lo


def _gates_kernel(h_ref, w_ref, wt_ref, b_ref, bt_ref, g_ref, gt_ref):
    h = h_ref[...]
    g = jnp.dot(h, w_ref[...], preferred_element_type=F32) + b_ref[...]
    gt = lax.dot_general(wt_ref[...], h, (((1,), (1,)), ((), ())),
                         preferred_element_type=F32) + bt_ref[...]
    L = L_MLSTM
    n_in = 2 * ML_HEADS
    n_fw = 3 * ML_HEADS
    lane = lax.broadcasted_iota(I32, (L, LANES), 1)
    sub = lax.broadcasted_iota(I32, (N_GATES, L), 0)
    ri = lax.broadcasted_iota(I32, (L, L), 0)
    ci = lax.broadcasted_iota(I32, (L, L), 1)
    lower = jnp.where(ci <= ri, 1.0, 0.0).astype(BF16)
    upper = jnp.where(ci >= ri, 1.0, 0.0).astype(BF16)
    nt = (((1,), (1,)), ((), ()))
    for c in range(g.shape[0] // L):
        gc = g[c * L:(c + 1) * L, :]
        pre = jnp.zeros((L, LANES), F32)
        suf = jnp.zeros((L, LANES), F32)
        for piece in _split3(jnp.where(lane >= n_in, _log_sigmoid(gc), 0.0)):
            pre = pre + jnp.dot(lower, piece, preferred_element_type=F32)
            suf = suf + jnp.dot(upper, piece, preferred_element_type=F32)
        g_ref[c * L:(c + 1) * L, :] = jnp.where(lane < n_in, gc, jnp.where(lane < n_fw, pre, suf))

        gtc = gt[:, c * L:(c + 1) * L]
        pre = jnp.zeros((N_GATES, L), F32)
        suf = jnp.zeros((N_GATES, L), F32)
        for piece in _split3(jnp.where(sub >= n_in, _log_sigmoid(gtc), 0.0)):
            pre = pre + lax.dot_general(piece, lower, nt, preferred_element_type=F32)
            suf = suf + lax.dot_general(piece, upper, nt, preferred_element_type=F32)
        gt_ref[:, c * L:(c + 1) * L] = jnp.where(sub < n_in, gtc, jnp.where(sub < n_fw, pre, suf))


def _gates(hn, w_g, w_gt, b_g, b_gt):
    n = hn.shape[0]
    tm = TM_PROJ
    return pl.pallas_call(
        _gates_kernel,
        grid=(n // tm,),
        in_specs=[pl.BlockSpec((tm, D_MODEL), lambda i: (i, 0)),
                  pl.BlockSpec((D_MODEL, LANES), lambda i: (0, 0)),
                  pl.BlockSpec((N_GATES, D_MODEL), lambda i: (0, 0)),
                  pl.BlockSpec((1, LANES), lambda i: (0, 0)),
                  pl.BlockSpec((N_GATES, 1), lambda i: (0, 0))],
        out_specs=[pl.BlockSpec((tm, LANES), lambda i: (i, 0)),
                   pl.BlockSpec((N_GATES, tm), lambda i: (0, i))],
        out_shape=[jax.ShapeDtypeStruct((n, LANES), F32),
                   jax.ShapeDtypeStruct((N_GATES, n), F32)],
        compiler_params=_cparams(("parallel",)),
        name="ml_gates",
    )(hn, w_g, w_gt, b_g, b_gt)


def _diffattn_kernel(q_ref, k_ref, v_ref, lamv_ref, g_ref, o_ref, va_ref, *, seq, lam_init):
    tq = q_ref.shape[0]
    tk = min(TK_ATT, seq)
    n_chunks = seq // tk
    unroll = min(n_chunks, ATT_UNROLL)

    @pl.when(pl.program_id(2) == 0)
    def _():
        va_ref[:, :DA_HEAD_W] = v_ref[...]
        va_ref[:, DA_HEAD_W:] = jnp.ones((seq, DA_HEAD_W), BF16)

    lv = lamv_ref[...]
    lam = (jnp.exp(jnp.sum(lv[0:1, :] * lv[1:2, :], axis=-1, keepdims=True))
           - jnp.exp(jnp.sum(lv[2:3, :] * lv[3:4, :], axis=-1, keepdims=True)) + lam_init)

    n_sub = 2 if (n_chunks == unroll and tq % 2048 == 0) else 1
    sub = tq // n_sub
    for r in range(n_sub):
        q = q_ref[r * sub:(r + 1) * sub, :]
        lane = lax.broadcasted_iota(I32, q.shape, 1)
        zero = jnp.zeros_like(q)
        in_map0 = ((lane // (DA_DIM // 2)) % 2) == 0
        qs = (jnp.where(in_map0, q, zero), jnp.where(in_map0, zero, q))

        def chunk(start, carry, qs=qs):
            kc = k_ref[pl.ds(start, tk), :]
            vc = va_ref[pl.ds(start, tk), :]
            out = []
            for qm, (m, l, a) in zip(qs, carry):
                s = lax.dot_general(qm, kc, (((1,), (1,)), ((), ())), preferred_element_type=F32)
                mn = jnp.maximum(m, jnp.max(s, axis=-1, keepdims=True))
                alpha = jnp.exp2(m - mn)
                p = jnp.exp2((s - mn).astype(BF16))
                pv = jnp.dot(p, vc, preferred_element_type=F32)
                out.append((mn, alpha * l + pv[:, DA_HEAD_W:],
                            alpha * a + pv[:, :DA_HEAD_W]))
            return tuple(out)

        def body(c, carry, chunk=chunk):
            for u in range(unroll):
                carry = chunk(pl.multiple_of((c * unroll + u) * tk, tk), carry)
            return carry

        init = (jnp.full((sub, 1), -jnp.inf, F32), jnp.zeros((sub, DA_HEAD_W), F32),
                jnp.zeros((sub, DA_HEAD_W), F32))
        (m0, l0, a0), (m1, l1, a1) = lax.fori_loop(0, n_chunks // unroll, body, (init, init))
        o = a0 * (1.0 / l0) - lam * (a1 * (1.0 / l1))
        o_ref[r * sub:(r + 1) * sub, :] = (_rms(o, g_ref[...]) * (1.0 - lam_init)).astype(BF16)


def _diffattn(zqk, z, lamv, g_da, batch, seq, lam_init):
    n = z.shape[0]
    tq = min(TQ_ATT, seq)
    nq = seq // tq
    hq = QK_Q // DA_HEAD_W
    hk = QK_K // DA_HEAD_W
    hv = Z_DA_V // DA_HEAD_W
    return pl.pallas_call(
        functools.partial(_diffattn_kernel, seq=seq, lam_init=lam_init),
        grid=(batch, DA_HEADS, nq),
        in_specs=[
            pl.BlockSpec((tq, DA_HEAD_W), lambda b, h, i: (b * nq + i, hq + h)),
            pl.BlockSpec((seq, DA_HEAD_W), lambda b, h, i: (b, hk + h)),
            pl.BlockSpec((seq, DA_HEAD_W), lambda b, h, i: (b, hv + h)),
            pl.BlockSpec((4, DA_DIM), lambda b, h, i: (0, 0)),
            pl.BlockSpec((1, DA_HEAD_W), lambda b, h, i: (0, 0)),
        ],
        out_specs=pl.BlockSpec((tq, DA_HEAD_W), lambda b, h, i: (b * nq + i, h)),
        out_shape=jax.ShapeDtypeStruct((n, DA_HEADS * DA_HEAD_W), BF16),
        scratch_shapes=[pltpu.VMEM((seq, 2 * DA_HEAD_W), BF16)],
        compiler_params=_cparams(("parallel", "parallel", "arbitrary")),
        name="diff_attention",
    )(zqk, zqk, z, lamv, g_da)


def _mlstm_chunk(q, kt, v, g, gt, cta_ref, m_ref, h_ref, head, reverse):
    L = q.shape[0]
    col_i = head + (ML_HEADS if reverse else 0)
    col_f = col_i + 2 * ML_HEADS
    lane = lax.broadcasted_iota(I32, g.shape, 1)
    sub = lax.broadcasted_iota(I32, gt.shape, 0)
    i_col = jnp.sum(jnp.where(lane == col_i, g, 0.0), axis=1, keepdims=True)
    g_col = jnp.sum(jnp.where(lane == col_f, g, 0.0), axis=1, keepdims=True)
    i_row = jnp.sum(jnp.where(sub == col_i, gt, 0.0), axis=0, keepdims=True)
    g_row = jnp.sum(jnp.where(sub == col_f, gt, 0.0), axis=0, keepdims=True)

    ri = lax.broadcasted_iota(I32, (L, L), 0)
    ci = lax.broadcasted_iota(I32, (L, L), 1)
    tri = (ci >= ri) if reverse else (ci <= ri)
    last = 0 if reverse else L - 1
    lane_l = lax.broadcasted_iota(I32, (1, L), 1)
    g_last = jnp.sum(jnp.where(lane_l == last, g_row, 0.0), axis=1, keepdims=True)

    m_prev = m_ref[...]
    dmat = jnp.where(tri, g_col + (i_row - g_row), -jnp.inf)
    a_col = g_col + m_prev
    mj = jnp.maximum(a_col, jnp.max(dmat, axis=1, keepdims=True))
    w_inter = jnp.exp(a_col - mj)
    s = jnp.dot(q, kt, preferred_element_type=F32) * jnp.exp(dmat - mj)
    ones = jnp.ones((L, LANES), BF16)
    cta = cta_ref[...]
    num = (w_inter * jnp.dot(q, cta.astype(BF16), preferred_element_type=F32)
           + jnp.dot(s.astype(BF16), jnp.concatenate([v, ones], axis=1), preferred_element_type=F32))
    inv = 1.0 / jnp.maximum(jnp.abs(num[:, ML_DIM:]), jnp.exp(-mj))
    h_ref[...] = (num[:, :ML_DIM] * jnp.concatenate([inv] * (ML_DIM // LANES), axis=1)).astype(BF16)

    dec_row = g_last - g_row + i_row
    dec_col = g_last - g_col + i_col
    m_new = jnp.maximum(g_last + m_prev, jnp.max(dec_row, axis=1, keepdims=True))
    wk_col = jnp.exp(dec_col - m_new)
    carry_scale = jnp.exp(g_last + m_prev - m_new)
    wv = jnp.concatenate([(wk_col * v.astype(F32)).astype(BF16),
                          jnp.broadcast_to(wk_col, (L, LANES)).astype(BF16)], axis=1)
    cta_ref[...] = carry_scale * cta + jnp.dot(kt, wv, preferred_element_type=F32)
    m_ref[...] = m_new


def _mlstm_kernel(qf_ref, ktf_ref, vf_ref, gf_ref, gtf_ref, qb_ref, ktb_ref, vb_ref, gb_ref, gtb_ref,
                  hf_ref, hb_ref, ctf_ref, mf_ref, ctb_ref, mb_ref):
    @pl.when(pl.program_id(1) == 0)
    def _():
        ctf_ref[...] = jnp.zeros_like(ctf_ref)
        mf_ref[...] = jnp.zeros_like(mf_ref)
        ctb_ref[...] = jnp.zeros_like(ctb_ref)
        mb_ref[...] = jnp.zeros_like(mb_ref)

    for head in range(ML_HEADS):
        sl = slice(head * ML_DIM, (head + 1) * ML_DIM)
        _mlstm_chunk(qf_ref[:, sl], ktf_ref[sl, :], vf_ref[:, sl], gf_ref[...], gtf_ref[...],
                     ctf_ref.at[head], mf_ref.at[head], hf_ref.at[:, sl], head, False)
        _mlstm_chunk(qb_ref[:, sl], ktb_ref[sl, :], vb_ref[:, sl], gb_ref[...], gtb_ref[...],
                     ctb_ref.at[head], mb_ref.at[head], hb_ref.at[:, sl], head, True)


def _mlstm(z, kt, gates, gates_t, batch, seq):
    n = z.shape[0]
    L = L_MLSTM
    nc = seq // L
    width = ML_HEADS * ML_DIM
    cq = Z_ML_Q // width
    cv = Z_ML_V // width

    def specs(tix):
        return [
            pl.BlockSpec((L, width), lambda b, t: (b * nc + tix(t), cq)),
            pl.BlockSpec((width, L), lambda b, t: (0, b * nc + tix(t))),
            pl.BlockSpec((L, width), lambda b, t: (b * nc + tix(t), cv)),
            pl.BlockSpec((L, LANES), lambda b, t: (b * nc + tix(t), 0)),
            pl.BlockSpec((N_GATES, L), lambda b, t: (0, b * nc + tix(t))),
        ]

    fwd = lambda t: t
    bwd = lambda t: nc - 1 - t
    out = jax.ShapeDtypeStruct((n, width), BF16)
    state = [pltpu.VMEM((ML_HEADS, ML_DIM, ML_DIM + LANES), F32), pltpu.VMEM((ML_HEADS, 1, 1), F32)]
    return pl.pallas_call(
        _mlstm_kernel,
        grid=(batch, nc),
        in_specs=specs(fwd) + specs(bwd),
        out_specs=[pl.BlockSpec((L, width), lambda b, t: (b * nc + fwd(t), 0)),
                   pl.BlockSpec((L, width), lambda b, t: (b * nc + bwd(t), 0))],
        out_shape=[out, out],
        scratch_shapes=state + state,
        compiler_params=_cparams(("parallel", "arbitrary")),
        name="mlstm",
    )(z, kt, z, gates, gates_t, z, kt, z, gates, gates_t)


def _memkv_kernel(m_ref, g_ref, w_ref, o_ref):
    mn = _rms(m_ref[...], g_ref[...]).astype(BF16)
    o_ref[...] = jnp.dot(mn, w_ref[...], preferred_element_type=F32).astype(BF16)


def _memkv(mem2d, g_mem, w_xkv, mem_tokens):
    n = mem2d.shape[0]
    return pl.pallas_call(
        _memkv_kernel,
        grid=(n // mem_tokens,),
        in_specs=[pl.BlockSpec((mem_tokens, D_MODEL), lambda b: (b, 0)),
                  pl.BlockSpec((1, D_MODEL), lambda b: (0, 0)),
                  pl.BlockSpec((D_MODEL, 2 * D_MODEL), lambda b: (0, 0))],
        out_specs=pl.BlockSpec((mem_tokens, 2 * D_MODEL), lambda b: (b, 0)),
        out_shape=jax.ShapeDtypeStruct((n, 2 * D_MODEL), BF16),
        compiler_params=_cparams(("parallel",)),
        name="mem_kv",
    )(mem2d, g_mem, w_xkv)


def _sigmoid(x):
    return 1.0 / (1.0 + jnp.exp(-x))


def _mix_kernel(x_ref, oa_ref, hf_ref, hb_ref, mo_ref, bga_ref, bgb_ref, kv_ref,
                gml_ref, gxa_ref, gffn_ref, wpa_ref, wpb_ref, wo_ref, wxq_ref, wxo_ref, wrt_ref,
                x2_ref, hn_ref, afft_ref):
    hsum = hf_ref[...].astype(F32) + hb_ref[...].astype(F32)
    gml = gml_ref[...]
    parts = []
    for hh in range(ML_HEADS):
        sl = slice(hh * ML_DIM, (hh + 1) * ML_DIM)
        parts.append(_rms(hsum[:, sl], gml[:, sl]))
    hb = jnp.concatenate(parts, axis=1)
    hb = hb * _sigmoid(mo_ref[...].astype(F32))
    y_a = jnp.dot(oa_ref[...], wpa_ref[...], preferred_element_type=F32)
    y_b = jnp.dot(hb.astype(BF16), wpb_ref[...], preferred_element_type=F32)
    merged = (_sigmoid(bga_ref[...].astype(F32)) * y_a
              + _sigmoid(bgb_ref[...].astype(F32)) * y_b)
    x1 = x_ref[...] + jnp.dot(merged.astype(BF16), wo_ref[...], preferred_element_type=F32)

    hq = _rms(x1, gxa_ref[...]).astype(BF16)
    qx = (jnp.dot(hq, wxq_ref[...], preferred_element_type=F32) * (XA_DIM ** -0.5)).astype(BF16)
    kv = kv_ref[...]
    outs = []
    for hh in range(XA_HEADS):
        sl = slice(hh * XA_DIM, (hh + 1) * XA_DIM)
        kh = kv[:, sl]
        vh = kv[:, D_MODEL + hh * XA_DIM:D_MODEL + (hh + 1) * XA_DIM]
        s = lax.dot_general(qx[:, sl], kh, (((1,), (1,)), ((), ())), preferred_element_type=F32)
        p = jnp.exp(s - jnp.max(s, axis=-1, keepdims=True))
        p = p / jnp.sum(p, axis=-1, keepdims=True)
        outs.append(jnp.dot(p.astype(BF16), vh, preferred_element_type=F32).astype(BF16))
    o = jnp.concatenate(outs, axis=1)
    x2 = x1 + jnp.dot(o, wxo_ref[...], preferred_element_type=F32)
    x2_ref[...] = x2

    hn = _rms(x2, gffn_ref[...])
    hn_ref[...] = hn.astype(BF16)
    logits_t = lax.dot_general(wrt_ref[...], hn, (((1,), (1,)), ((), ())),
                               precision=lax.Precision.HIGHEST, preferred_element_type=F32)
    e = jnp.exp(logits_t - jnp.max(logits_t, axis=0, keepdims=True))
    afft_ref[...] = e / jnp.sum(e, axis=0, keepdims=True)


def _mix(x2d, oa, h_fw, h_bw, z, kv, p, batch, seq, mem_tokens):
    n = x2d.shape[0]
    tm = TM_MIX
    nt = seq // tm
    cmo = Z_ML_O // D_MODEL
    cbg = Z_BR_G // D_MODEL
    full = lambda r, c: pl.BlockSpec((r, c), lambda i: (0, 0), pipeline_mode=pl.Buffered(1))
    return pl.pallas_call(
        _mix_kernel,
        grid=(n // tm,),
        in_specs=[
            pl.BlockSpec((tm, D_MODEL), lambda i: (i, 0)),
            pl.BlockSpec((tm, D_MODEL), lambda i: (i, 0)),
            pl.BlockSpec((tm, D_MODEL), lambda i: (i, 0)),
            pl.BlockSpec((tm, D_MODEL), lambda i: (i, 0)),
            pl.BlockSpec((tm, D_MODEL), lambda i: (i, cmo)),
            pl.BlockSpec((tm, D_MODEL), lambda i: (i, cbg)),
            pl.BlockSpec((tm, D_MODEL), lambda i: (i, cbg + 1)),
            pl.BlockSpec((mem_tokens, 2 * D_MODEL), lambda i: (i // nt, 0)),
            full(1, D_MODEL), full(1, D_MODEL), full(1, D_MODEL),
            full(D_MODEL, D_MODEL), full(D_MODEL, D_MODEL), full(D_MODEL, D_MODEL),
            full(D_MODEL, D_MODEL), full(D_MODEL, D_MODEL),
            full(N_EXPERTS, D_MODEL),
        ],
        out_specs=[pl.BlockSpec((tm, D_MODEL), lambda i: (i, 0)),
                   pl.BlockSpec((tm, D_MODEL), lambda i: (i, 0)),
                   pl.BlockSpec((N_EXPERTS, tm), lambda i: (0, i))],
        out_shape=[jax.ShapeDtypeStruct((n, D_MODEL), F32),
                   jax.ShapeDtypeStruct((n, D_MODEL), BF16),
                   jax.ShapeDtypeStruct((N_EXPERTS, n), F32)],
        compiler_params=_cparams(("parallel",)),
        name="mix_xattn_router",
    )(x2d, oa, h_fw, h_bw, z, z, z, kv,
      p["g_ml_out"], p["g_xa"], p["g_ffn"], p["w_proj_a"], p["w_proj_b"], p["w_out"],
      p["w_xq"], p["w_xo"], p["w_router_t"])


def _prefix_incl(mask_f, upper, ones, lower):
    mb = mask_f.astype(BF16)
    lane_incl = jnp.dot(mb, upper, preferred_element_type=F32)
    row_tot = jnp.dot(mb, ones, preferred_element_type=F32)
    row_off = jnp.dot(lower, row_tot.astype(BF16), preferred_element_type=F32)
    return lane_incl + row_off, row_off


def _select_kernel(a_ref, pos_ref, off_ref, *, cap):
    bits = pltpu.bitcast(a_ref[...], I32)
    rows = bits.shape[1]

    def body(i, thr):
        cand = thr | jnp.left_shift(jnp.int32(1), 30 - i)
        cnt = jnp.sum(jnp.where(bits >= cand, 1.0, 0.0), axis=(1, 2), keepdims=True)
        return jnp.where(cnt >= cap, cand, thr)

    thr_all = lax.fori_loop(0, 31, body, jnp.zeros((N_EXPERTS, 1, 1), I32))

    li = lax.broadcasted_iota(I32, (LANES, LANES), 0)
    lj = lax.broadcasted_iota(I32, (LANES, LANES), 1)
    upper = (li <= lj).astype(BF16)
    ones = jnp.ones((LANES, LANES), BF16)
    ri = lax.broadcasted_iota(I32, (rows, rows), 0)
    rj = lax.broadcasted_iota(I32, (rows, rows), 1)
    lower = (rj < ri).astype(BF16)

    for e in range(N_EXPERTS):
        be = bits[e]
        thr = thr_all[e]
        gt = be > thr
        eq = be == thr
        need = cap - jnp.sum(jnp.where(gt, 1.0, 0.0), keepdims=True)
        eq_rank, _ = _prefix_incl(jnp.where(eq, 1.0, 0.0), upper, ones, lower)
        sel = gt | (eq & (eq_rank <= need))
        incl, row_off = _prefix_incl(jnp.where(sel, 1.0, 0.0), upper, ones, lower)
        pos_ref[e] = jnp.where(sel, incl - 1.0, -1.0).astype(I32)
        off_ref[e] = row_off.astype(I32)


def _select(aff_t, n_tokens, cap):
    rows = n_tokens // LANES
    a3 = aff_t.reshape(N_EXPERTS, rows, LANES)
    spec = pl.BlockSpec((N_EXPERTS, rows, LANES), lambda i: (0, 0, 0))
    return pl.pallas_call(
        functools.partial(_select_kernel, cap=float(cap)),
        grid=(1,),
        in_specs=[spec],
        out_specs=[spec, spec],
        out_shape=[jax.ShapeDtypeStruct((N_EXPERTS, rows, LANES), I32),
                   jax.ShapeDtypeStruct((N_EXPERTS, rows, LANES), I32)],
        compiler_params=_cparams(("arbitrary",)),
        name="expert_choice_select",
    )(a3)


def _dispatch_kernel(s_ref, x_ref, pos_ref, xe_ref, buf_ref, stage_ref, sem_ref, *, n_blocks):
    t = pl.program_id(0)
    blk_rows = DISP_BLOCK
    shift = blk_rows.bit_length() - 1
    x = x_ref[...]
    tt = x.shape[0]
    slot = lax.broadcasted_iota(I32, (blk_rows, tt), 0)
    row = lax.broadcasted_iota(I32, (blk_rows, 1), 0)

    def block_copy(e, blk):
        first = blk * blk_rows
        if not isinstance(first, int):
            first = pl.multiple_of(first, blk_rows)
        ring = blk & (DISP_RING - 1)
        return pltpu.make_async_copy(buf_ref.at[e, ring], xe_ref.at[e, pl.ds(first, blk_rows)],
                                     sem_ref.at[e, ring])

    def gather_rows(e, blk):
        rel = pos_ref[e:e + 1, :] - blk * blk_rows
        onehot = jnp.where(slot == rel, 1.0, 0.0).astype(BF16)
        return jnp.dot(onehot, x, preferred_element_type=F32).astype(BF16)

    def first_write(e, blk, rows):
        @pl.when(blk >= DISP_RING)
        def _():
            block_copy(e, blk - DISP_RING).wait()
        buf_ref[e, blk & (DISP_RING - 1)] = rows

    for e in range(N_EXPERTS):
        stage_ref[e] = gather_rows(e, lax.shift_right_logical(s_ref[t * N_EXPERTS + e], shift))

    for e in range(N_EXPERTS):
        s0 = s_ref[t * N_EXPERTS + e]
        s1 = s_ref[(t + 1) * N_EXPERTS + e]
        b0 = lax.shift_right_logical(s0, shift)
        r0 = s0 - b0 * blk_rows

        @pl.when(s1 > s0)
        def _(e=e, s1=s1, b0=b0, r0=r0):
            @pl.when(r0 == 0)
            def _():
                first_write(e, b0, stage_ref[e])

            @pl.when(r0 != 0)
            def _():
                ring = b0 & (DISP_RING - 1)
                mine = (row >= r0) & (row < s1 - b0 * blk_rows)
                buf_ref[e, ring] = jnp.where(mine, stage_ref[e], buf_ref[e, ring])

            for k in range(1, DISP_SPAN):
                @pl.when(s1 > (b0 + k) * blk_rows)
                def _(k=k):
                    first_write(e, b0 + k, gather_rows(e, b0 + k))

            for k in range(DISP_SPAN - 1):
                @pl.when(s1 >= (b0 + k + 1) * blk_rows)
                def _(k=k):
                    block_copy(e, b0 + k).start()

    @pl.when(t == pl.num_programs(0) - 1)
    def _():
        for e in range(N_EXPERTS):
            for blk in range(max(n_blocks - DISP_RING, 0), n_blocks):
                block_copy(e, blk).wait()


def _dispatch(hn, pos_t, slot_starts, cap):
    n = hn.shape[0]
    tt = T_DISP
    assert DISP_SPAN == tt // DISP_BLOCK + 1 and DISP_RING >= DISP_SPAN + 1
    n_blocks = cap // DISP_BLOCK
    grid_spec = pltpu.PrefetchScalarGridSpec(
        num_scalar_prefetch=1,
        grid=(n // tt,),
        in_specs=[pl.BlockSpec((tt, D_MODEL), lambda t, s: (t, 0)),
                  pl.BlockSpec((N_EXPERTS, tt), lambda t, s: (0, t))],
        out_specs=pl.BlockSpec(memory_space=pl.ANY),
        scratch_shapes=[pltpu.VMEM((N_EXPERTS, DISP_RING, DISP_BLOCK, D_MODEL), BF16),
                        pltpu.VMEM((N_EXPERTS, DISP_BLOCK, D_MODEL), BF16),
                        pltpu.SemaphoreType.DMA((N_EXPERTS, DISP_RING))],
    )
    return pl.pallas_call(
        functools.partial(_dispatch_kernel, n_blocks=n_blocks),
        grid_spec=grid_spec,
        out_shape=jax.ShapeDtypeStruct((N_EXPERTS, cap, D_MODEL), BF16),
        compiler_params=_cparams(("arbitrary",)),
        name="dispatch",
    )(slot_starts, hn, pos_t)


def _slot_starts(row_off, tile, cap):
    start = row_off[:, ::tile // LANES, 0]
    start = jnp.concatenate([start, jnp.full((N_EXPERTS, 1), cap, I32)], axis=1)
    return start.T.reshape(-1).astype(I32)


def _ffn_kernel(x_ref, wg_ref, wu_ref, wd_ref, o_ref):
    x = x_ref[...]
    acc = jnp.zeros((x.shape[0], D_MODEL), F32)
    for c in range(D_EXPERT // F_CHUNK):
        sl = slice(c * F_CHUNK, (c + 1) * F_CHUNK)
        gte = jnp.dot(x, wg_ref[:, sl], preferred_element_type=F32)
        up = jnp.dot(x, wu_ref[:, sl], preferred_element_type=F32)
        hid = gte * (1.0 / (1.0 + jnp.exp(-gte))) * up
        acc = acc + jnp.dot(hid.astype(BF16), wd_ref[sl, :], preferred_element_type=F32)
    o_ref[...] = acc.astype(BF16)


def _ffn(xe, w_gate, w_up, w_down, cap):
    tm = min(TM_FFN, cap)
    return pl.pallas_call(
        _ffn_kernel,
        grid=(N_EXPERTS, cap // tm),
        in_specs=[pl.BlockSpec((None, tm, D_MODEL), lambda e, i: (e, i, 0)),
                  pl.BlockSpec((None, D_MODEL, D_EXPERT), lambda e, i: (e, 0, 0)),
                  pl.BlockSpec((None, D_MODEL, D_EXPERT), lambda e, i: (e, 0, 0)),
                  pl.BlockSpec((None, D_EXPERT, D_MODEL), lambda e, i: (e, 0, 0))],
        out_specs=pl.BlockSpec((None, tm, D_MODEL), lambda e, i: (e, i, 0)),
        out_shape=jax.ShapeDtypeStruct((N_EXPERTS, cap, D_MODEL), BF16),
        compiler_params=_cparams(("parallel", "arbitrary")),
        name="expert_ffn",
    )(xe, w_gate, w_up, w_down)


def _combine_kernel(b0_ref, st_ref, x_ref, pos_ref, aff_ref, g_ref, *rest):
    ye_refs = rest[:2 * N_EXPERTS]
    o_ref = rest[2 * N_EXPERTS]
    t = pl.program_id(0)
    tt = pos_ref.shape[0]
    slot = lax.broadcasted_iota(I32, (tt, B_SLOT), 1)

    def gated_onehot(e, kk):
        rel = pos_ref[:, e:e + 1] - (b0_ref[t * N_EXPERTS + e] + kk) * B_SLOT
        return jnp.where(slot == rel, aff_ref[:, e:e + 1], 0.0).astype(BF16)

    onehots = [gated_onehot(e, 0) for e in range(N_EXPERTS)]
    for c in range(D_MODEL // COMB_STRIP):
        sl = slice(c * COMB_STRIP, (c + 1) * COMB_STRIP)
        acc = x_ref[:, sl]
        for e in range(N_EXPERTS):
            acc = acc + jnp.dot(onehots[e], ye_refs[2 * e][:, sl], preferred_element_type=F32)
        o_ref[:, sl] = acc
    for e in range(N_EXPERTS):
        @pl.when(st_ref[t * N_EXPERTS + e] != 0)
        def _(e=e):
            o_ref[...] += jnp.dot(gated_onehot(e, 1), ye_refs[2 * e + 1][...],
                                  preferred_element_type=F32)
    o_ref[...] = _rms(o_ref[...], g_ref[...])


def _combine(x2, pos_c, aff_c, ye, b0, straddle, g_final, cap):
    n = x2.shape[0]
    tt = T_COMB
    n_blocks = cap // B_SLOT

    def ye_spec(e, kk):
        return pl.BlockSpec(
            (None, B_SLOT, D_MODEL),
            lambda t, b0r, st: (e, jnp.minimum(b0r[t * N_EXPERTS + e] + kk, n_blocks - 1), 0))

    ye_specs = [ye_spec(e, kk) for e in range(N_EXPERTS) for kk in range(2)]
    grid_spec = pltpu.PrefetchScalarGridSpec(
        num_scalar_prefetch=2,
        grid=(n // tt,),
        in_specs=[pl.BlockSpec((tt, D_MODEL), lambda t, b0r, st: (t, 0)),
                  pl.BlockSpec((tt, N_EXPERTS), lambda t, b0r, st: (t, 0)),
                  pl.BlockSpec((tt, N_EXPERTS), lambda t, b0r, st: (t, 0)),
                  pl.BlockSpec((1, D_MODEL), lambda t, b0r, st: (0, 0))] + ye_specs,
        out_specs=pl.BlockSpec((tt, D_MODEL), lambda t, b0r, st: (t, 0)),
    )
    return pl.pallas_call(
        _combine_kernel,
        grid_spec=grid_spec,
        out_shape=jax.ShapeDtypeStruct((n, D_MODEL), F32),
        compiler_params=_cparams(("arbitrary",)),
        name="combine_final_norm",
    )(b0, straddle, x2, pos_c, aff_c, g_final, *([ye] * (2 * N_EXPERTS)))


def _encoder(x, mem, p):
    batch, seq, _ = x.shape
    mem_tokens = mem.shape[1]
    n = batch * seq
    cap = CAPACITY_FACTOR * n // N_EXPERTS
    x2d = x.reshape(n, D_MODEL)

    half = DA_DIM // 2
    pos = jnp.arange(seq, dtype=F32)
    inv = ROPE_THETA ** (-jnp.arange(half, dtype=F32) / half)
    ang = pos[:, None] * inv[None, :]
    cos = jnp.cos(ang)
    sin = jnp.sin(ang)
    cos_t = jnp.tile(cos, (1, LANES // half))
    sin_t = jnp.concatenate([jnp.tile(-sin, (1, 2)), jnp.tile(sin, (1, 2))], axis=1)

    zqk, hn = _qk_proj(x2d, p["g_mix"], p["w_qk"], cos_t, sin_t, seq)
    z = _proj(hn, p["w_rest"])
    kt = _ktrans(hn, p["w_kt"])
    gates, gates_t = _gates(hn, p["w_g"], p["w_gt"], p["b_g"], p["b_gt"])

    oa = _diffattn(zqk, z, p["lamv"], p["g_da_out"], batch, seq, p["lam_init"])
    h_fw, h_bw = _mlstm(z, kt, gates, gates_t, batch, seq)

    kv = _memkv(mem.reshape(batch * mem_tokens, D_MODEL), p["g_mem"], p["w_xkv"], mem_tokens)
    x2, hffn, aff_t = _mix(x2d, oa, h_fw, h_bw, z, kv, p, batch, seq, mem_tokens)

    pos3, off3 = _select(aff_t, n, cap)
    pos_t = pos3.reshape(N_EXPERTS, n)
    xe = _dispatch(hffn, pos_t, _slot_starts(off3, T_DISP, cap), cap)
    ye = _ffn(xe, p["w_e_gate"], p["w_e_up"], p["w_e_down"], cap)

    n_blocks = cap // B_SLOT
    start = off3[:, ::T_COMB // LANES, 0]
    end = jnp.concatenate([start[:, 1:], jnp.full((N_EXPERTS, 1), cap, I32)], axis=1)
    b0 = jnp.minimum(start // B_SLOT, n_blocks - 1)
    straddle = (end > start) & ((end - 1) // B_SLOT > b0)
    y = _combine(x2, pos_t.T, aff_t.T, ye, b0.T.reshape(-1).astype(I32),
                 straddle.T.reshape(-1).astype(I32), p["g_final"], cap)
    return y.reshape(batch, seq, D_MODEL)


def kernel(x_prompt, x_sample, mem_prompt, mem_sample, g_mix, w_in, lam_q1, lam_k1, lam_q2, lam_k2,
           g_da_out, b_ml_gates, g_ml_out, w_proj_a, w_proj_b, w_out, g_xa, g_mem, w_xq, w_xkv, w_xo,
           g_ffn, w_router, w_e_gate, w_e_up, w_e_down, g_final):
    l = 0
    w = w_in[l]
    assert w_in.shape[0] == 1, "single-layer encoder"
    gate_lo = W_GATES
    gate_hi = gate_lo + N_GATES
    lane = jnp.arange(LANES)
    part = lane // (DA_DIM // 2)
    in_head = (part % 2) * DA_DIM + (part // 2) * (DA_DIM // 2) + lane % (DA_DIM // 2)
    qk_cols = (jnp.arange(QK_WIDTH // LANES)[:, None] * LANES + in_head[None, :]).reshape(-1)
    w_g = w[:, gate_lo:gate_hi]
    b = b_ml_gates[l].astype(F32)
    p = {
        "g_mix": g_mix[l][None, :],
        "w_qk": w[:, :W_DA_V][:, qk_cols].astype(BF16),
        "w_rest": jnp.concatenate([w[:, W_DA_V:W_ML_K], w[:, W_ML_V:gate_lo], w[:, gate_hi:]],
                                  axis=1).astype(BF16),
        "w_kt": w[:, W_ML_K:W_ML_V].T.astype(BF16),
        "w_g": jnp.pad(w_g, ((0, 0), (0, LANES - N_GATES))).astype(BF16),
        "w_gt": w_g.T.astype(BF16),
        "b_g": jnp.pad(b, (0, LANES - N_GATES))[None, :],
        "b_gt": b[:, None],
        "lamv": jnp.stack([lam_q1[l], lam_k1[l], lam_q2[l], lam_k2[l]]).astype(F32),
        "lam_init": 0.8 - 0.6 * math.exp(-0.3 * l),
        "g_da_out": g_da_out[l][None, :],
        "g_ml_out": g_ml_out[l][None, :],
        "w_proj_a": w_proj_a[l].astype(BF16),
        "w_proj_b": w_proj_b[l].astype(BF16),
        "w_out": w_out[l].astype(BF16),
        "g_xa": g_xa[l][None, :],
        "g_mem": g_mem[l][None, :],
        "w_xq": w_xq[l].astype(BF16),
        "w_xkv": w_xkv[l].astype(BF16),
        "w_xo": w_xo[l].astype(BF16),
        "g_ffn": g_ffn[l][None, :],
        "w_router_t": w_router[l].T.astype(F32),
        "w_e_gate": w_e_gate[l].astype(BF16),
        "w_e_up": w_e_up[l].astype(BF16),
        "w_e_down": w_e_down[l].astype(BF16),
        "g_final": g_final[None, :],
    }
    y_prompt = _encoder(x_prompt, mem_prompt, p)
    y_sample = _encoder(x_sample, mem_sample, p)
    return (y_prompt, y_sample)
```

```python
import functools
import math

import jax
import jax.numpy as jnp
from jax import lax
from jax.experimental import pallas as pl
from jax.experimental.pallas import tpu as pltpu

F32 = jnp.float32
BF16 = jnp.bfloat16
I32 = jnp.int32

D_MODEL = 1024
DA_HEADS = 8
DA_DIM = 64
DA_HEAD_W = 2 * DA_DIM
ML_HEADS = 4
ML_DIM = 256
XA_HEADS = 4
XA_DIM = D_MODEL // XA_HEADS
N_EXPERTS = 16
CAPACITY_FACTOR = 2
D_EXPERT = 2048
EPS = 1e-6
ROPE_THETA = 10000.0
N_GATES = 4 * ML_HEADS

QK_Q = 0
QK_K = 1024
QK_WIDTH = 2048
Z_DA_V = 0
Z_ML_Q = 1024
Z_ML_V = 2048
Z_ML_O = 3072
Z_BR_G = 4096
Z_WIDTH = 6144
W_DA_V = 2048
W_ML_K = 4096
W_ML_V = 5120
W_GATES = 7168

LANES = 128
VMEM_LIMIT = 56 * 1024 * 1024

TM_PROJ = 1024
TN_PROJ = 1024
TN_REST = 2048
TQ_ATT = 2048
TK_ATT = 512
ATT_UNROLL = 8
L_MLSTM = 256
TM_MIX = 512
T_DISP = 256
DISP_BLOCK = 128
DISP_SPAN = 3
DISP_RING = 4
B_SLOT = 256
TM_FFN = 512
F_CHUNK = 512
T_COMB = 256
COMB_STRIP = 256


def _cparams(sem):
    return pltpu.CompilerParams(dimension_semantics=sem, vmem_limit_bytes=VMEM_LIMIT)


def _rms(xf, g):
    return xf * lax.rsqrt(jnp.mean(xf * xf, axis=-1, keepdims=True) + EPS) * g


def _qk_proj_kernel(x_ref, g_ref, w_ref, cos_ref, sin_ref, z_ref, hn_ref, hs_ref, *, n_q_tiles):
    j = pl.program_id(1)

    @pl.when(j == 0)
    def _():
        hb = _rms(x_ref[...], g_ref[...]).astype(BF16)
        hs_ref[...] = hb
        hn_ref[...] = hb

    scale = jnp.where(j < n_q_tiles, DA_DIM ** -0.5 * math.log2(math.e), 1.0)
    acc = jnp.dot(hs_ref[...], w_ref[...], preferred_element_type=F32) * scale
    cos = cos_ref[...]
    sin = sin_ref[...]
    for hh in range(acc.shape[1] // LANES):
        a = acc[:, hh * LANES:(hh + 1) * LANES]
        z_ref[hh] = (a * cos + pltpu.roll(a, LANES // 2, 1) * sin).astype(BF16)


def _qk_proj(x2d, g_mix, w_qk, cos_t, sin_t, seq):
    n = x2d.shape[0]
    tm, tn = TM_PROJ, TN_PROJ
    n_seq_tiles = seq // tm
    return pl.pallas_call(
        functools.partial(_qk_proj_kernel, n_q_tiles=D_MODEL // tn),
        grid=(n // tm, QK_WIDTH // tn),
        in_specs=[
            pl.BlockSpec((tm, D_MODEL), lambda i, j: (i, 0)),
            pl.BlockSpec((1, D_MODEL), lambda i, j: (0, 0)),
            pl.BlockSpec((D_MODEL, tn), lambda i, j: (0, j)),
            pl.BlockSpec((tm, LANES), lambda i, j: (i % n_seq_tiles, 0)),
            pl.BlockSpec((tm, LANES), lambda i, j: (i % n_seq_tiles, 0)),
        ],
        out_specs=[
            pl.BlockSpec((tn // LANES, tm, LANES), lambda i, j: (j, i, 0)),
            pl.BlockSpec((tm, D_MODEL), lambda i, j: (i, 0)),
        ],
        out_shape=[jax.ShapeDtypeStruct((QK_WIDTH // LANES, n, LANES), BF16),
                   jax.ShapeDtypeStruct((n, D_MODEL), BF16)],
        scratch_shapes=[pltpu.VMEM((tm, D_MODEL), BF16)],
        compiler_params=_cparams(("parallel", "arbitrary")),
        name="inproj_qk",
    )(x2d, g_mix, w_qk, cos_t, sin_t)


def _proj_kernel(h_ref, w_ref, z_ref):
    z_ref[...] = jnp.dot(h_ref[...], w_ref[...], preferred_element_type=F32).astype(BF16)


def _proj(hn, w_rest):
    n = hn.shape[0]
    tm, tn = TM_PROJ, TN_REST
    return pl.pallas_call(
        _proj_kernel,
        grid=(n // tm, Z_WIDTH // tn),
        in_specs=[pl.BlockSpec((tm, D_MODEL), lambda i, j: (i, 0)),
                  pl.BlockSpec((D_MODEL, tn), lambda i, j: (0, j))],
        out_specs=pl.BlockSpec((tm, tn), lambda i, j: (i, j)),
        out_shape=jax.ShapeDtypeStruct((n, Z_WIDTH), BF16),
        compiler_params=_cparams(("parallel", "arbitrary")),
        name="inproj_rest",
    )(hn, w_rest)


def _ktrans_kernel(w_ref, h_ref, o_ref):
    kt = lax.dot_general(w_ref[...], h_ref[...], (((1,), (1,)), ((), ())),
                         preferred_element_type=F32)
    kt = (kt * (ML_DIM ** -0.5)).astype(BF16)
    for c in range(o_ref.shape[0]):
        o_ref[c] = kt[:, c * L_MLSTM:(c + 1) * L_MLSTM]


def _ktrans(hn, w_kt):
    n = hn.shape[0]
    tm = TM_PROJ
    return pl.pallas_call(
        _ktrans_kernel,
        grid=(n // tm,),
        in_specs=[pl.BlockSpec((ML_HEADS * ML_DIM, D_MODEL), lambda i: (0, 0)),
                  pl.BlockSpec((tm, D_MODEL), lambda i: (i, 0))],
        out_specs=pl.BlockSpec((tm // L_MLSTM, ML_HEADS * ML_DIM, L_MLSTM), lambda i: (i, 0, 0)),
        out_shape=jax.ShapeDtypeStruct((n // L_MLSTM, ML_HEADS * ML_DIM, L_MLSTM), BF16),
        compiler_params=_cparams(("parallel",)),
        name="ml_k_transposed",
    )(w_kt, hn)


def _log_sigmoid(x):
    return jnp.minimum(x, 0.0) - jnp.log1p(jnp.exp(-jnp.abs(x)))


def _split3(x):
    hi = x.astype(BF16)
    r = x - hi.astype(F32)
    mid = r.astype(BF16)
    lo = (r - mid.astype(F32)).astype(BF16)
    return hi, mid, lo


def _gates_kernel(h_ref, w_ref, wt_ref, b_ref, bt_ref, g_ref, gt_ref):
    h = h_ref[...]
    g = jnp.dot(h, w_ref[...], preferred_element_type=F32) + b_ref[...]
    gt = lax.dot_general(wt_ref[...], h, (((1,), (1,)), ((), ())),
                         preferred_element_type=F32) + bt_ref[...]
    L = L_MLSTM
    n_in = 2 * ML_HEADS
    n_fw = 3 * ML_HEADS
    lane = lax.broadcasted_iota(I32, (L, LANES), 1)
    sub = lax.broadcasted_iota(I32, (N_GATES, L), 0)
    ri = lax.broadcasted_iota(I32, (L, L), 0)
    ci = lax.broadcasted_iota(I32, (L, L), 1)
    lower = jnp.where(ci <= ri, 1.0, 0.0).astype(BF16)
    upper = jnp.where(ci >= ri, 1.0, 0.0).astype(BF16)
    nt = (((1,), (1,)), ((), ()))
    for c in range(g.shape[0] // L):
        gc = g[c * L:(c + 1) * L, :]
        pre = jnp.zeros((L, LANES), F32)
        suf = jnp.zeros((L, LANES), F32)
        for piece in _split3(jnp.where(lane >= n_in, _log_sigmoid(gc), 0.0)):
            pre = pre + jnp.dot(lower, piece, preferred_element_type=F32)
            suf = suf + jnp.dot(upper, piece, preferred_element_type=F32)
        g_ref[c * L:(c + 1) * L, :] = jnp.where(lane < n_in, gc, jnp.where(lane < n_fw, pre, suf))

        gtc = gt[:, c * L:(c + 1) * L]
        pre = jnp.zeros((N_GATES, L), F32)
        suf = jnp.zeros((N_GATES, L), F32)
        for piece in _split3(jnp.where(sub >= n_in, _log_sigmoid(gtc), 0.0)):
            pre = pre + lax.dot_general(piece, lower, nt, preferred_element_type=F32)
            suf = suf + lax.dot_general(piece, upper, nt, preferred_element_type=F32)
        gt_ref[:, c * L:(c + 1) * L] = jnp.where(sub < n_in, gtc, jnp.where(sub < n_fw, pre, suf))


def _gates(hn, w_g, w_gt, b_g, b_gt):
    n = hn.shape[0]
    tm = TM_PROJ
    return pl.pallas_call(
        _gates_kernel,
        grid=(n // tm,),
        in_specs=[pl.BlockSpec((tm, D_MODEL), lambda i: (i, 0)),
                  pl.BlockSpec((D_MODEL, LANES), lambda i: (0, 0)),
                  pl.BlockSpec((N_GATES, D_MODEL), lambda i: (0, 0)),
                  pl.BlockSpec((1, LANES), lambda i: (0, 0)),
                  pl.BlockSpec((N_GATES, 1), lambda i: (0, 0))],
        out_specs=[pl.BlockSpec((tm, LANES), lambda i: (i, 0)),
                   pl.BlockSpec((N_GATES, tm), lambda i: (0, i))],
        out_shape=[jax.ShapeDtypeStruct((n, LANES), F32),
                   jax.ShapeDtypeStruct((N_GATES, n), F32)],
        compiler_params=_cparams(("parallel",)),
        name="ml_gates",
    )(hn, w_g, w_gt, b_g, b_gt)


def _diffattn_kernel(q_ref, k_ref, v_ref, lamv_ref, g_ref, o_ref, va_ref, *, seq, lam_init):
    tq = q_ref.shape[0]
    tk = min(TK_ATT, seq)
    n_chunks = seq // tk
    unroll = min(n_chunks, ATT_UNROLL)

    @pl.when(pl.program_id(2) == 0)
    def _():
        va_ref[:, :DA_HEAD_W] = v_ref[...]
        va_ref[:, DA_HEAD_W:] = jnp.ones((seq, DA_HEAD_W), BF16)

    lv = lamv_ref[...]
    lam = (jnp.exp(jnp.sum(lv[0:1, :] * lv[1:2, :], axis=-1, keepdims=True))
           - jnp.exp(jnp.sum(lv[2:3, :] * lv[3:4, :], axis=-1, keepdims=True)) + lam_init)

    n_sub = 2 if (n_chunks == unroll and tq % 2048 == 0) else 1
    sub = tq // n_sub
    for r in range(n_sub):
        q = q_ref[r * sub:(r + 1) * sub, :]
        lane = lax.broadcasted_iota(I32, q.shape, 1)
        zero = jnp.zeros_like(q)
        in_map0 = ((lane // (DA_DIM // 2)) % 2) == 0
        qs = (jnp.where(in_map0, q, zero), jnp.where(in_map0, zero, q))

        def chunk(start, carry, qs=qs):
            kc = k_ref[pl.ds(start, tk), :]
            vc = va_ref[pl.ds(start, tk), :]
            out = []
            for qm, (m, l, a) in zip(qs, carry):
                s = lax.dot_general(qm, kc, (((1,), (1,)), ((), ())), preferred_element_type=F32)
                mn = jnp.maximum(m, jnp.max(s, axis=-1, keepdims=True))
                alpha = jnp.exp2(m - mn)
                p = jnp.exp2((s - mn).astype(BF16))
                pv = jnp.dot(p, vc, preferred_element_type=F32)
                out.append((mn, alpha * l + pv[:, DA_HEAD_W:],
                            alpha * a + pv[:, :DA_HEAD_W]))
            return tuple(out)

        def body(c, carry, chunk=chunk):
            for u in range(unroll):
                carry = chunk(pl.multiple_of((c * unroll + u) * tk, tk), carry)
            return carry

        init = (jnp.full((sub, 1), -jnp.inf, F32), jnp.zeros((sub, DA_HEAD_W), F32),
                jnp.zeros((sub, DA_HEAD_W), F32))
        (m0, l0, a0), (m1, l1, a1) = lax.fori_loop(0, n_chunks // unroll, body, (init, init))
        o = a0 * (1.0 / l0) - lam * (a1 * (1.0 / l1))
        o_ref[r * sub:(r + 1) * sub, :] = (_rms(o, g_ref[...]) * (1.0 - lam_init)).astype(BF16)


def _diffattn(zqk, z, lamv, g_da, batch, seq, lam_init):
    n = z.shape[0]
    tq = min(TQ_ATT, seq)
    nq = seq // tq
    hq = QK_Q // DA_HEAD_W
    hk = QK_K // DA_HEAD_W
    hv = Z_DA_V // DA_HEAD_W
    return pl.pallas_call(
        functools.partial(_diffattn_kernel, seq=seq, lam_init=lam_init),
        grid=(batch, DA_HEADS, nq),
        in_specs=[
            pl.BlockSpec((None, tq, DA_HEAD_W), lambda b, h, i: (hq + h, b * nq + i, 0)),
            pl.BlockSpec((None, seq, DA_HEAD_W), lambda b, h, i: (hk + h, b, 0)),
            pl.BlockSpec((seq, DA_HEAD_W), lambda b, h, i: (b, hv + h)),
            pl.BlockSpec((4, DA_DIM), lambda b, h, i: (0, 0)),
            pl.BlockSpec((1, DA_HEAD_W), lambda b, h, i: (0, 0)),
        ],
        out_specs=pl.BlockSpec((tq, DA_HEAD_W), lambda b, h, i: (b * nq + i, h)),
        out_shape=jax.ShapeDtypeStruct((n, DA_HEADS * DA_HEAD_W), BF16),
        scratch_shapes=[pltpu.VMEM((seq, 2 * DA_HEAD_W), BF16)],
        compiler_params=_cparams(("parallel", "parallel", "arbitrary")),
        name="diff_attention",
    )(zqk, zqk, z, lamv, g_da)


def _mlstm_chunk(q, kt, v, g, gt, cta_ref, m_ref, h_ref, head, reverse):
    L = q.shape[0]
    col_i = head + (ML_HEADS if reverse else 0)
    col_f = col_i + 2 * ML_HEADS
    lane = lax.broadcasted_iota(I32, g.shape, 1)
    sub = lax.broadcasted_iota(I32, gt.shape, 0)
    i_col = jnp.sum(jnp.where(lane == col_i, g, 0.0), axis=1, keepdims=True)
    g_col = jnp.sum(jnp.where(lane == col_f, g, 0.0), axis=1, keepdims=True)
    i_row = jnp.sum(jnp.where(sub == col_i, gt, 0.0), axis=0, keepdims=True)
    g_row = jnp.sum(jnp.where(sub == col_f, gt, 0.0), axis=0, keepdims=True)

    ri = lax.broadcasted_iota(I32, (L, L), 0)
    ci = lax.broadcasted_iota(I32, (L, L), 1)
    tri = (ci >= ri) if reverse else (ci <= ri)
    last = 0 if reverse else L - 1
    lane_l = lax.broadcasted_iota(I32, (1, L), 1)
    g_last = jnp.sum(jnp.where(lane_l == last, g_row, 0.0), axis=1, keepdims=True)

    m_prev = m_ref[...]
    dmat = jnp.where(tri, g_col + (i_row - g_row), -jnp.inf)
    a_col = g_col + m_prev
    mj = jnp.maximum(a_col, jnp.max(dmat, axis=1, keepdims=True))
    w_inter = jnp.exp(a_col - mj)
    s = jnp.dot(q, kt, preferred_element_type=F32) * jnp.exp(dmat - mj)
    ones = jnp.ones((L, LANES), BF16)
    cta = cta_ref[...]
    num = (w_inter * jnp.dot(q, cta.astype(BF16), preferred_element_type=F32)
           + jnp.dot(s.astype(BF16), jnp.concatenate([v, ones], axis=1), preferred_element_type=F32))
    inv = 1.0 / jnp.maximum(jnp.abs(num[:, ML_DIM:]), jnp.exp(-mj))
    h_ref[...] = (num[:, :ML_DIM] * jnp.concatenate([inv] * (ML_DIM // LANES), axis=1)).astype(BF16)

    dec_row = g_last - g_row + i_row
    dec_col = g_last - g_col + i_col
    m_new = jnp.maximum(g_last + m_prev, jnp.max(dec_row, axis=1, keepdims=True))
    wk_col = jnp.exp(dec_col - m_new)
    carry_scale = jnp.exp(g_last + m_prev - m_new)
    wv = jnp.concatenate([(wk_col * v.astype(F32)).astype(BF16),
                          jnp.broadcast_to(wk_col, (L, LANES)).astype(BF16)], axis=1)
    cta_ref[...] = carry_scale * cta + jnp.dot(kt, wv, preferred_element_type=F32)
    m_ref[...] = m_new


def _mlstm_kernel(qf_ref, ktf_ref, vf_ref, gf_ref, gtf_ref, qb_ref, ktb_ref, vb_ref, gb_ref, gtb_ref,
                  hf_ref, hb_ref, ctf_ref, mf_ref, ctb_ref, mb_ref):
    @pl.when(pl.program_id(1) == 0)
    def _():
        ctf_ref[...] = jnp.zeros_like(ctf_ref)
        mf_ref[...] = jnp.zeros_like(mf_ref)
        ctb_ref[...] = jnp.zeros_like(ctb_ref)
        mb_ref[...] = jnp.zeros_like(mb_ref)

    for head in range(ML_HEADS):
        sl = slice(head * ML_DIM, (head + 1) * ML_DIM)
        _mlstm_chunk(qf_ref[:, sl], ktf_ref[sl, :], vf_ref[:, sl], gf_ref[...], gtf_ref[...],
                     ctf_ref.at[head], mf_ref.at[head], hf_ref.at[:, sl], head, False)
        _mlstm_chunk(qb_ref[:, sl], ktb_ref[sl, :], vb_ref[:, sl], gb_ref[...], gtb_ref[...],
                     ctb_ref.at[head], mb_ref.at[head], hb_ref.at[:, sl], head, True)


def _mlstm(z, kt, gates, gates_t, batch, seq):
    n = z.shape[0]
    L = L_MLSTM
    nc = seq // L
    width = ML_HEADS * ML_DIM
    cq = Z_ML_Q // width
    cv = Z_ML_V // width

    def specs(tix):
        return [
            pl.BlockSpec((L, width), lambda b, t: (b * nc + tix(t), cq)),
            pl.BlockSpec((None, width, L), lambda b, t: (b * nc + tix(t), 0, 0)),
            pl.BlockSpec((L, width), lambda b, t: (b * nc + tix(t), cv)),
            pl.BlockSpec((L, LANES), lambda b, t: (b * nc + tix(t), 0)),
            pl.BlockSpec((N_GATES, L), lambda b, t: (0, b * nc + tix(t))),
        ]

    fwd = lambda t: t
    bwd = lambda t: nc - 1 - t
    out = jax.ShapeDtypeStruct((n, width), BF16)
    state = [pltpu.VMEM((ML_HEADS, ML_DIM, ML_DIM + LANES), F32), pltpu.VMEM((ML_HEADS, 1, 1), F32)]
    return pl.pallas_call(
        _mlstm_kernel,
        grid=(batch, nc),
        in_specs=specs(fwd) + specs(bwd),
        out_specs=[pl.BlockSpec((L, width), lambda b, t: (b * nc + fwd(t), 0)),
                   pl.BlockSpec((L, width), lambda b, t: (b * nc + bwd(t), 0))],
        out_shape=[out, out],
        scratch_shapes=state + state,
        compiler_params=_cparams(("parallel", "arbitrary")),
        name="mlstm",
    )(z, kt, z, gates, gates_t, z, kt, z, gates, gates_t)


def _memkv_kernel(m_ref, g_ref, w_ref, o_ref):
    mn = _rms(m_ref[...], g_ref[...]).astype(BF16)
    o_ref[...] = jnp.dot(mn, w_ref[...], preferred_element_type=F32).astype(BF16)


def _memkv(mem2d, g_mem, w_xkv, mem_tokens):
    n = mem2d.shape[0]
    return pl.pallas_call(
        _memkv_kernel,
        grid=(n // mem_tokens,),
        in_specs=[pl.BlockSpec((mem_tokens, D_MODEL), lambda b: (b, 0)),
                  pl.BlockSpec((1, D_MODEL), lambda b: (0, 0)),
                  pl.BlockSpec((D_MODEL, 2 * D_MODEL), lambda b: (0, 0))],
        out_specs=pl.BlockSpec((mem_tokens, 2 * D_MODEL), lambda b: (b, 0)),
        out_shape=jax.ShapeDtypeStruct((n, 2 * D_MODEL), BF16),
        compiler_params=_cparams(("parallel",)),
        name="mem_kv",
    )(mem2d, g_mem, w_xkv)


def _sigmoid(x):
    return 1.0 / (1.0 + jnp.exp(-x))


def _mix_kernel(x_ref, oa_ref, hf_ref, hb_ref, mo_ref, bga_ref, bgb_ref, kv_ref,
                gml_ref, gxa_ref, gffn_ref, wpa_ref, wpb_ref, wo_ref, wxq_ref, wxo_ref, wrt_ref,
                x2_ref, hn_ref, afft_ref):
    hsum = hf_ref[...].astype(F32) + hb_ref[...].astype(F32)
    gml = gml_ref[...]
    parts = []
    for hh in range(ML_HEADS):
        sl = slice(hh * ML_DIM, (hh + 1) * ML_DIM)
        parts.append(_rms(hsum[:, sl], gml[:, sl]))
    hb = jnp.concatenate(parts, axis=1)
    hb = hb * _sigmoid(mo_ref[...].astype(F32))
    y_a = jnp.dot(oa_ref[...], wpa_ref[...], preferred_element_type=F32)
    y_b = jnp.dot(hb.astype(BF16), wpb_ref[...], preferred_element_type=F32)
    merged = (_sigmoid(bga_ref[...].astype(F32)) * y_a
              + _sigmoid(bgb_ref[...].astype(F32)) * y_b)
    x1 = x_ref[...] + jnp.dot(merged.astype(BF16), wo_ref[...], preferred_element_type=F32)

    hq = _rms(x1, gxa_ref[...]).astype(BF16)
    qx = (jnp.dot(hq, wxq_ref[...], preferred_element_type=F32) * (XA_DIM ** -0.5)).astype(BF16)
    kv = kv_ref[...]
    outs = []
    for hh in range(XA_HEADS):
        sl = slice(hh * XA_DIM, (hh + 1) * XA_DIM)
        kh = kv[:, sl]
        vh = kv[:, D_MODEL + hh * XA_DIM:D_MODEL + (hh + 1) * XA_DIM]
        s = lax.dot_general(qx[:, sl], kh, (((1,), (1,)), ((), ())), preferred_element_type=F32)
        p = jnp.exp(s - jnp.max(s, axis=-1, keepdims=True))
        p = p / jnp.sum(p, axis=-1, keepdims=True)
        outs.append(jnp.dot(p.astype(BF16), vh, preferred_element_type=F32).astype(BF16))
    o = jnp.concatenate(outs, axis=1)
    x2 = x1 + jnp.dot(o, wxo_ref[...], preferred_element_type=F32)
    x2_ref[...] = x2

    hn = _rms(x2, gffn_ref[...])
    hn_ref[...] = hn.astype(BF16)
    logits_t = lax.dot_general(wrt_ref[...], hn, (((1,), (1,)), ((), ())),
                               precision=lax.Precision.HIGHEST, preferred_element_type=F32)
    e = jnp.exp(logits_t - jnp.max(logits_t, axis=0, keepdims=True))
    afft_ref[...] = e / jnp.sum(e, axis=0, keepdims=True)


def _mix(x2d, oa, h_fw, h_bw, z, kv, p, batch, seq, mem_tokens):
    n = x2d.shape[0]
    tm = TM_MIX
    nt = seq // tm
    cmo = Z_ML_O // D_MODEL
    cbg = Z_BR_G // D_MODEL
    full = lambda r, c: pl.BlockSpec((r, c), lambda i: (0, 0), pipeline_mode=pl.Buffered(1))
    return pl.pallas_call(
        _mix_kernel,
        grid=(n // tm,),
        in_specs=[
            pl.BlockSpec((tm, D_MODEL), lambda i: (i, 0)),
            pl.BlockSpec((tm, D_MODEL), lambda i: (i, 0)),
            pl.BlockSpec((tm, D_MODEL), lambda i: (i, 0)),
            pl.BlockSpec((tm, D_MODEL), lambda i: (i, 0)),
            pl.BlockSpec((tm, D_MODEL), lambda i: (i, cmo)),
            pl.BlockSpec((tm, D_MODEL), lambda i: (i, cbg)),
            pl.BlockSpec((tm, D_MODEL), lambda i: (i, cbg + 1)),
            pl.BlockSpec((mem_tokens, 2 * D_MODEL), lambda i: (i // nt, 0)),
            full(1, D_MODEL), full(1, D_MODEL), full(1, D_MODEL),
            full(D_MODEL, D_MODEL), full(D_MODEL, D_MODEL), full(D_MODEL, D_MODEL),
            full(D_MODEL, D_MODEL), full(D_MODEL, D_MODEL),
            full(N_EXPERTS, D_MODEL),
        ],
        out_specs=[pl.BlockSpec((tm, D_MODEL), lambda i: (i, 0)),
                   pl.BlockSpec((tm, D_MODEL), lambda i: (i, 0)),
                   pl.BlockSpec((N_EXPERTS, tm), lambda i: (0, i))],
        out_shape=[jax.ShapeDtypeStruct((n, D_MODEL), F32),
                   jax.ShapeDtypeStruct((n, D_MODEL), BF16),
                   jax.ShapeDtypeStruct((N_EXPERTS, n), F32)],
        compiler_params=_cparams(("parallel",)),
        name="mix_xattn_router",
    )(x2d, oa, h_fw, h_bw, z, z, z, kv,
      p["g_ml_out"], p["g_xa"], p["g_ffn"], p["w_proj_a"], p["w_proj_b"], p["w_out"],
      p["w_xq"], p["w_xo"], p["w_router_t"])


def _prefix_incl(mask_f, upper, ones, lower):
    mb = mask_f.astype(BF16)
    lane_incl = jnp.dot(mb, upper, preferred_element_type=F32)
    row_tot = jnp.dot(mb, ones, preferred_element_type=F32)
    row_off = jnp.dot(lower, row_tot.astype(BF16), preferred_element_type=F32)
    return lane_incl + row_off, row_off


def _select_kernel(a_ref, pos_ref, off_ref, *, cap):
    bits = pltpu.bitcast(a_ref[...], I32)
    rows = bits.shape[1]

    def body(i, thr):
        cand = thr | jnp.left_shift(jnp.int32(1), 30 - i)
        cnt = jnp.sum(jnp.where(bits >= cand, 1.0, 0.0), axis=(1, 2), keepdims=True)
        return jnp.where(cnt >= cap, cand, thr)

    thr_all = lax.fori_loop(0, 31, body, jnp.zeros((N_EXPERTS, 1, 1), I32))

    li = lax.broadcasted_iota(I32, (LANES, LANES), 0)
    lj = lax.broadcasted_iota(I32, (LANES, LANES), 1)
    upper = (li <= lj).astype(BF16)
    ones = jnp.ones((LANES, LANES), BF16)
    ri = lax.broadcasted_iota(I32, (rows, rows), 0)
    rj = lax.broadcasted_iota(I32, (rows, rows), 1)
    lower = (rj < ri).astype(BF16)

    for e in range(N_EXPERTS):
        be = bits[e]
        thr = thr_all[e]
        gt = be > thr
        eq = be == thr
        need = cap - jnp.sum(jnp.where(gt, 1.0, 0.0), keepdims=True)
        eq_rank, _ = _prefix_incl(jnp.where(eq, 1.0, 0.0), upper, ones, lower)
        sel = gt | (eq & (eq_rank <= need))
        incl, row_off = _prefix_incl(jnp.where(sel, 1.0, 0.0), upper, ones, lower)
        pos_ref[e] = jnp.where(sel, incl - 1.0, -1.0).astype(I32)
        off_ref[e] = row_off.astype(I32)


def _select(aff_t, n_tokens, cap):
    rows = n_tokens // LANES
    a3 = aff_t.reshape(N_EXPERTS, rows, LANES)
    spec = pl.BlockSpec((N_EXPERTS, rows, LANES), lambda i: (0, 0, 0))
    return pl.pallas_call(
        functools.partial(_select_kernel, cap=float(cap)),
        grid=(1,),
        in_specs=[spec],
        out_specs=[spec, spec],
        out_shape=[jax.ShapeDtypeStruct((N_EXPERTS, rows, LANES), I32),
                   jax.ShapeDtypeStruct((N_EXPERTS, rows, LANES), I32)],
        compiler_params=_cparams(("arbitrary",)),
        name="expert_choice_select",
    )(a3)


def _dispatch_kernel(s_ref, x_ref, pos_ref, xe_ref, buf_ref, stage_ref, sem_ref, *, n_blocks):
    t = pl.program_id(0)
    blk_rows = DISP_BLOCK
    shift = blk_rows.bit_length() - 1
    x = x_ref[...]
    tt = x.shape[0]
    slot = lax.broadcasted_iota(I32, (blk_rows, tt), 0)
    row = lax.broadcasted_iota(I32, (blk_rows, 1), 0)

    def block_copy(e, blk):
        first = blk * blk_rows
        if not isinstance(first, int):
            first = pl.multiple_of(first, blk_rows)
        ring = blk & (DISP_RING - 1)
        return pltpu.make_async_copy(buf_ref.at[e, ring], xe_ref.at[e, pl.ds(first, blk_rows)],
                                     sem_ref.at[e, ring])

    def gather_rows(e, blk):
        rel = pos_ref[e:e + 1, :] - blk * blk_rows
        onehot = jnp.where(slot == rel, 1.0, 0.0).astype(BF16)
        return jnp.dot(onehot, x, preferred_element_type=F32).astype(BF16)

    def first_write(e, blk, rows):
        @pl.when(blk >= DISP_RING)
        def _():
            block_copy(e, blk - DISP_RING).wait()
        buf_ref[e, blk & (DISP_RING - 1)] = rows

    for e in range(N_EXPERTS):
        stage_ref[e] = gather_rows(e, lax.shift_right_logical(s_ref[t * N_EXPERTS + e], shift))

    for e in range(N_EXPERTS):
        s0 = s_ref[t * N_EXPERTS + e]
        s1 = s_ref[(t + 1) * N_EXPERTS + e]
        b0 = lax.shift_right_logical(s0, shift)
        r0 = s0 - b0 * blk_rows

        @pl.when(s1 > s0)
        def _(e=e, s1=s1, b0=b0, r0=r0):
            @pl.when(r0 == 0)
            def _():
                first_write(e, b0, stage_ref[e])

            @pl.when(r0 != 0)
            def _():
                ring = b0 & (DISP_RING - 1)
                mine = (row >= r0) & (row < s1 - b0 * blk_rows)
                buf_ref[e, ring] = jnp.where(mine, stage_ref[e], buf_ref[e, ring])

            for k in range(1, DISP_SPAN):
                @pl.when(s1 > (b0 + k) * blk_rows)
                def _(k=k):
                    first_write(e, b0 + k, gather_rows(e, b0 + k))

            for k in range(DISP_SPAN - 1):
                @pl.when(s1 >= (b0 + k + 1) * blk_rows)
                def _(k=k):
                    block_copy(e, b0 + k).start()

    @pl.when(t == pl.num_programs(0) - 1)
    def _():
        for e in range(N_EXPERTS):
            for blk in range(max(n_blocks - DISP_RING, 0), n_blocks):
                block_copy(e, blk).wait()


def _dispatch(hn, pos_t, slot_starts, cap):
    n = hn.shape[0]
    tt = T_DISP
    assert DISP_SPAN == tt // DISP_BLOCK + 1 and DISP_RING >= DISP_SPAN + 1
    n_blocks = cap // DISP_BLOCK
    grid_spec = pltpu.PrefetchScalarGridSpec(
        num_scalar_prefetch=1,
        grid=(n // tt,),
        in_specs=[pl.BlockSpec((tt, D_MODEL), lambda t, s: (t, 0)),
                  pl.BlockSpec((N_EXPERTS, tt), lambda t, s: (0, t))],
        out_specs=pl.BlockSpec(memory_space=pl.ANY),
        scratch_shapes=[pltpu.VMEM((N_EXPERTS, DISP_RING, DISP_BLOCK, D_MODEL), BF16),
                        pltpu.VMEM((N_EXPERTS, DISP_BLOCK, D_MODEL), BF16),
                        pltpu.SemaphoreType.DMA((N_EXPERTS, DISP_RING))],
    )
    return pl.pallas_call(
        functools.partial(_dispatch_kernel, n_blocks=n_blocks),
        grid_spec=grid_spec,
        out_shape=jax.ShapeDtypeStruct((N_EXPERTS, cap, D_MODEL), BF16),
        compiler_params=_cparams(("arbitrary",)),
        name="dispatch",
    )(slot_starts, hn, pos_t)


def _slot_starts(row_off, tile, cap):
    start = row_off[:, ::tile // LANES, 0]
    start = jnp.concatenate([start, jnp.full((N_EXPERTS, 1), cap, I32)], axis=1)
    return start.T.reshape(-1).astype(I32)


def _ffn_kernel(x_ref, wg_ref, wu_ref, wd_ref, o_ref):
    x = x_ref[...]
    acc = jnp.zeros((x.shape[0], D_MODEL), F32)
    for c in range(D_EXPERT // F_CHUNK):
        sl = slice(c * F_CHUNK, (c + 1) * F_CHUNK)
        gte = jnp.dot(x, wg_ref[:, sl], preferred_element_type=F32)
        up = jnp.dot(x, wu_ref[:, sl], preferred_element_type=F32)
        hid = gte * (1.0 / (1.0 + jnp.exp(-gte))) * up
        acc = acc + jnp.dot(hid.astype(BF16), wd_ref[sl, :], preferred_element_type=F32)
    o_ref[...] = acc.astype(BF16)


def _ffn(xe, w_gate, w_up, w_down, cap):
    tm = min(TM_FFN, cap)
    return pl.pallas_call(
        _ffn_kernel,
        grid=(N_EXPERTS, cap // tm),
        in_specs=[pl.BlockSpec((None, tm, D_MODEL), lambda e, i: (e, i, 0)),
                  pl.BlockSpec((None, D_MODEL, D_EXPERT), lambda e, i: (e, 0, 0)),
                  pl.BlockSpec((None, D_MODEL, D_EXPERT), lambda e, i: (e, 0, 0)),
                  pl.BlockSpec((None, D_EXPERT, D_MODEL), lambda e, i: (e, 0, 0))],
        out_specs=pl.BlockSpec((None, tm, D_MODEL), lambda e, i: (e, i, 0)),
        out_shape=jax.ShapeDtypeStruct((N_EXPERTS, cap, D_MODEL), BF16),
        compiler_params=_cparams(("parallel", "arbitrary")),
        name="expert_ffn",
    )(xe, w_gate, w_up, w_down)


def _combine_kernel(b0_ref, st_ref, x_ref, pos_ref, aff_ref, g_ref, *rest):
    ye_refs = rest[:2 * N_EXPERTS]
    o_ref = rest[2 * N_EXPERTS]
    t = pl.program_id(0)
    tt = pos_ref.shape[0]
    slot = lax.broadcasted_iota(I32, (tt, B_SLOT), 1)

    def gated_onehot(e, kk):
        rel = pos_ref[:, e:e + 1] - (b0_ref[t * N_EXPERTS + e] + kk) * B_SLOT
        return jnp.where(slot == rel, aff_ref[:, e:e + 1], 0.0).astype(BF16)

    onehots = [gated_onehot(e, 0) for e in range(N_EXPERTS)]
    for c in range(D_MODEL // COMB_STRIP):
        sl = slice(c * COMB_STRIP, (c + 1) * COMB_STRIP)
        acc = x_ref[:, sl]
        for e in range(N_EXPERTS):
            acc = acc + jnp.dot(onehots[e], ye_refs[2 * e][:, sl], preferred_element_type=F32)
        o_ref[:, sl] = acc
    for e in range(N_EXPERTS):
        @pl.when(st_ref[t * N_EXPERTS + e] != 0)
        def _(e=e):
            o_ref[...] += jnp.dot(gated_onehot(e, 1), ye_refs[2 * e + 1][...],
                                  preferred_element_type=F32)
    o_ref[...] = _rms(o_ref[...], g_ref[...])


def _combine(x2, pos_c, aff_c, ye, b0, straddle, g_final, cap):
    n = x2.shape[0]
    tt = T_COMB
    n_blocks = cap // B_SLOT

    def ye_spec(e, kk):
        return pl.BlockSpec(
            (None, B_SLOT, D_MODEL),
            lambda t, b0r, st: (e, jnp.minimum(b0r[t * N_EXPERTS + e] + kk, n_blocks - 1), 0))

    ye_specs = [ye_spec(e, kk) for e in range(N_EXPERTS) for kk in range(2)]
    grid_spec = pltpu.PrefetchScalarGridSpec(
        num_scalar_prefetch=2,
        grid=(n // tt,),
        in_specs=[pl.BlockSpec((tt, D_MODEL), lambda t, b0r, st: (t, 0)),
                  pl.BlockSpec((tt, N_EXPERTS), lambda t, b0r, st: (t, 0)),
                  pl.BlockSpec((tt, N_EXPERTS), lambda t, b0r, st: (t, 0)),
                  pl.BlockSpec((1, D_MODEL), lambda t, b0r, st: (0, 0))] + ye_specs,
        out_specs=pl.BlockSpec((tt, D_MODEL), lambda t, b0r, st: (t, 0)),
    )
    return pl.pallas_call(
        _combine_kernel,
        grid_spec=grid_spec,
        out_shape=jax.ShapeDtypeStruct((n, D_MODEL), F32),
        compiler_params=_cparams(("arbitrary",)),
        name="combine_final_norm",
    )(b0, straddle, x2, pos_c, aff_c, g_final, *([ye] * (2 * N_EXPERTS)))


def _encoder(x, mem, p):
    batch, seq, _ = x.shape
    mem_tokens = mem.shape[1]
    n = batch * seq
    cap = CAPACITY_FACTOR * n // N_EXPERTS
    x2d = x.reshape(n, D_MODEL)

    half = DA_DIM // 2
    pos = jnp.arange(seq, dtype=F32)
    inv = ROPE_THETA ** (-jnp.arange(half, dtype=F32) / half)
    ang = pos[:, None] * inv[None, :]
    cos = jnp.cos(ang)
    sin = jnp.sin(ang)
    cos_t = jnp.tile(cos, (1, LANES // half))
    sin_t = jnp.concatenate([jnp.tile(-sin, (1, 2)), jnp.tile(sin, (1, 2))], axis=1)

    zqk, hn = _qk_proj(x2d, p["g_mix"], p["w_qk"], cos_t, sin_t, seq)
    z = _proj(hn, p["w_rest"])
    kt = _ktrans(hn, p["w_kt"])
    gates, gates_t = _gates(hn, p["w_g"], p["w_gt"], p["b_g"], p["b_gt"])

    oa = _diffattn(zqk, z, p["lamv"], p["g_da_out"], batch, seq, p["lam_init"])
    h_fw, h_bw = _mlstm(z, kt, gates, gates_t, batch, seq)

    kv = _memkv(mem.reshape(batch * mem_tokens, D_MODEL), p["g_mem"], p["w_xkv"], mem_tokens)
    x2, hffn, aff_t = _mix(x2d, oa, h_fw, h_bw, z, kv, p, batch, seq, mem_tokens)

    pos3, off3 = _select(aff_t, n, cap)
    pos_t = pos3.reshape(N_EXPERTS, n)
    xe = _dispatch(hffn, pos_t, _slot_starts(off3, T_DISP, cap), cap)
    ye = _ffn(xe, p["w_e_gate"], p["w_e_up"], p["w_e_down"], cap)

    n_blocks = cap // B_SLOT
    start = off3[:, ::T_COMB // LANES, 0]
    end = jnp.concatenate([start[:, 1:], jnp.full((N_EXPERTS, 1), cap, I32)], axis=1)
    b0 = jnp.minimum(start // B_SLOT, n_blocks - 1)
    straddle = (end > start) & ((end - 1) // B_SLOT > b0)
    y = _combine(x2, pos_t.T, aff_t.T, ye, b0.T.reshape(-1).astype(I32),
                 straddle.T.reshape(-1).astype(I32), p["g_final"], cap)
    return y.reshape(batch, seq, D_MODEL)


def kernel(x_prompt, x_sample, mem_prompt, mem_sample, g_mix, w_in, lam_q1, lam_k1, lam_q2, lam_k2,
           g_da_out, b_ml_gates, g_ml_out, w_proj_a, w_proj_b, w_out, g_xa, g_mem, w_xq, w_xkv, w_xo,
           g_ffn, w_router, w_e_gate, w_e_up, w_e_down, g_final):
    l = 0
    w = w_in[l]
    assert w_in.shape[0] == 1, "single-layer encoder"
    gate_lo = W_GATES
    gate_hi = gate_lo + N_GATES
    lane = jnp.arange(LANES)
    part = lane // (DA_DIM // 2)
    in_head = (part % 2) * DA_DIM + (part // 2) * (DA_DIM // 2) + lane % (DA_DIM // 2)
    qk_cols = (jnp.arange(QK_WIDTH // LANES)[:, None] * LANES + in_head[None, :]).reshape(-1)
    w_g = w[:, gate_lo:gate_hi]
    b = b_ml_gates[l].astype(F32)
    p = {
        "g_mix": g_mix[l][None, :],
        "w_qk": w[:, :W_DA_V][:, qk_cols].astype(BF16),
        "w_rest": jnp.concatenate([w[:, W_DA_V:W_ML_K], w[:, W_ML_V:gate_lo], w[:, gate_hi:]],
                                  axis=1).astype(BF16),
        "w_kt": w[:, W_ML_K:W_ML_V].T.astype(BF16),
        "w_g": jnp.pad(w_g, ((0, 0), (0, LANES - N_GATES))).astype(BF16),
        "w_gt": w_g.T.astype(BF16),
        "b_g": jnp.pad(b, (0, LANES - N_GATES))[None, :],
        "b_gt": b[:, None],
        "lamv": jnp.stack([lam_q1[l], lam_k1[l], lam_q2[l], lam_k2[l]]).astype(F32),
        "lam_init": 0.8 - 0.6 * math.exp(-0.3 * l),
        "g_da_out": g_da_out[l][None, :],
        "g_ml_out": g_ml_out[l][None, :],
        "w_proj_a": w_proj_a[l].astype(BF16),
        "w_proj_b": w_proj_b[l].astype(BF16),
        "w_out": w_out[l].astype(BF16),
        "g_xa": g_xa[l][None, :],
        "g_mem": g_mem[l][None, :],
        "w_xq": w_xq[l].astype(BF16),
        "w_xkv": w_xkv[l].astype(BF16),
        "w_xo": w_xo[l].astype(BF16),
        "g_ffn": g_ffn[l][None, :],
        "w_router_t": w_router[l].T.astype(F32),
        "w_e_gate": w_e_gate[l].astype(BF16),
        "w_e_up": w_e_up[l].astype(BF16),
        "w_e_down": w_e_down[l].astype(BF16),
        "g_final": g_final[None, :],
    }
    y_prompt = _encoder(x_prompt, mem_prompt, p)
    y_sample = _encoder(x_sample, mem_sample, p)
    return (y_prompt, y_sample)
```

```python
import functools
import math

import jax
import jax.numpy as jnp
from jax import lax
from jax.experimental import pallas as pl
from jax.experimental.pallas import tpu as pltpu

F32 = jnp.float32
BF16 = jnp.bfloat16
I32 = jnp.int32

D_MODEL = 1024
DA_HEADS = 8
DA_DIM = 64
DA_HEAD_W = 2 * DA_DIM
ML_HEADS = 4
ML_DIM = 256
XA_HEADS = 4
XA_DIM = D_MODEL // XA_HEADS
N_EXPERTS = 16
CAPACITY_FACTOR = 2
D_EXPERT = 2048
EPS = 1e-6
ROPE_THETA = 10000.0
N_GATES = 4 * ML_HEADS

QK_Q = 0
QK_K = 1024
QK_WIDTH = 2048
Z_DA_V = 0
Z_ML_Q = 1024
Z_ML_V = 2048
Z_ML_O = 3072
Z_BR_G = 4096
Z_WIDTH = 6144
W_DA_V = 2048
W_ML_K = 4096
W_ML_V = 5120
W_GATES = 7168

LANES = 128
VMEM_LIMIT = 56 * 1024 * 1024

TM_PROJ = 1024
TN_PROJ = 1024
TN_REST = 2048
TQ_ATT = 2048
TK_ATT = 512
ATT_UNROLL = 16
L_MLSTM = 256
TM_MIX = 512
T_DISP = 256
DISP_BLOCK = 128
DISP_SPAN = 3
DISP_RING = 4
B_SLOT = 256
TM_FFN = 512
F_CHUNK = 512
T_COMB = 256
COMB_STRIP = 256


def _cparams(sem):
    return pltpu.CompilerParams(dimension_semantics=sem, vmem_limit_bytes=VMEM_LIMIT)


def _rms(xf, g):
    return xf * lax.rsqrt(jnp.mean(xf * xf, axis=-1, keepdims=True) + EPS) * g


def _qk_proj_kernel(x_ref, g_ref, w_ref, cos_ref, sin_ref, z_ref, hn_ref, hs_ref, *, n_q_tiles):
    j = pl.program_id(1)

    @pl.when(j == 0)
    def _():
        hb = _rms(x_ref[...], g_ref[...]).astype(BF16)
        hs_ref[...] = hb
        hn_ref[...] = hb

    scale = jnp.where(j < n_q_tiles, DA_DIM ** -0.5 * math.log2(math.e), 1.0)
    acc = jnp.dot(hs_ref[...], w_ref[...], preferred_element_type=F32) * scale
    cos = cos_ref[...]
    sin = sin_ref[...]
    for hh in range(acc.shape[1] // LANES):
        a = acc[:, hh * LANES:(hh + 1) * LANES]
        z_ref[hh] = (a * cos + pltpu.roll(a, LANES // 2, 1) * sin).astype(BF16)


def _qk_proj(x2d, g_mix, w_qk, cos_t, sin_t, seq):
    n = x2d.shape[0]
    tm, tn = TM_PROJ, TN_PROJ
    n_seq_tiles = seq // tm
    return pl.pallas_call(
        functools.partial(_qk_proj_kernel, n_q_tiles=D_MODEL // tn),
        grid=(n // tm, QK_WIDTH // tn),
        in_specs=[
            pl.BlockSpec((tm, D_MODEL), lambda i, j: (i, 0)),
            pl.BlockSpec((1, D_MODEL), lambda i, j: (0, 0)),
            pl.BlockSpec((D_MODEL, tn), lambda i, j: (0, j)),
            pl.BlockSpec((tm, LANES), lambda i, j: (i % n_seq_tiles, 0)),
            pl.BlockSpec((tm, LANES), lambda i, j: (i % n_seq_tiles, 0)),
        ],
        out_specs=[
            pl.BlockSpec((tn // LANES, tm, LANES), lambda i, j: (j, i, 0)),
            pl.BlockSpec((tm, D_MODEL), lambda i, j: (i, 0)),
        ],
        out_shape=[jax.ShapeDtypeStruct((QK_WIDTH // LANES, n, LANES), BF16),
                   jax.ShapeDtypeStruct((n, D_MODEL), BF16)],
        scratch_shapes=[pltpu.VMEM((tm, D_MODEL), BF16)],
        compiler_params=_cparams(("parallel", "arbitrary")),
        name="inproj_qk",
    )(x2d, g_mix, w_qk, cos_t, sin_t)


def _proj_kernel(h_ref, w_ref, z_ref):
    z_ref[...] = jnp.dot(h_ref[...], w_ref[...], preferred_element_type=F32).astype(BF16)


def _proj(hn, w_rest):
    n = hn.shape[0]
    tm, tn = TM_PROJ, TN_REST
    return pl.pallas_call(
        _proj_kernel,
        grid=(n // tm, Z_WIDTH // tn),
        in_specs=[pl.BlockSpec((tm, D_MODEL), lambda i, j: (i, 0)),
                  pl.BlockSpec((D_MODEL, tn), lambda i, j: (0, j))],
        out_specs=pl.BlockSpec((tm, tn), lambda i, j: (i, j)),
        out_shape=jax.ShapeDtypeStruct((n, Z_WIDTH), BF16),
        compiler_params=_cparams(("parallel", "arbitrary")),
        name="inproj_rest",
    )(hn, w_rest)


def _ktrans_kernel(w_ref, h_ref, o_ref):
    kt = lax.dot_general(w_ref[...], h_ref[...], (((1,), (1,)), ((), ())),
                         preferred_element_type=F32)
    kt = (kt * (ML_DIM ** -0.5)).astype(BF16)
    for c in range(o_ref.shape[0]):
        o_ref[c] = kt[:, c * L_MLSTM:(c + 1) * L_MLSTM]


def _ktrans(hn, w_kt):
    n = hn.shape[0]
    tm = TM_PROJ
    return pl.pallas_call(
        _ktrans_kernel,
        grid=(n // tm,),
        in_specs=[pl.BlockSpec((ML_HEADS * ML_DIM, D_MODEL), lambda i: (0, 0)),
                  pl.BlockSpec((tm, D_MODEL), lambda i: (i, 0))],
        out_specs=pl.BlockSpec((tm // L_MLSTM, ML_HEADS * ML_DIM, L_MLSTM), lambda i: (i, 0, 0)),
        out_shape=jax.ShapeDtypeStruct((n // L_MLSTM, ML_HEADS * ML_DIM, L_MLSTM), BF16),
        compiler_params=_cparams(("parallel",)),
        name="ml_k_transposed",
    )(w_kt, hn)


def _log_sigmoid(x):
    return jnp.minimum(x, 0.0) - jnp.log1p(jnp.exp(-jnp.abs(x)))


def _split3(x):
    hi = x.astype(BF16)
    r = x - hi.astype(F32)
    mid = r.astype(BF16)
    lo = (r - mid.astype(F32)).astype(BF16)
    return hi, mid, lo


def _gates_kernel(h_ref, w_ref, wt_ref, b_ref, bt_ref, g_ref, gt_ref):
    h = h_ref[...]
    g = jnp.dot(h, w_ref[...], preferred_element_type=F32) + b_ref[...]
    gt = lax.dot_general(wt_ref[...], h, (((1,), (1,)), ((), ())),
                         preferred_element_type=F32) + bt_ref[...]
    L = L_MLSTM
    n_in = 2 * ML_HEADS
    n_fw = 3 * ML_HEADS
    lane = lax.broadcasted_iota(I32, (L, LANES), 1)
    sub = lax.broadcasted_iota(I32, (N_GATES, L), 0)
    ri = lax.broadcasted_iota(I32, (L, L), 0)
    ci = lax.broadcasted_iota(I32, (L, L), 1)
    lower = jnp.where(ci <= ri, 1.0, 0.0).astype(BF16)
    upper = jnp.where(ci >= ri, 1.0, 0.0).astype(BF16)
    nt = (((1,), (1,)), ((), ()))
    for c in range(g.shape[0] // L):
        gc = g[c * L:(c + 1) * L, :]
        pre = jnp.zeros((L, LANES), F32)
        suf = jnp.zeros((L, LANES), F32)
        for piece in _split3(jnp.where(lane >= n_in, _log_sigmoid(gc), 0.0)):
            pre = pre + jnp.dot(lower, piece, preferred_element_type=F32)
            suf = suf + jnp.dot(upper, piece, preferred_element_type=F32)
        g_ref[c * L:(c + 1) * L, :] = jnp.where(lane < n_in, gc, jnp.where(lane < n_fw, pre, suf))

        gtc = gt[:, c * L:(c + 1) * L]
        pre = jnp.zeros((N_GATES, L), F32)
        suf = jnp.zeros((N_GATES, L), F32)
        for piece in _split3(jnp.where(sub >= n_in, _log_sigmoid(gtc), 0.0)):
            pre = pre + lax.dot_general(piece, lower, nt, preferred_element_type=F32)
            suf = suf + lax.dot_general(piece, upper, nt, preferred_element_type=F32)
        gt_ref[:, c * L:(c + 1) * L] = jnp.where(sub < n_in, gtc, jnp.where(sub < n_fw, pre, suf))


def _gates(hn, w_g, w_gt, b_g, b_gt):
    n = hn.shape[0]
    tm = TM_PROJ
    return pl.pallas_call(
        _gates_kernel,
        grid=(n // tm,),
        in_specs=[pl.BlockSpec((tm, D_MODEL), lambda i: (i, 0)),
                  pl.BlockSpec((D_MODEL, LANES), lambda i: (0, 0)),
                  pl.BlockSpec((N_GATES, D_MODEL), lambda i: (0, 0)),
                  pl.BlockSpec((1, LANES), lambda i: (0, 0)),
                  pl.BlockSpec((N_GATES, 1), lambda i: (0, 0))],
        out_specs=[pl.BlockSpec((tm, LANES), lambda i: (i, 0)),
                   pl.BlockSpec((N_GATES, tm), lambda i: (0, i))],
        out_shape=[jax.ShapeDtypeStruct((n, LANES), F32),
                   jax.ShapeDtypeStruct((N_GATES, n), F32)],
        compiler_params=_cparams(("parallel",)),
        name="ml_gates",
    )(hn, w_g, w_gt, b_g, b_gt)


def _diffattn_kernel(q_ref, k_ref, v_ref, lamv_ref, g_ref, o_ref, va_ref, *, seq, lam_init):
    tq = q_ref.shape[0]
    tk = min(TK_ATT, seq)
    n_chunks = seq // tk
    unroll = min(n_chunks, ATT_UNROLL)

    @pl.when(pl.program_id(2) == 0)
    def _():
        va_ref[:, :DA_HEAD_W] = v_ref[...]
        va_ref[:, DA_HEAD_W:] = jnp.ones((seq, DA_HEAD_W), BF16)

    lv = lamv_ref[...]
    lam = (jnp.exp(jnp.sum(lv[0:1, :] * lv[1:2, :], axis=-1, keepdims=True))
           - jnp.exp(jnp.sum(lv[2:3, :] * lv[3:4, :], axis=-1, keepdims=True)) + lam_init)

    n_sub = 2 if (n_chunks == unroll and tq % 2048 == 0) else 1
    sub = tq // n_sub
    for r in range(n_sub):
        q = q_ref[r * sub:(r + 1) * sub, :]
        lane = lax.broadcasted_iota(I32, q.shape, 1)
        zero = jnp.zeros_like(q)
        in_map0 = ((lane // (DA_DIM // 2)) % 2) == 0
        qs = (jnp.where(in_map0, q, zero), jnp.where(in_map0, zero, q))

        def chunk(start, carry, qs=qs):
            kc = k_ref[pl.ds(start, tk), :]
            vc = va_ref[pl.ds(start, tk), :]
            out = []
            for qm, (m, l, a) in zip(qs, carry):
                s = lax.dot_general(qm, kc, (((1,), (1,)), ((), ())), preferred_element_type=F32)
                mn = jnp.maximum(m, jnp.max(s, axis=-1, keepdims=True))
                alpha = jnp.exp2(m - mn)
                p = jnp.exp2((s - mn).astype(BF16))
                pv = jnp.dot(p, vc, preferred_element_type=F32)
                out.append((mn, alpha * l + pv[:, DA_HEAD_W:],
                            alpha * a + pv[:, :DA_HEAD_W]))
            return tuple(out)

        def body(c, carry, chunk=chunk):
            for u in range(unroll):
                carry = chunk(pl.multiple_of((c * unroll + u) * tk, tk), carry)
            return carry

        init = (jnp.full((sub, 1), -jnp.inf, F32), jnp.zeros((sub, DA_HEAD_W), F32),
                jnp.zeros((sub, DA_HEAD_W), F32))
        (m0, l0, a0), (m1, l1, a1) = lax.fori_loop(0, n_chunks // unroll, body, (init, init))
        o = a0 * (1.0 / l0) - lam * (a1 * (1.0 / l1))
        o_ref[r * sub:(r + 1) * sub, :] = (_rms(o, g_ref[...]) * (1.0 - lam_init)).astype(BF16)


def _diffattn(zqk, z, lamv, g_da, batch, seq, lam_init):
    n = z.shape[0]
    tq = min(TQ_ATT, seq)
    nq = seq // tq
    hq = QK_Q // DA_HEAD_W
    hk = QK_K // DA_HEAD_W
    hv = Z_DA_V // DA_HEAD_W
    return pl.pallas_call(
        functools.partial(_diffattn_kernel, seq=seq, lam_init=lam_init),
        grid=(batch, DA_HEADS, nq),
        in_specs=[
            pl.BlockSpec((None, tq, DA_HEAD_W), lambda b, h, i: (hq + h, b * nq + i, 0)),
            pl.BlockSpec((None, seq, DA_HEAD_W), lambda b, h, i: (hk + h, b, 0)),
            pl.BlockSpec((seq, DA_HEAD_W), lambda b, h, i: (b, hv + h)),
            pl.BlockSpec((4, DA_DIM), lambda b, h, i: (0, 0)),
            pl.BlockSpec((1, DA_HEAD_W), lambda b, h, i: (0, 0)),
        ],
        out_specs=pl.BlockSpec((tq, DA_HEAD_W), lambda b, h, i: (b * nq + i, h)),
        out_shape=jax.ShapeDtypeStruct((n, DA_HEADS * DA_HEAD_W), BF16),
        scratch_shapes=[pltpu.VMEM((seq, 2 * DA_HEAD_W), BF16)],
        compiler_params=_cparams(("parallel", "parallel", "arbitrary")),
        name="diff_attention",
    )(zqk, zqk, z, lamv, g_da)


def _mlstm_chunk(q, kt, v, g, gt, cta_ref, m_ref, h_ref, head, reverse):
    L = q.shape[0]
    col_i = head + (ML_HEADS if reverse else 0)
    col_f = col_i + 2 * ML_HEADS
    lane = lax.broadcasted_iota(I32, g.shape, 1)
    sub = lax.broadcasted_iota(I32, gt.shape, 0)
    i_col = jnp.sum(jnp.where(lane == col_i, g, 0.0), axis=1, keepdims=True)
    g_col = jnp.sum(jnp.where(lane == col_f, g, 0.0), axis=1, keepdims=True)
    i_row = jnp.sum(jnp.where(sub == col_i, gt, 0.0), axis=0, keepdims=True)
    g_row = jnp.sum(jnp.where(sub == col_f, gt, 0.0), axis=0, keepdims=True)

    ri = lax.broadcasted_iota(I32, (L, L), 0)
    ci = lax.broadcasted_iota(I32, (L, L), 1)
    tri = (ci >= ri) if reverse else (ci <= ri)
    last = 0 if reverse else L - 1
    lane_l = lax.broadcasted_iota(I32, (1, L), 1)
    g_last = jnp.sum(jnp.where(lane_l == last, g_row, 0.0), axis=1, keepdims=True)

    m_prev = m_ref[...]
    dmat = jnp.where(tri, g_col + (i_row - g_row), -jnp.inf)
    a_col = g_col + m_prev
    mj = jnp.maximum(a_col, jnp.max(dmat, axis=1, keepdims=True))
    w_inter = jnp.exp(a_col - mj)
    s = jnp.dot(q, kt, preferred_element_type=F32) * jnp.exp(dmat - mj)
    ones = jnp.ones((L, LANES), BF16)
    cta = cta_ref[...]
    num = (w_inter * jnp.dot(q, cta.astype(BF16), preferred_element_type=F32)
           + jnp.dot(s.astype(BF16), jnp.concatenate([v, ones], axis=1), preferred_element_type=F32))
    inv = 1.0 / jnp.maximum(jnp.abs(num[:, ML_DIM:]), jnp.exp(-mj))
    h_ref[...] = (num[:, :ML_DIM] * jnp.concatenate([inv] * (ML_DIM // LANES), axis=1)).astype(BF16)

    dec_row = g_last - g_row + i_row
    dec_col = g_last - g_col + i_col
    m_new = jnp.maximum(g_last + m_prev, jnp.max(dec_row, axis=1, keepdims=True))
    wk_col = jnp.exp(dec_col - m_new)
    carry_scale = jnp.exp(g_last + m_prev - m_new)
    wv = jnp.concatenate([(wk_col * v.astype(F32)).astype(BF16),
                          jnp.broadcast_to(wk_col, (L, LANES)).astype(BF16)], axis=1)
    cta_ref[...] = carry_scale * cta + jnp.dot(kt, wv, preferred_element_type=F32)
    m_ref[...] = m_new


def _mlstm_kernel(qf_ref, ktf_ref, vf_ref, gf_ref, gtf_ref, qb_ref, ktb_ref, vb_ref, gb_ref, gtb_ref,
                  hf_ref, hb_ref, ctf_ref, mf_ref, ctb_ref, mb_ref):
    @pl.when(pl.program_id(1) == 0)
    def _():
        ctf_ref[...] = jnp.zeros_like(ctf_ref)
        mf_ref[...] = jnp.zeros_like(mf_ref)
        ctb_ref[...] = jnp.zeros_like(ctb_ref)
        mb_ref[...] = jnp.zeros_like(mb_ref)

    for head in range(ML_HEADS):
        sl = slice(head * ML_DIM, (head + 1) * ML_DIM)
        _mlstm_chunk(qf_ref[:, sl], ktf_ref[sl, :], vf_ref[:, sl], gf_ref[...], gtf_ref[...],
                     ctf_ref.at[head], mf_ref.at[head], hf_ref.at[:, sl], head, False)
        _mlstm_chunk(qb_ref[:, sl], ktb_ref[sl, :], vb_ref[:, sl], gb_ref[...], gtb_ref[...],
                     ctb_ref.at[head], mb_ref.at[head], hb_ref.at[:, sl], head, True)


def _mlstm(z, kt, gates, gates_t, batch, seq):
    n = z.shape[0]
    L = L_MLSTM
    nc = seq // L
    width = ML_HEADS * ML_DIM
    cq = Z_ML_Q // width
    cv = Z_ML_V // width

    def specs(tix):
        return [
            pl.BlockSpec((L, width), lambda b, t: (b * nc + tix(t), cq)),
            pl.BlockSpec((None, width, L), lambda b, t: (b * nc + tix(t), 0, 0)),
            pl.BlockSpec((L, width), lambda b, t: (b * nc + tix(t), cv)),
            pl.BlockSpec((L, LANES), lambda b, t: (b * nc + tix(t), 0)),
            pl.BlockSpec((N_GATES, L), lambda b, t: (0, b * nc + tix(t))),
        ]

    fwd = lambda t: t
    bwd = lambda t: nc - 1 - t
    out = jax.ShapeDtypeStruct((n, width), BF16)
    state = [pltpu.VMEM((ML_HEADS, ML_DIM, ML_DIM + LANES), F32), pltpu.VMEM((ML_HEADS, 1, 1), F32)]
    return pl.pallas_call(
        _mlstm_kernel,
        grid=(batch, nc),
        in_specs=specs(fwd) + specs(bwd),
        out_specs=[pl.BlockSpec((L, width), lambda b, t: (b * nc + fwd(t), 0)),
                   pl.BlockSpec((L, width), lambda b, t: (b * nc + bwd(t), 0))],
        out_shape=[out, out],
        scratch_shapes=state + state,
        compiler_params=_cparams(("parallel", "arbitrary")),
        name="mlstm",
    )(z, kt, z, gates, gates_t, z, kt, z, gates, gates_t)


def _memkv_kernel(m_ref, g_ref, w_ref, o_ref):
    mn = _rms(m_ref[...], g_ref[...]).astype(BF16)
    o_ref[...] = jnp.dot(mn, w_ref[...], preferred_element_type=F32).astype(BF16)


def _memkv(mem2d, g_mem, w_xkv, mem_tokens):
    n = mem2d.shape[0]
    return pl.pallas_call(
        _memkv_kernel,
        grid=(n // mem_tokens,),
        in_specs=[pl.BlockSpec((mem_tokens, D_MODEL), lambda b: (b, 0)),
                  pl.BlockSpec((1, D_MODEL), lambda b: (0, 0)),
                  pl.BlockSpec((D_MODEL, 2 * D_MODEL), lambda b: (0, 0))],
        out_specs=pl.BlockSpec((mem_tokens, 2 * D_MODEL), lambda b: (b, 0)),
        out_shape=jax.ShapeDtypeStruct((n, 2 * D_MODEL), BF16),
        compiler_params=_cparams(("parallel",)),
        name="mem_kv",
    )(mem2d, g_mem, w_xkv)


def _sigmoid(x):
    return 1.0 / (1.0 + jnp.exp(-x))


def _mix_kernel(x_ref, oa_ref, hf_ref, hb_ref, mo_ref, bga_ref, bgb_ref, kv_ref,
                gml_ref, gxa_ref, gffn_ref, wpa_ref, wpb_ref, wo_ref, wxq_ref, wxo_ref, wrt_ref,
                x2_ref, hn_ref, afft_ref):
    hsum = hf_ref[...].astype(F32) + hb_ref[...].astype(F32)
    gml = gml_ref[...]
    parts = []
    for hh in range(ML_HEADS):
        sl = slice(hh * ML_DIM, (hh + 1) * ML_DIM)
        parts.append(_rms(hsum[:, sl], gml[:, sl]))
    hb = jnp.concatenate(parts, axis=1)
    hb = hb * _sigmoid(mo_ref[...].astype(F32))
    y_a = jnp.dot(oa_ref[...], wpa_ref[...], preferred_element_type=F32)
    y_b = jnp.dot(hb.astype(BF16), wpb_ref[...], preferred_element_type=F32)
    merged = (_sigmoid(bga_ref[...].astype(F32)) * y_a
              + _sigmoid(bgb_ref[...].astype(F32)) * y_b)
    x1 = x_ref[...] + jnp.dot(merged.astype(BF16), wo_ref[...], preferred_element_type=F32)

    hq = _rms(x1, gxa_ref[...]).astype(BF16)
    qx = (jnp.dot(hq, wxq_ref[...], preferred_element_type=F32) * (XA_DIM ** -0.5)).astype(BF16)
    kv = kv_ref[...]
    outs = []
    for hh in range(XA_HEADS):
        sl = slice(hh * XA_DIM, (hh + 1) * XA_DIM)
        kh = kv[:, sl]
        vh = kv[:, D_MODEL + hh * XA_DIM:D_MODEL + (hh + 1) * XA_DIM]
        s = lax.dot_general(qx[:, sl], kh, (((1,), (1,)), ((), ())), preferred_element_type=F32)
        p = jnp.exp(s - jnp.max(s, axis=-1, keepdims=True))
        p = p / jnp.sum(p, axis=-1, keepdims=True)
        outs.append(jnp.dot(p.astype(BF16), vh, preferred_element_type=F32).astype(BF16))
    o = jnp.concatenate(outs, axis=1)
    x2 = x1 + jnp.dot(o, wxo_ref[...], preferred_element_type=F32)
    x2_ref[...] = x2

    hn = _rms(x2, gffn_ref[...])
    hn_ref[...] = hn.astype(BF16)
    logits_t = lax.dot_general(wrt_ref[...], hn, (((1,), (1,)), ((), ())),
                               precision=lax.Precision.HIGHEST, preferred_element_type=F32)
    e = jnp.exp(logits_t - jnp.max(logits_t, axis=0, keepdims=True))
    afft_ref[...] = e / jnp.sum(e, axis=0, keepdims=True)


def _mix(x2d, oa, h_fw, h_bw, z, kv, p, batch, seq, mem_tokens):
    n = x2d.shape[0]
    tm = TM_MIX
    nt = seq // tm
    cmo = Z_ML_O // D_MODEL
    cbg = Z_BR_G // D_MODEL
    full = lambda r, c: pl.BlockSpec((r, c), lambda i: (0, 0), pipeline_mode=pl.Buffered(1))
    return pl.pallas_call(
        _mix_kernel,
        grid=(n // tm,),
        in_specs=[
            pl.BlockSpec((tm, D_MODEL), lambda i: (i, 0)),
            pl.BlockSpec((tm, D_MODEL), lambda i: (i, 0)),
            pl.BlockSpec((tm, D_MODEL), lambda i: (i, 0)),
            pl.BlockSpec((tm, D_MODEL), lambda i: (i, 0)),
            pl.BlockSpec((tm, D_MODEL), lambda i: (i, cmo)),
            pl.BlockSpec((tm, D_MODEL), lambda i: (i, cbg)),
            pl.BlockSpec((tm, D_MODEL), lambda i: (i, cbg + 1)),
            pl.BlockSpec((mem_tokens, 2 * D_MODEL), lambda i: (i // nt, 0)),
            full(1, D_MODEL), full(1, D_MODEL), full(1, D_MODEL),
            full(D_MODEL, D_MODEL), full(D_MODEL, D_MODEL), full(D_MODEL, D_MODEL),
            full(D_MODEL, D_MODEL), full(D_MODEL, D_MODEL),
            full(N_EXPERTS, D_MODEL),
        ],
        out_specs=[pl.BlockSpec((tm, D_MODEL), lambda i: (i, 0)),
                   pl.BlockSpec((tm, D_MODEL), lambda i: (i, 0)),
                   pl.BlockSpec((N_EXPERTS, tm), lambda i: (0, i))],
        out_shape=[jax.ShapeDtypeStruct((n, D_MODEL), F32),
                   jax.ShapeDtypeStruct((n, D_MODEL), BF16),
                   jax.ShapeDtypeStruct((N_EXPERTS, n), F32)],
        compiler_params=_cparams(("parallel",)),
        name="mix_xattn_router",
    )(x2d, oa, h_fw, h_bw, z, z, z, kv,
      p["g_ml_out"], p["g_xa"], p["g_ffn"], p["w_proj_a"], p["w_proj_b"], p["w_out"],
      p["w_xq"], p["w_xo"], p["w_router_t"])


def _prefix_incl(mask_f, upper, ones, lower):
    mb = mask_f.astype(BF16)
    lane_incl = jnp.dot(mb, upper, preferred_element_type=F32)
    row_tot = jnp.dot(mb, ones, preferred_element_type=F32)
    row_off = jnp.dot(lower, row_tot.astype(BF16), preferred_element_type=F32)
    return lane_incl + row_off, row_off


def _select_kernel(a_ref, pos_ref, off_ref, *, cap):
    bits = pltpu.bitcast(a_ref[...], I32)
    rows = bits.shape[1]

    def body(i, thr):
        cand = thr | jnp.left_shift(jnp.int32(1), 30 - i)
        cnt = jnp.sum(jnp.where(bits >= cand, 1.0, 0.0), axis=(1, 2), keepdims=True)
        return jnp.where(cnt >= cap, cand, thr)

    thr_all = lax.fori_loop(0, 31, body, jnp.zeros((N_EXPERTS, 1, 1), I32))

    li = lax.broadcasted_iota(I32, (LANES, LANES), 0)
    lj = lax.broadcasted_iota(I32, (LANES, LANES), 1)
    upper = (li <= lj).astype(BF16)
    ones = jnp.ones((LANES, LANES), BF16)
    ri = lax.broadcasted_iota(I32, (rows, rows), 0)
    rj = lax.broadcasted_iota(I32, (rows, rows), 1)
    lower = (rj < ri).astype(BF16)

    for e in range(N_EXPERTS):
        be = bits[e]
        thr = thr_all[e]
        gt = be > thr
        eq = be == thr
        need = cap - jnp.sum(jnp.where(gt, 1.0, 0.0), keepdims=True)
        eq_rank, _ = _prefix_incl(jnp.where(eq, 1.0, 0.0), upper, ones, lower)
        sel = gt | (eq & (eq_rank <= need))
        incl, row_off = _prefix_incl(jnp.where(sel, 1.0, 0.0), upper, ones, lower)
        pos_ref[e] = jnp.where(sel, incl - 1.0, -1.0).astype(I32)
        off_ref[e] = row_off.astype(I32)


def _select(aff_t, n_tokens, cap):
    rows = n_tokens // LANES
    a3 = aff_t.reshape(N_EXPERTS, rows, LANES)
    spec = pl.BlockSpec((N_EXPERTS, rows, LANES), lambda i: (0, 0, 0))
    return pl.pallas_call(
        functools.partial(_select_kernel, cap=float(cap)),
        grid=(1,),
        in_specs=[spec],
        out_specs=[spec, spec],
        out_shape=[jax.ShapeDtypeStruct((N_EXPERTS, rows, LANES), I32),
                   jax.ShapeDtypeStruct((N_EXPERTS, rows, LANES), I32)],
        compiler_params=_cparams(("arbitrary",)),
        name="expert_choice_select",
    )(a3)


def _dispatch_kernel(s_ref, x_ref, pos_ref, xe_ref, buf_ref, stage_ref, sem_ref, *, n_blocks):
    t = pl.program_id(0)
    blk_rows = DISP_BLOCK
    shift = blk_rows.bit_length() - 1
    x = x_ref[...]
    tt = x.shape[0]
    slot = lax.broadcasted_iota(I32, (blk_rows, tt), 0)
    row = lax.broadcasted_iota(I32, (blk_rows, 1), 0)

    def block_copy(e, blk):
        first = blk * blk_rows
        if not isinstance(first, int):
            first = pl.multiple_of(first, blk_rows)
        ring = blk & (DISP_RING - 1)
        return pltpu.make_async_copy(buf_ref.at[e, ring], xe_ref.at[e, pl.ds(first, blk_rows)],
                                     sem_ref.at[e, ring])

    def gather_rows(e, blk):
        rel = pos_ref[e:e + 1, :] - blk * blk_rows
        onehot = jnp.where(slot == rel, 1.0, 0.0).astype(BF16)
        return jnp.dot(onehot, x, preferred_element_type=F32).astype(BF16)

    def first_write(e, blk, rows):
        @pl.when(blk >= DISP_RING)
        def _():
            block_copy(e, blk - DISP_RING).wait()
        buf_ref[e, blk & (DISP_RING - 1)] = rows

    for e in range(N_EXPERTS):
        stage_ref[e] = gather_rows(e, lax.shift_right_logical(s_ref[t * N_EXPERTS + e], shift))

    for e in range(N_EXPERTS):
        s0 = s_ref[t * N_EXPERTS + e]
        s1 = s_ref[(t + 1) * N_EXPERTS + e]
        b0 = lax.shift_right_logical(s0, shift)
        r0 = s0 - b0 * blk_rows

        @pl.when(s1 > s0)
        def _(e=e, s1=s1, b0=b0, r0=r0):
            @pl.when(r0 == 0)
            def _():
                first_write(e, b0, stage_ref[e])

            @pl.when(r0 != 0)
            def _():
                ring = b0 & (DISP_RING - 1)
                mine = (row >= r0) & (row < s1 - b0 * blk_rows)
                buf_ref[e, ring] = jnp.where(mine, stage_ref[e], buf_ref[e, ring])

            for k in range(1, DISP_SPAN):
                @pl.when(s1 > (b0 + k) * blk_rows)
                def _(k=k):
                    first_write(e, b0 + k, gather_rows(e, b0 + k))

            for k in range(DISP_SPAN - 1):
                @pl.when(s1 >= (b0 + k + 1) * blk_rows)
                def _(k=k):
                    block_copy(e, b0 + k).start()

    @pl.when(t == pl.num_programs(0) - 1)
    def _():
        for e in range(N_EXPERTS):
            for blk in range(max(n_blocks - DISP_RING, 0), n_blocks):
                block_copy(e, blk).wait()


def _dispatch(hn, pos_t, slot_starts, cap):
    n = hn.shape[0]
    tt = T_DISP
    assert DISP_SPAN == tt // DISP_BLOCK + 1 and DISP_RING >= DISP_SPAN + 1
    n_blocks = cap // DISP_BLOCK
    grid_spec = pltpu.PrefetchScalarGridSpec(
        num_scalar_prefetch=1,
        grid=(n // tt,),
        in_specs=[pl.BlockSpec((tt, D_MODEL), lambda t, s: (t, 0)),
                  pl.BlockSpec((N_EXPERTS, tt), lambda t, s: (0, t))],
        out_specs=pl.BlockSpec(memory_space=pl.ANY),
        scratch_shapes=[pltpu.VMEM((N_EXPERTS, DISP_RING, DISP_BLOCK, D_MODEL), BF16),
                        pltpu.VMEM((N_EXPERTS, DISP_BLOCK, D_MODEL), BF16),
                        pltpu.SemaphoreType.DMA((N_EXPERTS, DISP_RING))],
    )
    return pl.pallas_call(
        functools.partial(_dispatch_kernel, n_blocks=n_blocks),
        grid_spec=grid_spec,
        out_shape=jax.ShapeDtypeStruct((N_EXPERTS, cap, D_MODEL), BF16),
        compiler_params=_cparams(("arbitrary",)),
        name="dispatch",
    )(slot_starts, hn, pos_t)


def _slot_starts(row_off, tile, cap):
    start = row_off[:, ::tile // LANES, 0]
    start = jnp.concatenate([start, jnp.full((N_EXPERTS, 1), cap, I32)], axis=1)
    return start.T.reshape(-1).astype(I32)


def _ffn_kernel(x_ref, wg_ref, wu_ref, wd_ref, o_ref):
    x = x_ref[...]
    acc = jnp.zeros((x.shape[0], D_MODEL), F32)
    for c in range(D_EXPERT // F_CHUNK):
        sl = slice(c * F_CHUNK, (c + 1) * F_CHUNK)
        gte = jnp.dot(x, wg_ref[:, sl], preferred_element_type=F32)
        up = jnp.dot(x, wu_ref[:, sl], preferred_element_type=F32)
        hid = gte * (1.0 / (1.0 + jnp.exp(-gte))) * up
        acc = acc + jnp.dot(hid.astype(BF16), wd_ref[sl, :], preferred_element_type=F32)
    o_ref[...] = acc.astype(BF16)


def _ffn(xe, w_gate, w_up, w_down, cap):
    tm = min(TM_FFN, cap)
    return pl.pallas_call(
        _ffn_kernel,
        grid=(N_EXPERTS, cap // tm),
        in_specs=[pl.BlockSpec((None, tm, D_MODEL), lambda e, i: (e, i, 0)),
                  pl.BlockSpec((None, D_MODEL, D_EXPERT), lambda e, i: (e, 0, 0)),
                  pl.BlockSpec((None, D_MODEL, D_EXPERT), lambda e, i: (e, 0, 0)),
                  pl.BlockSpec((None, D_EXPERT, D_MODEL), lambda e, i: (e, 0, 0))],
        out_specs=pl.BlockSpec((None, tm, D_MODEL), lambda e, i: (e, i, 0)),
        out_shape=jax.ShapeDtypeStruct((N_EXPERTS, cap, D_MODEL), BF16),
        compiler_params=_cparams(("parallel", "arbitrary")),
        name="expert_ffn",
    )(xe, w_gate, w_up, w_down)


def _combine_kernel(b0_ref, st_ref, x_ref, pos_ref, aff_ref, g_ref, *rest):
    ye_refs = rest[:2 * N_EXPERTS]
    o_ref = rest[2 * N_EXPERTS]
    t = pl.program_id(0)
    tt = pos_ref.shape[0]
    slot = lax.broadcasted_iota(I32, (tt, B_SLOT), 1)

    def gated_onehot(e, kk):
        rel = pos_ref[:, e:e + 1] - (b0_ref[t * N_EXPERTS + e] + kk) * B_SLOT
        return jnp.where(slot == rel, aff_ref[:, e:e + 1], 0.0).astype(BF16)

    onehots = [gated_onehot(e, 0) for e in range(N_EXPERTS)]
    for c in range(D_MODEL // COMB_STRIP):
        sl = slice(c * COMB_STRIP, (c + 1) * COMB_STRIP)
        acc = x_ref[:, sl]
        for e in range(N_EXPERTS):
            acc = acc + jnp.dot(onehots[e], ye_refs[2 * e][:, sl], preferred_element_type=F32)
        o_ref[:, sl] = acc
    for e in range(N_EXPERTS):
        @pl.when(st_ref[t * N_EXPERTS + e] != 0)
        def _(e=e):
            o_ref[...] += jnp.dot(gated_onehot(e, 1), ye_refs[2 * e + 1][...],
                                  preferred_element_type=F32)
    o_ref[...] = _rms(o_ref[...], g_ref[...])


def _combine(x2, pos_c, aff_c, ye, b0, straddle, g_final, cap):
    n = x2.shape[0]
    tt = T_COMB
    n_blocks = cap // B_SLOT

    def ye_spec(e, kk):
        return pl.BlockSpec(
            (None, B_SLOT, D_MODEL),
            lambda t, b0r, st: (e, jnp.minimum(b0r[t * N_EXPERTS + e] + kk, n_blocks - 1), 0))

    ye_specs = [ye_spec(e, kk) for e in range(N_EXPERTS) for kk in range(2)]
    grid_spec = pltpu.PrefetchScalarGridSpec(
        num_scalar_prefetch=2,
        grid=(n // tt,),
        in_specs=[pl.BlockSpec((tt, D_MODEL), lambda t, b0r, st: (t, 0)),
                  pl.BlockSpec((tt, N_EXPERTS), lambda t, b0r, st: (t, 0)),
                  pl.BlockSpec((tt, N_EXPERTS), lambda t, b0r, st: (t, 0)),
                  pl.BlockSpec((1, D_MODEL), lambda t, b0r, st: (0, 0))] + ye_specs,
        out_specs=pl.BlockSpec((tt, D_MODEL), lambda t, b0r, st: (t, 0)),
    )
    return pl.pallas_call(
        _combine_kernel,
        grid_spec=grid_spec,
        out_shape=jax.ShapeDtypeStruct((n, D_MODEL), F32),
        compiler_params=_cparams(("arbitrary",)),
        name="combine_final_norm",
    )(b0, straddle, x2, pos_c, aff_c, g_final, *([ye] * (2 * N_EXPERTS)))


def _encoder(x, mem, p):
    batch, seq, _ = x.shape
    mem_tokens = mem.shape[1]
    n = batch * seq
    cap = CAPACITY_FACTOR * n // N_EXPERTS
    x2d = x.reshape(n, D_MODEL)

    half = DA_DIM // 2
    pos = jnp.arange(seq, dtype=F32)
    inv = ROPE_THETA ** (-jnp.arange(half, dtype=F32) / half)
    ang = pos[:, None] * inv[None, :]
    cos = jnp.cos(ang)
    sin = jnp.sin(ang)
    cos_t = jnp.tile(cos, (1, LANES // half))
    sin_t = jnp.concatenate([jnp.tile(-sin, (1, 2)), jnp.tile(sin, (1, 2))], axis=1)

    zqk, hn = _qk_proj(x2d, p["g_mix"], p["w_qk"], cos_t, sin_t, seq)
    z = _proj(hn, p["w_rest"])
    kt = _ktrans(hn, p["w_kt"])
    gates, gates_t = _gates(hn, p["w_g"], p["w_gt"], p["b_g"], p["b_gt"])

    oa = _diffattn(zqk, z, p["lamv"], p["g_da_out"], batch, seq, p["lam_init"])
    h_fw, h_bw = _mlstm(z, kt, gates, gates_t, batch, seq)

    kv = _memkv(mem.reshape(batch * mem_tokens, D_MODEL), p["g_mem"], p["w_xkv"], mem_tokens)
    x2, hffn, aff_t = _mix(x2d, oa, h_fw, h_bw, z, kv, p, batch, seq, mem_tokens)

    pos3, off3 = _select(aff_t, n, cap)
    pos_t = pos3.reshape(N_EXPERTS, n)
    xe = _dispatch(hffn, pos_t, _slot_starts(off3, T_DISP, cap), cap)
    ye = _ffn(xe, p["w_e_gate"], p["w_e_up"], p["w_e_down"], cap)

    n_blocks = cap // B_SLOT
    start = off3[:, ::T_COMB // LANES, 0]
    end = jnp.concatenate([start[:, 1:], jnp.full((N_EXPERTS, 1), cap, I32)], axis=1)
    b0 = jnp.minimum(start // B_SLOT, n_blocks - 1)
    straddle = (end > start) & ((end - 1) // B_SLOT > b0)
    y = _combine(x2, pos_t.T, aff_t.T, ye, b0.T.reshape(-1).astype(I32),
                 straddle.T.reshape(-1).astype(I32), p["g_final"], cap)
    return y.reshape(batch, seq, D_MODEL)


def kernel(x_prompt, x_sample, mem_prompt, mem_sample, g_mix, w_in, lam_q1, lam_k1, lam_q2, lam_k2,
           g_da_out, b_ml_gates, g_ml_out, w_proj_a, w_proj_b, w_out, g_xa, g_mem, w_xq, w_xkv, w_xo,
           g_ffn, w_router, w_e_gate, w_e_up, w_e_down, g_final):
    l = 0
    w = w_in[l]
    assert w_in.shape[0] == 1, "single-layer encoder"
    gate_lo = W_GATES
    gate_hi = gate_lo + N_GATES
    lane = jnp.arange(LANES)
    part = lane // (DA_DIM // 2)
    in_head = (part % 2) * DA_DIM + (part // 2) * (DA_DIM // 2) + lane % (DA_DIM // 2)
    qk_cols = (jnp.arange(QK_WIDTH // LANES)[:, None] * LANES + in_head[None, :]).reshape(-1)
    w_g = w[:, gate_lo:gate_hi]
    b = b_ml_gates[l].astype(F32)
    p = {
        "g_mix": g_mix[l][None, :],
        "w_qk": w[:, :W_DA_V][:, qk_cols].astype(BF16),
        "w_rest": jnp.concatenate([w[:, W_DA_V:W_ML_K], w[:, W_ML_V:gate_lo], w[:, gate_hi:]],
                                  axis=1).astype(BF16),
        "w_kt": w[:, W_ML_K:W_ML_V].T.astype(BF16),
        "w_g": jnp.pad(w_g, ((0, 0), (0, LANES - N_GATES))).astype(BF16),
        "w_gt": w_g.T.astype(BF16),
        "b_g": jnp.pad(b, (0, LANES - N_GATES))[None, :],
        "b_gt": b[:, None],
        "lamv": jnp.stack([lam_q1[l], lam_k1[l], lam_q2[l], lam_k2[l]]).astype(F32),
        "lam_init": 0.8 - 0.6 * math.exp(-0.3 * l),
        "g_da_out": g_da_out[l][None, :],
        "g_ml_out": g_ml_out[l][None, :],
        "w_proj_a": w_proj_a[l].astype(BF16),
        "w_proj_b": w_proj_b[l].astype(BF16),
        "w_out": w_out[l].astype(BF16),
        "g_xa": g_xa[l][None, :],
        "g_mem": g_mem[l][None, :],
        "w_xq": w_xq[l].astype(BF16),
        "w_xkv": w_xkv[l].astype(BF16),
        "w_xo": w_xo[l].astype(BF16),
        "g_ffn": g_ffn[l][None, :],
        "w_router_t": w_router[l].T.astype(F32),
        "w_e_gate": w_e_gate[l].astype(BF16),
        "w_e_up": w_e_up[l].astype(BF16),
        "w_e_down": w_e_down[l].astype(BF16),
        "g_final": g_final[None, :],
    }
    y_prompt = _encoder(x_prompt, mem_prompt, p)
    y_sample = _encoder(x_sample, mem_sample, p)
    return (y_prompt, y_sample)
```
